```python
import math
import jax, jax.numpy as jnp
from jax import lax
import numpy as np

D_MODEL = 1024
BATCH = 4
SEQ = 8192
DEPTH = 2

GRID_W = 64
CTX_LEN = 256
N_MIXERS = 2
D_FF = 4 * D_MODEL
NORM_EPS = 1e-6
LRU_WIDTH = D_MODEL
LRU_HEADS = 4
LRU_BLOCK = LRU_WIDTH // LRU_HEADS
LRU_CONV = 4
LRU_CONV_LEFT = 2
LRU_C = 8.0
N_DIRS = 2
HYENA_ORDER = 2
HYENA_CONV = 3
HYENA_CONV_LEFT = 1
FILTER_BANDS = 16
FILTER_EMB = 1 + 2 * FILTER_BANDS
FILTER_HIDDEN = 64
FILTER_TARGET = 1e-2
FAST_DECAY_PCT = 0.3
SLOW_DECAY_PCT = 1.5

kernel_name = 'hybrid_rglru_hyena_diffusion_block'


def rms_norm(x, g):
    x32 = x.astype(jnp.float32)
    y = x32 * lax.rsqrt(jnp.mean(x32 * x32, axis=-1, keepdims=True) + NORM_EPS)
    return (y * g.astype(jnp.float32)).astype(x.dtype)


def modulate(x, g, shift, scale):
    return rms_norm(x, g) * (1 + scale) + shift


def squared_relu_mlp(u, w1, w2):
    return jnp.square(jax.nn.relu(u @ w1)) @ w2


def depthwise_conv(u, w, b, left):
    K = w.shape[0]
    L = u.shape[1]
    up = jnp.pad(u, ((0, 0), (left, K - 1 - left), (0, 0)))
    return sum(up[:, k:k + L] * w[k] for k in range(K)) + b


def row_conv(u, w, b, left):
    B, L, C = u.shape
    rows = L // GRID_W
    y = depthwise_conv(u.reshape(B * rows, GRID_W, C), w, b, left)
    return y.reshape(B, L, C)


def _combine(left, right):
    a1, b1 = left
    a2, b2 = right
    return a1 * a2, a2 * b1 + b2


def linear_scan(a, bx, h0):
    a_cum, h = lax.associative_scan(_combine, (a, bx), axis=1)
    return h + a_cum * h0[:, None]


def rglru_coeffs(xc, w_a, b_a, w_i, b_i, lam):
    B, L, W = xc.shape
    x32 = xc.astype(jnp.float32)
    xh = x32.reshape(B, L, LRU_HEADS, LRU_BLOCK)
    r = jax.nn.sigmoid(jnp.einsum('blhi,ehij->eblhj', xh, w_a) + b_a[:, None, None]).reshape(N_DIRS, B, L, W)
    i = jax.nn.sigmoid(jnp.einsum('blhi,ehij->eblhj', xh, w_i) + b_i[:, None, None]).reshape(N_DIRS, B, L, W)
    log_a = -LRU_C * r * jax.nn.softplus(-lam.astype(jnp.float32))[:, None, None, :]
    a = jnp.exp(log_a)
    bx = jnp.sqrt(-jnp.expm1(2.0 * log_a)) * i * x32[None]
    return a, bx


def rglru_mixer(u, u_ctx, p, want_ctx_out):
    w_in, b_in, conv_w, conv_b, w_a, b_a, w_i, b_i, lam, w_out, b_out = p
    W = LRU_WIDTH
    B = u.shape[0]
    zeros = jnp.zeros((B, W), jnp.float32)
    xc_c = depthwise_conv(u_ctx @ w_in[:, W:] + b_in[W:], conv_w, conv_b, LRU_CONV_LEFT)
    a_c, bx_c = rglru_coeffs(xc_c, w_a, b_a, w_i, b_i, lam)
    h_cf = linear_scan(a_c[0], bx_c[0], zeros)
    h_cb = linear_scan(jnp.flip(a_c[1], 1), jnp.flip(bx_c[1], 1), zeros)
    z = u @ w_in + b_in
    gate = jax.nn.gelu(z[..., :W])
    xl = row_conv(z[..., W:], conv_w, conv_b, LRU_CONV_LEFT)
    a_l, bx_l = rglru_coeffs(xl, w_a, b_a, w_i, b_i, lam)
    h_f = linear_scan(a_l[0], bx_l[0], h_cf[:, -1])
    h_b = jnp.flip(linear_scan(jnp.flip(a_l[1], 1), jnp.flip(bx_l[1], 1), h_cb[:, -1]), 1)
    y = ((h_f + h_b).astype(gate.dtype) * gate) @ w_out + b_out
    y_ctx = None
    if want_ctx_out:
        gate_c = jax.nn.gelu(u_ctx @ w_in[:, :W] + b_in[:W])
        h_c = h_cf + jnp.flip(h_cb, 1)
        y_ctx = (h_c.astype(gate_c.dtype) * gate_c) @ w_out + b_out
    return y, y_ctx


def hyena_filters(L, fw1, fb1, fw2, fb2, fw3, fb3, fw4, freq):
    t = jnp.linspace(0.0, 1.0, L, dtype=jnp.float32)[:, None]
    w = (2.0 * math.pi / L) * jnp.arange(L, dtype=jnp.float32)[:, None]
    bands = jnp.linspace(1e-4, FILTER_BANDS - 1, FILTER_BANDS, dtype=jnp.float32)
    pos = jnp.concatenate([t, jnp.cos(bands * w), -jnp.sin(bands * w)], axis=-1)
    h = jnp.sin(freq * (pos @ fw1 + fb1))
    h = jnp.sin(freq * (h @ fw2 + fb2))
    h = jnp.sin(freq * (h @ fw3 + fb3))
    h = (h @ fw4).reshape(L, N_DIRS, HYENA_ORDER, D_MODEL)
    deltas = jnp.abs(jnp.linspace(math.log(FILTER_TARGET) / SLOW_DECAY_PCT,
                                  math.log(FILTER_TARGET) / FAST_DECAY_PCT, D_MODEL, dtype=jnp.float32))
    h = h * jnp.exp(-t * deltas)[:, None, None, :]
    h = h / jnp.sum(jnp.abs(h), axis=(0, 1), keepdims=True)
    fwd, bwd = h[:, 0], h[:, 1]
    k = jnp.concatenate([fwd[:1] + bwd[:1], fwd[1:],
                         jnp.zeros((1, HYENA_ORDER, D_MODEL), h.dtype), jnp.flip(bwd[1:], 0)], axis=0)
    return jnp.fft.rfft(k, axis=0)


def long_conv(u, k_f, bias):
    L = u.shape[1]
    u32 = u.astype(jnp.float32)
    y = jnp.fft.irfft(jnp.fft.rfft(u32, n=2 * L, axis=1) * k_f, n=2 * L, axis=1)[:, :L]
    return (y + u32 * bias).astype(u.dtype)


def hyena_operator(u, p, conv_fn):
    w_in, b_in, conv_w, conv_b, fw1, fb1, fw2, fb2, fw3, fb3, fw4, freq, skip, w_out, b_out = p
    L = u.shape[1]
    z = conv_fn(u @ w_in + b_in, conv_w, conv_b, HYENA_CONV_LEFT)
    v, x1, x2 = jnp.split(z, HYENA_ORDER + 1, axis=-1)
    k_f = hyena_filters(L, fw1, fb1, fw2, fb2, fw3, fb3, fw4, freq)
    v = x1 * long_conv(v, k_f[:, 0], skip[0])
    v = x2 * long_conv(v, k_f[:, 1], skip[1])
    return v @ w_out + b_out


def _dense(key, shape, fan_in, gain=1.0):
    return (gain * fan_in ** -0.5) * jax.random.normal(key, shape, jnp.float32)


def setup_inputs(seed: int = 0) -> dict:
    key = jax.random.key(seed)
    k = list(jax.random.split(key, 40))
    D, W, F = D_MODEL, LRU_WIDTH, FILTER_HIDDEN
    n_a = (DEPTH + 1) // 2
    n_b = DEPTH // 2
    small = lambda kk, shape: 0.02 * jax.random.normal(kk, shape, jnp.float32)
    s = jax.random.uniform(k[14], (n_a, N_DIRS, W), jnp.float32, minval=0.9, maxval=0.999) ** (1.0 / LRU_C)
    return {
        'x': jax.random.normal(k[0], (BATCH, SEQ, D), jnp.float32),
        'c': jax.random.normal(k[1], (BATCH, D), jnp.float32),
        'ctx': jax.random.normal(k[2], (BATCH, CTX_LEN, D), jnp.float32),
        'c_ctx': jax.random.normal(k[3], (D,), jnp.float32),
        'ada_w': _dense(k[4], (DEPTH, D, 6 * D), D, 0.5),
        'ada_b': small(k[5], (DEPTH, 6 * D)),
        'norm_g': 1.0 + 0.1 * jax.random.normal(k[6], (DEPTH, 2, D), jnp.float32),
        'mlp_w1': _dense(k[7], (DEPTH, D, D_FF), D),
        'mlp_w2': _dense(k[8], (DEPTH, D_FF, D), D_FF),
        'lru_w_in': _dense(k[9], (n_a, D, 2 * W), D),
        'lru_b_in': small(k[10], (n_a, 2 * W)),
        'lru_conv_w': _dense(k[11], (n_a, LRU_CONV, W), LRU_CONV),
        'lru_conv_b': small(k[12], (n_a, W)),
        'lru_w_a': _dense(k[13], (n_a, N_DIRS, LRU_HEADS, LRU_BLOCK, LRU_BLOCK), LRU_BLOCK),
        'lru_b_a': small(k[15], (n_a, N_DIRS, LRU_HEADS, LRU_BLOCK)),
        'lru_w_i': _dense(k[16], (n_a, N_DIRS, LRU_HEADS, LRU_BLOCK, LRU_BLOCK), LRU_BLOCK),
        'lru_b_i': small(k[17], (n_a, N_DIRS, LRU_HEADS, LRU_BLOCK)),
        'lru_lambda': jnp.log(s) - jnp.log1p(-s),
        'lru_w_out': _dense(k[18], (n_a, W, D), W),
        'lru_b_out': small(k[19], (n_a, D)),
        'hy_w_in': _dense(k[20], (n_b, D, (HYENA_ORDER + 1) * D), D),
        'hy_b_in': small(k[21], (n_b, (HYENA_ORDER + 1) * D)),
        'hy_conv_w': _dense(k[22], (n_b, HYENA_CONV, (HYENA_ORDER + 1) * D), HYENA_CONV),
        'hy_conv_b': small(k[23], (n_b, (HYENA_ORDER + 1) * D)),
        'hy_fw1': _dense(k[24], (n_b, FILTER_EMB, F), FILTER_EMB),
        'hy_fb1': small(k[25], (n_b, F)),
        'hy_fw2': _dense(k[26], (n_b, F, F), F),
        'hy_fb2': small(k[27], (n_b, F)),
        'hy_fw3': _dense(k[28], (n_b, F, F), F),
        'hy_fb3': small(k[29], (n_b, F)),
        'hy_fw4': _dense(k[30], (n_b, F, N_DIRS * HYENA_ORDER * D), F),
        'hy_freq': 1.0 + 0.01 * jax.random.normal(k[31], (n_b, F), jnp.float32),
        'hy_skip': jax.random.normal(k[32], (n_b, HYENA_ORDER, D), jnp.float32),
        'hy_w_out': _dense(k[33], (n_b, D, D), D),
        'hy_b_out': small(k[34], (n_b, D)),
        'final_g': 1.0 + 0.1 * jax.random.normal(k[35], (D,), jnp.float32),
    }


def reference(x, c, ctx, c_ctx, ada_w, ada_b, norm_g, mlp_w1, mlp_w2,
              lru_w_in, lru_b_in, lru_conv_w, lru_conv_b, lru_w_a, lru_b_a, lru_w_i, lru_b_i,
              lru_lambda, lru_w_out, lru_b_out,
              hy_w_in, hy_b_in, hy_conv_w, hy_conv_b, hy_fw1, hy_fb1, hy_fw2, hy_fb2,
              hy_fw3, hy_fb3, hy_fw4, hy_freq, hy_skip, hy_w_out, hy_b_out, final_g):
    cond = jax.nn.silu(c)
    cond_ctx = jax.nn.silu(c_ctx)
    h_ctx = ctx
    for i in range(DEPTH):
        j = i // N_MIXERS
        is_lru = (i % N_MIXERS) == 0
        ctx_later = any(l % N_MIXERS == 0 for l in range(i + 1, DEPTH))
        mod = (cond @ ada_w[i] + ada_b[i])[:, None, :]
        sh1, sc1, g1, sh2, sc2, g2 = jnp.split(mod, 6, axis=-1)
        u = modulate(x, norm_g[i, 0], sh1, sc1)
        u_ctx = None
        mod_c = None
        if is_lru or ctx_later:
            mod_c = jnp.split(cond_ctx @ ada_w[i] + ada_b[i], 6)
            u_ctx = modulate(h_ctx, norm_g[i, 0], mod_c[0], mod_c[1])
        if is_lru:
            p = (lru_w_in[j], lru_b_in[j], lru_conv_w[j], lru_conv_b[j], lru_w_a[j], lru_b_a[j],
                 lru_w_i[j], lru_b_i[j], lru_lambda[j], lru_w_out[j], lru_b_out[j])
            y, y_ctx = rglru_mixer(u, u_ctx, p, ctx_later)
        else:
            p = (hy_w_in[j], hy_b_in[j], hy_conv_w[j], hy_conv_b[j], hy_fw1[j], hy_fb1[j],
                 hy_fw2[j], hy_fb2[j], hy_fw3[j], hy_fb3[j], hy_fw4[j], hy_freq[j], hy_skip[j],
                 hy_w_out[j], hy_b_out[j])
            y = hyena_operator(u, p, row_conv)
            y_ctx = hyena_operator(u_ctx, p, depthwise_conv) if ctx_later else None
        x = x + g1 * y
        x = x + g2 * squared_relu_mlp(modulate(x, norm_g[i, 1], sh2, sc2), mlp_w1[i], mlp_w2[i])
        if ctx_later:
            h_ctx = h_ctx + mod_c[2] * y_ctx
            h_ctx = h_ctx + mod_c[5] * squared_relu_mlp(
                modulate(h_ctx, norm_g[i, 1], mod_c[3], mod_c[4]), mlp_w1[i], mlp_w2[i])
    return rms_norm(x, final_g)
```

```python
import functools
import math

import numpy as np
import jax
import jax.numpy as jnp
from jax import lax
from jax.experimental import pallas as pl
from jax.experimental.pallas import tpu as pltpu

F32 = jnp.float32
BF16 = jnp.bfloat16
HIGHEST = lax.Precision.HIGHEST

NORM_EPS = 1e-6
GRID_W = 64
LRU_HEADS = 4
LRU_C = 8.0
LRU_CONV_LEFT = 2
HYENA_CONV_LEFT = 1
FILTER_BANDS = 16
FILTER_TARGET = 1e-2
FAST_DECAY_PCT = 0.3
SLOW_DECAY_PCT = 1.5

NSEG = 8
SUBLANES = 8
LANES = 128
VMEM_LIMIT = 58 * 1024 * 1024

DFT_N2 = 64
ROW_PAD = 8
ROW_PITCH = DFT_N2 + ROW_PAD
SPEC_PITCH = 2 * DFT_N2 + ROW_PAD


def _cparams(sem):
    return pltpu.CompilerParams(dimension_semantics=sem, vmem_limit_bytes=VMEM_LIMIT)


def _const_spec(shape):
    nd = len(shape)
    return pl.BlockSpec(shape, lambda *_: (0,) * nd, pipeline_mode=pl.Buffered(1))


def _round_up(a, m):
    return (a + m - 1) // m * m


def _rms_norm(x, g):
    ms = jnp.mean(x * x, axis=-1, keepdims=True)
    return (x * lax.rsqrt(ms + NORM_EPS)) * g


def _modulate(x, g, shift, scale):
    return _rms_norm(x, g) * (1.0 + scale) + shift


def _sigmoid(x):
    return 0.5 * jnp.tanh(0.5 * x) + 0.5


def _gelu_tanh(x):
    c = math.sqrt(2.0 / math.pi)
    return x * (0.5 * (1.0 + jnp.tanh(c * (x + 0.044715 * (x * x * x)))))


def _softplus(x):
    return jnp.maximum(x, 0.0) + jnp.log1p(jnp.exp(-jnp.abs(x)))


def _bdot(a, b):
    return jnp.dot(a, b, preferred_element_type=F32)


def _row_conv(z, w, b, left, period):
    T = z.shape[0]
    r = lax.broadcasted_iota(jnp.int32, z.shape, 0) & (period - 1)
    acc = b + w[left:left + 1] * z
    for k in range(w.shape[0]):
        o = k - left
        if o == 0:
            continue
        shifted = pltpu.roll(z, (-o) % T, 0)
        valid = (r >= -o) if o < 0 else (r < period - o)
        acc = acc + w[k:k + 1] * jnp.where(valid, shifted, 0.0)
    return acc


def _ada_kernel(c_ref, w_ref, b_ref, o_ref):
    c = c_ref[...]
    cond = c * jax.nn.sigmoid(c)
    o_ref[0] = jnp.dot(cond, w_ref[0], preferred_element_type=F32, precision=HIGHEST) + b_ref[0]


def _ada_call(cvec, ada_w, ada_b):
    depth, d, n = ada_w.shape
    tn = 1536
    return pl.pallas_call(
        _ada_kernel,
        grid=(depth, n // tn),
        in_specs=[
            pl.BlockSpec((SUBLANES, d), lambda i, j: (0, 0)),
            pl.BlockSpec((1, d, tn), lambda i, j: (i, 0, j)),
            pl.BlockSpec((1, 1, tn), lambda i, j: (i, 0, j)),
        ],
        out_specs=pl.BlockSpec((1, SUBLANES, tn), lambda i, j: (i, 0, j)),
        out_shape=jax.ShapeDtypeStruct((depth, SUBLANES, n), F32),
        compiler_params=_cparams(("parallel", "parallel")),
    )(cvec, ada_w, ada_b.reshape(depth, 1, n))


def _lru_coeffs(x, mv, ng, w_rec, b_rec, cw, cb, wg, bg, lam, ab_scr, *, period, seg_len, pitch):
    w = w_rec.shape[1]
    hb = w // LRU_HEADS
    u = _modulate(x, ng, mv[0:1], mv[1:2]).astype(BF16)
    zr = _bdot(u, w_rec) + b_rec
    xl = _row_conv(zr, cw, cb, LRU_CONV_LEFT, period)
    sp = _softplus(-lam)
    for h in range(LRU_HEADS):
        cs = slice(h * hb, (h + 1) * hb)
        xh = xl[:, cs]
        gates = _bdot(xh.astype(BF16), wg[h]) + bg[h]
        for e in range(2):
            r = _sigmoid(gates[:, (2 * e) * hb:(2 * e + 1) * hb])
            i = _sigmoid(gates[:, (2 * e + 1) * hb:(2 * e + 2) * hb])
            log_a = (-LRU_C * sp[e:e + 1, cs]) * r
            a = jnp.exp(log_a)
            th = jnp.tanh(log_a)
            bx = jnp.sqrt((-2.0 * th) / (1.0 - th)) * i * xh
            for s in range(NSEG):
                rows = slice(s * pitch, s * pitch + seg_len)
                for j in range(hb // LANES):
                    ls = slice(j * LANES, (j + 1) * LANES)
                    lt = h * (hb // LANES) + j
                    ab_scr[e, 0, lt, rows, :] = a[s * seg_len:(s + 1) * seg_len, ls]
                    ab_scr[e, 1, lt, rows, :] = bx[s * seg_len:(s + 1) * seg_len, ls]


def _seg_rows_load(scr, e, k, t, pitch):
    rows = pl.ds(t, NSEG, stride=pitch)
    return jnp.concatenate([scr[e, k, lt, rows, :] for lt in range(scr.shape[2])], axis=1)


def _seg_rows_store(scr, e, k, t, pitch, val):
    rows = pl.ds(t, NSEG, stride=pitch)
    for lt in range(scr.shape[2]):
        scr[e, k, lt, rows, :] = val[:, lt * LANES:(lt + 1) * LANES]


def _lru_pass1_kernel(x_ref, mv_ref, ng_ref, wrec_ref, brec_ref, cw_ref, cb_ref, wg_ref, bg_ref,
                      lam_ref, agg_ref, ab_scr, *, period, seg_len, pitch):
    _lru_coeffs(x_ref[0], mv_ref[0], ng_ref[...], wrec_ref[...], brec_ref[...], cw_ref[...],
                cb_ref[...], wg_ref, bg_ref, lam_ref[...], ab_scr,
                period=period, seg_len=seg_len, pitch=pitch)
    w = ab_scr.shape[2] * LANES

    def body(t, carry):
        pf, hf, pb, hb = carry
        af = _seg_rows_load(ab_scr, 0, 0, t, pitch)
        bf = _seg_rows_load(ab_scr, 0, 1, t, pitch)
        tb = seg_len - 1 - t
        ab = _seg_rows_load(ab_scr, 1, 0, tb, pitch)
        bb = _seg_rows_load(ab_scr, 1, 1, tb, pitch)
        return pf * af, af * hf + bf, pb * ab, ab * hb + bb

    one = jnp.ones((NSEG, w), F32)
    zero = jnp.zeros((NSEG, w), F32)
    pf, hf, pb, hb = lax.fori_loop(0, seg_len, body, (one, zero, one, zero))
    agg_ref[0, 0, 0] = pf
    agg_ref[0, 0, 1] = hf
    agg_ref[0, 1, 0] = pb
    agg_ref[0, 1, 1] = hb


def _lru_pass1_call(x, mv, ng, wrec, brec, cw, cb, wg, bg, lam, *, tile, period):
    b, s, d = x.shape
    w = wrec.shape[1]
    seg_len = tile // NSEG
    pitch = seg_len + ROW_PAD
    nt = s // tile
    kern = functools.partial(_lru_pass1_kernel, period=period, seg_len=seg_len, pitch=pitch)
    return pl.pallas_call(
        kern,
        grid=(b, nt),
        in_specs=[
            pl.BlockSpec((1, tile, d), lambda i, j: (i, j, 0)),
            pl.BlockSpec((1, SUBLANES, d), lambda i, j: (i, 0, 0)),
            _const_spec((1, d)),
            _const_spec((d, w)),
            _const_spec((1, w)),
            _const_spec(cw.shape),
            _const_spec((1, w)),
            _const_spec(wg.shape),
            _const_spec(bg.shape),
            _const_spec(lam.shape),
        ],
        out_specs=pl.BlockSpec((1, 2, 2, NSEG, w), lambda i, j: (i, 0, 0, j, 0)),
        out_shape=jax.ShapeDtypeStruct((b, 2, 2, nt * NSEG, w), F32),
        scratch_shapes=[pltpu.VMEM((2, 2, w // LANES, NSEG * pitch, LANES), F32)],
        compiler_params=_cparams(("parallel", "parallel")),
    )(x, mv, ng, wrec, brec, cw, cb, wg, bg, lam)


def _segscan_kernel(aggl_ref, aggc_ref, hin_ref):
    nsl = aggl_ref.shape[3]
    nsc = aggc_ref.shape[3]
    w = aggl_ref.shape[-1]
    for e in range(2):
        order_c = range(nsc) if e == 0 else range(nsc - 1, -1, -1)
        order_l = range(nsl) if e == 0 else range(nsl - 1, -1, -1)
        st = jnp.zeros((1, w), F32)
        for s in order_c:
            st = aggc_ref[0, e, 0, s:s + 1, :] * st + aggc_ref[0, e, 1, s:s + 1, :]
        for s in order_l:
            hin_ref[0, e, s:s + 1, :] = st
            st = aggl_ref[0, e, 0, s:s + 1, :] * st + aggl_ref[0, e, 1, s:s + 1, :]


def _segscan_call(agg_l, agg_c):
    b, _, _, nsl, w = agg_l.shape
    nsc = agg_c.shape[3]
    return pl.pallas_call(
        _segscan_kernel,
        grid=(b,),
        in_specs=[
            pl.BlockSpec((1, 2, 2, nsl, w), lambda i: (i, 0, 0, 0, 0)),
            pl.BlockSpec((1, 2, 2, nsc, w), lambda i: (i, 0, 0, 0, 0)),
        ],
        out_specs=pl.BlockSpec((1, 2, nsl, w), lambda i: (i, 0, 0, 0)),
        out_shape=jax.ShapeDtypeStruct((b, 2, nsl, w), F32),
        compiler_params=_cparams(("parallel",)),
    )(agg_l, agg_c)


def _lru_pass2_kernel(x_ref, mv_ref, ng_ref, wgate_ref, bgate_ref, wrec_ref, brec_ref, cw_ref,
                      cb_ref, wg_ref, bg_ref, lam_ref, wout_ref, bout_ref, hin_ref, o_ref,
                      ab_scr, gate_scr, yg_scr, *, period, seg_len, pitch):
    x = x_ref[0]
    mv = mv_ref[0]
    ng = ng_ref[...]
    u = _modulate(x, ng, mv[0:1], mv[1:2]).astype(BF16)
    gate_scr[...] = _gelu_tanh(_bdot(u, wgate_ref[...]) + bgate_ref[...])
    _lru_coeffs(x, mv, ng, wrec_ref[...], brec_ref[...], cw_ref[...], cb_ref[...], wg_ref, bg_ref,
                lam_ref[...], ab_scr, period=period, seg_len=seg_len, pitch=pitch)

    def body(t, carry):
        hf, hb = carry
        hf = _seg_rows_load(ab_scr, 0, 0, t, pitch) * hf + _seg_rows_load(ab_scr, 0, 1, t, pitch)
        _seg_rows_store(ab_scr, 0, 0, t, pitch, hf)
        tb = seg_len - 1 - t
        hb = _seg_rows_load(ab_scr, 1, 0, tb, pitch) * hb + _seg_rows_load(ab_scr, 1, 1, tb, pitch)
        _seg_rows_store(ab_scr, 1, 0, tb, pitch, hb)
        return hf, hb

    lax.fori_loop(0, seg_len, body, (hin_ref[0, 0], hin_ref[0, 1]))
    for s in range(NSEG):
        rows = slice(s * pitch, s * pitch + seg_len)
        trows = slice(s * seg_len, (s + 1) * seg_len)
        for lt in range(ab_scr.shape[2]):
            ls = slice(lt * LANES, (lt + 1) * LANES)
            hs = ab_scr[0, 0, lt, rows, :] + ab_scr[1, 0, lt, rows, :]
            yg_scr[trows, ls] = (hs * gate_scr[trows, ls]).astype(BF16)
    y = _bdot(yg_scr[...], wout_ref[...]) + bout_ref[...]
    o_ref[0] = x + mv[2:3] * y


def _lru_pass2_call(x, mv, ng, wgate, bgate, wrec, brec, cw, cb, wg, bg, lam, wout, bout, hin, *, tile):
    b, s, d = x.shape
    w = wrec.shape[1]
    seg_len = tile // NSEG
    pitch = seg_len + ROW_PAD
    nt = s // tile
    kern = functools.partial(_lru_pass2_kernel, period=GRID_W, seg_len=seg_len, pitch=pitch)
    return pl.pallas_call(
        kern,
        grid=(b, nt),
        in_specs=[
            pl.BlockSpec((1, tile, d), lambda i, j: (i, j, 0)),
            pl.BlockSpec((1, SUBLANES, d), lambda i, j: (i, 0, 0)),
            _const_spec((1, d)),
            _const_spec((d, w)),
            _const_spec((1, w)),
            _const_spec((d, w)),
            _const_spec((1, w)),
            _const_spec(cw.shape),
            _const_spec((1, w)),
            _const_spec(wg.shape),
            _const_spec(bg.shape),
            _const_spec(lam.shape),
            _const_spec((w, d)),
            _const_spec((1, d)),
            pl.BlockSpec((1, 2, NSEG, w), lambda i, j: (i, 0, j, 0)),
        ],
        out_specs=pl.BlockSpec((1, tile, d), lambda i, j: (i, j, 0)),
        out_shape=jax.ShapeDtypeStruct((b, s, d), F32),
        scratch_shapes=[
            pltpu.VMEM((2, 2, w // LANES, NSEG * pitch, LANES), F32),
            pltpu.VMEM((tile, w), F32),
            pltpu.VMEM((tile, w), BF16),
        ],
        compiler_params=_cparams(("parallel", "parallel")),
    )(x, mv, ng, wgate, bgate, wrec, brec, cw, cb, wg, bg, lam, wout, bout, hin)


FF_CHUNK = 1024


def _mlp_kernel(*refs, pre, final, groups):
    refs = list(refs)
    x_ref = refs.pop(0)
    mv_ref = refs.pop(0)
    ng_ref = refs.pop(0)
    w1_ref = refs.pop(0)
    w2_ref = refs.pop(0)
    if pre:
        v_ref = refs.pop(0)
        wout_ref = refs.pop(0)
        bout_ref = refs.pop(0)
    if final:
        fg_ref = refs.pop(0)
    o_ref = refs.pop(0)
    mv = mv_ref[0]
    x = x_ref[0]
    if pre:
        v = jnp.concatenate(
            [v_ref[0, g * ROW_PITCH:g * ROW_PITCH + DFT_N2, :] for g in range(groups)], axis=0)
        x = x + mv[2:3] * (_bdot(v.astype(BF16), wout_ref[...]) + bout_ref[...])
    u = _modulate(x, ng_ref[...], mv[3:4], mv[4:5]).astype(BF16)
    acc = jnp.zeros(x.shape, F32)
    for c in range(w1_ref.shape[1] // FF_CHUNK):
        cs = slice(c * FF_CHUNK, (c + 1) * FF_CHUNK)
        h = jnp.maximum(_bdot(u, w1_ref[:, cs]), 0.0)
        acc = acc + _bdot((h * h).astype(BF16), w2_ref[cs, :])
    out = x + mv[5:6] * acc
    if final:
        out = _rms_norm(out, fg_ref[...])
    o_ref[0] = out


def _mlp_call(x, mv, ng, w1, w2, *, tile, pre=None, final_g=None):
    b, s, d = x.shape
    f = w1.shape[1]
    groups = tile // DFT_N2
    args = [x, mv, ng, w1, w2]
    in_specs = [
        pl.BlockSpec((1, tile, d), lambda i, j: (i, j, 0)),
        pl.BlockSpec((1, SUBLANES, d), lambda i, j: (i, 0, 0)),
        _const_spec((1, d)),
        _const_spec((d, f)),
        _const_spec((f, d)),
    ]
    if pre is not None:
        v, wout, bout = pre
        args += [v, wout, bout]
        in_specs += [
            pl.BlockSpec((1, groups * ROW_PITCH, d), lambda i, j: (i, j, 0)),
            _const_spec((d, d)),
            _const_spec((1, d)),
        ]
    if final_g is not None:
        args.append(final_g)
        in_specs.append(_const_spec((1, d)))
    kern = functools.partial(_mlp_kernel, pre=pre is not None, final=final_g is not None,
                             groups=groups)
    return pl.pallas_call(
        kern,
        grid=(b, s // tile),
        in_specs=in_specs,
        out_specs=pl.BlockSpec((1, tile, d), lambda i, j: (i, j, 0)),
        out_shape=jax.ShapeDtypeStruct((b, s, d), F32),
        compiler_params=_cparams(("parallel", "parallel")),
    )(*args)


def _store_padded(o_ref, val, groups):
    pad = jnp.zeros((ROW_PAD, val.shape[1]), val.dtype)
    for g in range(groups):
        o_ref[0, g * ROW_PITCH:g * ROW_PITCH + DFT_N2, :] = val[g * DFT_N2:(g + 1) * DFT_N2]
        o_ref[0, g * ROW_PITCH + DFT_N2:(g + 1) * ROW_PITCH, :] = pad


def _hyproj_kernel(x_ref, mv_ref, ng_ref, win_ref, bin_ref, cw_ref, cb_ref, v_ref, xa_ref, xb_ref,
                   *, groups):
    mv = mv_ref[0]
    d = x_ref.shape[-1]
    u = _modulate(x_ref[0], ng_ref[...], mv[0:1], mv[1:2]).astype(BF16)
    for k, o_ref in enumerate((v_ref, xa_ref, xb_ref)):
        cs = slice(k * d, (k + 1) * d)
        z = _bdot(u, win_ref[:, cs]) + bin_ref[:, cs]
        z = _row_conv(z, cw_ref[:, cs], cb_ref[:, cs], HYENA_CONV_LEFT, GRID_W)
        _store_padded(o_ref, z, groups)


def _hyproj_call(x, mv, ng, win, bin_, cw, cb, *, tile):
    b, s, d = x.shape
    groups = tile // DFT_N2
    nt = s // tile
    out_sds = jax.ShapeDtypeStruct((b, nt * groups * ROW_PITCH, d), F32)
    out_spec = pl.BlockSpec((1, groups * ROW_PITCH, d), lambda i, j: (i, j, 0))
    return pl.pallas_call(
        functools.partial(_hyproj_kernel, groups=groups),
        grid=(b, nt),
        in_specs=[
            pl.BlockSpec((1, tile, d), lambda i, j: (i, j, 0)),
            pl.BlockSpec((1, SUBLANES, d), lambda i, j: (i, 0, 0)),
            _const_spec((1, d)),
            _const_spec(win.shape),
            _const_spec(bin_.shape),
            _const_spec(cw.shape),
            _const_spec(cb.shape),
        ],
        out_specs=[out_spec, out_spec, out_spec],
        out_shape=[out_sds, out_sds, out_sds],
        compiler_params=_cparams(("parallel", "parallel")),
    )(x, mv, ng, win, bin_, cw, cb)


def _filter_kernel(pos_ref, fw1_ref, fb1_ref, fw2_ref, fb2_ref, fw3_ref, fb3_ref, fw4_ref, freq_ref,
                   deltas_ref, h_ref, nrm_ref, *, groups):
    pos = pos_ref[...]
    freq = freq_ref[...]

    def hdot(a, b):
        return jnp.dot(a, b, preferred_element_type=F32, precision=HIGHEST)

    h = jnp.sin(freq * (hdot(pos, fw1_ref[...]) + fb1_ref[...]))
    h = jnp.sin(freq * (hdot(h, fw2_ref[...]) + fb2_ref[...]))
    h = jnp.sin(freq * (hdot(h, fw3_ref[...]) + fb3_ref[...]))
    d = deltas_ref.shape[1]
    decay = jnp.exp(-pos[:, 0:1] * deltas_ref[...])
    nparts = fw4_ref.shape[1] // d
    sums = []
    for p in range(nparts):
        cs = slice(p * d, (p + 1) * d)
        hp = hdot(h, fw4_ref[:, cs]) * decay
        sums.append(jnp.sum(jnp.abs(hp), axis=0, keepdims=True))
        pad = jnp.zeros((ROW_PAD, d), F32)
        for g in range(groups):
            h_ref[g * ROW_PITCH:g * ROW_PITCH + DFT_N2, cs] = hp[g * DFT_N2:(g + 1) * DFT_N2]
            h_ref[g * ROW_PITCH + DFT_N2:(g + 1) * ROW_PITCH, cs] = pad
    half = nparts // 2
    tot = jnp.concatenate([sums[p] + sums[p + half] for p in range(half)], axis=1)

    @pl.when(pl.program_id(0) == 0)
    def _():
        nrm_ref[...] = jnp.zeros_like(nrm_ref)

    nrm_ref[...] += tot


def _filter_call(pos, fw1, fb1, fw2, fb2, fw3, fb3, fw4, freq, deltas, *, tile):
    l, pe = pos.shape
    fh = fw2.shape[0]
    n4 = fw4.shape[1]
    d = deltas.shape[1]
    groups = tile // DFT_N2
    nt = l // tile
    return pl.pallas_call(
        functools.partial(_filter_kernel, groups=groups),
        grid=(nt,),
        in_specs=[
            pl.BlockSpec((tile, pe), lambda j: (j, 0)),
            _const_spec((pe, fh)), _const_spec((1, fh)),
            _const_spec((fh, fh)), _const_spec((1, fh)),
            _const_spec((fh, fh)), _const_spec((1, fh)),
            _const_spec((fh, n4)), _const_spec((1, fh)),
            _const_spec((1, d)),
        ],
        out_specs=[
            pl.BlockSpec((groups * ROW_PITCH, n4), lambda j: (j, 0)),
            pl.BlockSpec((1, n4 // 2), lambda j: (0, 0)),
        ],
        out_shape=[
            jax.ShapeDtypeStruct((nt * groups * ROW_PITCH, n4), F32),
            jax.ShapeDtypeStruct((1, n4 // 2), F32),
        ],
        compiler_params=_cparams(("arbitrary",)),
    )(pos, fw1, fb1, fw2, fb2, fw3, fb3, fw4, freq, deltas)


@functools.lru_cache(maxsize=None)
def _dft_constants(seq_len):
    n = 2 * seq_len
    n2 = DFT_N2
    n1 = n // n2
    nt1 = n1 // 2
    nf = n1 // 2 + 1
    nfp = _round_up(nf, SUBLANES)
    t1 = np.arange(nt1)[None, :]
    f1 = np.arange(nf)[:, None]
    ang1 = 2.0 * np.pi * (t1 * f1 % n1) / n1
    f1h = np.zeros((2 * nfp, nt1))
    f1h[:nf] = np.cos(ang1)
    f1h[nfp:nfp + nf] = -np.sin(ang1)
    t2 = np.arange(n2)[None, None, :]
    f2 = np.arange(n2)[None, :, None]
    ff1 = np.arange(nf)[:, None, None]
    ang2 = 2.0 * np.pi * ((t2 * (ff1 + n1 * f2)) % n) / n
    gr, gim = np.cos(ang2), -np.sin(ang2)
    g = np.concatenate([np.concatenate([gr, -gim], axis=2),
                        np.concatenate([gim, gr], axis=2)], axis=1)
    gi = np.transpose(g, (0, 2, 1))
    cf = np.full((nf,), 2.0)
    cf[0] = 1.0
    cf[-1] = 1.0
    f1i = np.zeros((nt1, 2 * nfp))
    f1i[:, :nf] = (np.cos(ang1) * cf[:, None]).T
    f1i[:, nfp:nfp + nf] = (-np.sin(ang1) * cf[:, None]).T
    as32 = lambda a: np.ascontiguousarray(a, dtype=np.float32)
    return as32(f1h), as32(g), as32(gi), as32(f1i), nt1, nf, nfp


def _dft_stage1(load_slab, f1h_ref, spec_scr, nfp):
    def body(t2, carry):
        a = _bdot(f1h_ref[...], load_slab(t2).astype(BF16))
        spec_scr[pl.ds(t2, nfp, stride=SPEC_PITCH), :] = a[:nfp]
        spec_scr[pl.ds(t2 + DFT_N2, nfp, stride=SPEC_PITCH), :] = a[nfp:]
        return carry

    lax.fori_loop(0, DFT_N2, body, 0)


def _filtfft_kernel(hf_ref, hb_ref, nrm_ref, f1h_ref, g_ref, k_ref, spec_scr, *, nt1, nf, nfp):
    inv = 1.0 / nrm_ref[...]
    for p, src in enumerate((hf_ref, hb_ref)):
        _dft_stage1(lambda t2: src[pl.ds(t2, nt1, stride=ROW_PITCH), :], f1h_ref, spec_scr, nfp)

        def body(f1, carry):
            base = pl.multiple_of(f1 * SPEC_PITCH, SUBLANES)
            a = spec_scr[pl.ds(base, 2 * DFT_N2), :].astype(BF16)
            xs = _bdot(g_ref[f1], a) * inv
            if p == 0:
                k_ref[0, f1] = xs
            else:
                prev = k_ref[0, f1]
                k_ref[0, f1] = jnp.concatenate(
                    [prev[:DFT_N2] + xs[:DFT_N2], prev[DFT_N2:] - xs[DFT_N2:]], axis=0)
            return carry

        lax.fori_loop(0, nf, body, 0)


def _filtfft_call(hraw, nrm, seq_len, *, ct):
    f1h, g, _, _, nt1, nf, nfp = _dft_constants(seq_len)
    rows, n4 = hraw.shape
    half = n4 // 2
    d = half // 2
    nct = d // ct
    kern = functools.partial(_filtfft_kernel, nt1=nt1, nf=nf, nfp=nfp)
    return pl.pallas_call(
        kern,
        grid=(2, nct),
        in_specs=[
            pl.BlockSpec((rows, ct), lambda o, c: (0, o * nct + c)),
            pl.BlockSpec((rows, ct), lambda o, c: (0, 2 * nct + o * nct + c)),
            pl.BlockSpec((1, ct), lambda o, c: (0, o * nct + c)),
            _const_spec(f1h.shape),
            _const_spec(g.shape),
        ],
        out_specs=pl.BlockSpec((1, nf, 2 * DFT_N2, ct), lambda o, c: (o, 0, 0, c)),
        out_shape=jax.ShapeDtypeStruct((2, nf, 2 * DFT_N2, d), F32),
        scratch_shapes=[pltpu.VMEM((nfp * SPEC_PITCH, ct), F32)],
        compiler_params=_cparams(("parallel", "parallel")),
    )(hraw, hraw, nrm, jnp.asarray(f1h).astype(BF16), jnp.asarray(g).astype(BF16))


def _longconv_kernel(v_ref, m_ref, k_ref, skip_ref, f1h_ref, g_ref, gi_ref, f1i_ref, o_ref, spec_scr,
                     *, nt1, nf, nfp):
    _dft_stage1(lambda t2: v_ref[0, pl.ds(t2, nt1, stride=ROW_PITCH), :], f1h_ref, spec_scr, nfp)

    def mid(f1, carry):
        base = pl.multiple_of(f1 * SPEC_PITCH, SUBLANES)
        a = spec_scr[pl.ds(base, 2 * DFT_N2), :].astype(BF16)
        xs = _bdot(g_ref[f1], a)
        kf = k_ref[0, f1]
        xr, xi = xs[:DFT_N2], xs[DFT_N2:]
        kr, ki = kf[:DFT_N2], kf[DFT_N2:]
        ys = jnp.concatenate([xr * kr - xi * ki, xr * ki + xi * kr], axis=0).astype(BF16)
        spec_scr[pl.ds(base, 2 * DFT_N2), :] = _bdot(gi_ref[f1], ys)
        return carry

    lax.fori_loop(0, nf, mid, 0)
    skip = skip_ref[...]

    def last(t2, carry):
        re = spec_scr[pl.ds(t2, nfp, stride=SPEC_PITCH), :]
        im = spec_scr[pl.ds(t2 + DFT_N2, nfp, stride=SPEC_PITCH), :]
        y = _bdot(f1i_ref[...], jnp.concatenate([re, im], axis=0).astype(BF16))
        rows = pl.ds(t2, nt1, stride=ROW_PITCH)
        vs = v_ref[0, rows, :]
        o_ref[0, rows, :] = m_ref[0, rows, :] * (y + vs * skip)
        return carry

    lax.fori_loop(0, DFT_N2, last, 0)
    pad = jnp.zeros((ROW_PAD, o_ref.shape[-1]), F32)
    for g in range(nt1):
        o_ref[0, g * ROW_PITCH + DFT_N2:(g + 1) * ROW_PITCH, :] = pad


def _longconv_call(v, m, kf, order, skip, seq_len, *, ct):
    f1h, g, gi, f1i, nt1, nf, nfp = _dft_constants(seq_len)
    b, rows, d = v.shape
    nct = d // ct
    kern = functools.partial(_longconv_kernel, nt1=nt1, nf=nf, nfp=nfp)
    seq_spec = pl.BlockSpec((1, rows, ct), lambda c, i: (i, 0, c))
    mul_spec = pl.BlockSpec((1, rows, ct), lambda c, i: (i, 0, c), pipeline_mode=pl.Buffered(1))
    scale = 1.0 / (2 * seq_len)
    return pl.pallas_call(
        kern,
        grid=(nct, b),
        in_specs=[
            seq_spec,
            mul_spec,
            pl.BlockSpec((1, nf, 2 * DFT_N2, ct), lambda c, i: (order, 0, 0, c),
                         pipeline_mode=pl.Buffered(1)),
            pl.BlockSpec((1, ct), lambda c, i: (0, c)),
            _const_spec(f1h.shape),
            _const_spec(g.shape),
            _const_spec(gi.shape),
            _const_spec(f1i.shape),
        ],
        out_specs=seq_spec,
        out_shape=jax.ShapeDtypeStruct((b, rows, d), F32),
        scratch_shapes=[pltpu.VMEM((nfp * SPEC_PITCH, ct), F32)],
        compiler_params=_cparams(("parallel", "parallel")),
    )(v, m, kf, skip, jnp.asarray(f1h).astype(BF16), jnp.asarray(g).astype(BF16),
      jnp.asarray(gi).astype(BF16), jnp.asarray(f1i * scale).astype(BF16))


def _mod_rows(mod_layer, nb, d):
    m = mod_layer.reshape(SUBLANES, 6, d)
    m = jnp.concatenate([m, jnp.zeros((SUBLANES, SUBLANES - 6, d), F32)], axis=1)
    return m[:nb], jnp.broadcast_to(m[nb:nb + 1], (nb, SUBLANES, d))


def _filter_positions(seq_len):
    t = jnp.linspace(0.0, 1.0, seq_len, dtype=F32)[:, None]
    w = (2.0 * math.pi / seq_len) * jnp.arange(seq_len, dtype=F32)[:, None]
    bands = jnp.linspace(1e-4, FILTER_BANDS - 1, FILTER_BANDS, dtype=F32)
    return jnp.concatenate([t, jnp.cos(bands * w), -jnp.sin(bands * w)], axis=-1)


def kernel(x, c, ctx, c_ctx, ada_w, ada_b, norm_g, mlp_w1, mlp_w2, lru_w_in, lru_b_in, lru_conv_w, lru_conv_b, lru_w_a, lru_b_a, lru_w_i, lru_b_i, lru_lambda, lru_w_out, lru_b_out, hy_w_in, hy_b_in, hy_conv_w, hy_conv_b, hy_fw1, hy_fb1, hy_fw2, hy_fb2, hy_fw3, hy_fb3, hy_fw4, hy_freq, hy_skip, hy_w_out, hy_b_out, final_g):
    nb, seq, d = x.shape
    ctx_len = ctx.shape[1]
    w = lru_w_out.shape[1]
    assert nb + 1 <= SUBLANES and seq % 512 == 0 and ctx_len % (NSEG * SUBLANES) == 0

    cvec = jnp.concatenate([c, c_ctx[None, :], jnp.zeros((SUBLANES - nb - 1, d), F32)], axis=0)
    mod = _ada_call(cvec, ada_w, ada_b)

    mv_l, mv_c = _mod_rows(mod[0], nb, d)
    ng = norm_g[0, 0][None, :]
    wgate = lru_w_in[0, :, :w].astype(BF16)
    wrec = lru_w_in[0, :, w:].astype(BF16)
    bgate = lru_b_in[0, :w][None, :]
    brec = lru_b_in[0, w:][None, :]
    cw = lru_conv_w[0]
    cb = lru_conv_b[0][None, :]
    wg = jnp.concatenate([lru_w_a[0, 0], lru_w_i[0, 0], lru_w_a[0, 1], lru_w_i[0, 1]], axis=-1).astype(BF16)
    bg = jnp.concatenate([lru_b_a[0, 0], lru_b_i[0, 0], lru_b_a[0, 1], lru_b_i[0, 1]], axis=-1)[:, None, :]
    lam = lru_lambda[0]
    agg_c = _lru_pass1_call(ctx, mv_c, ng, wrec, brec, cw, cb, wg, bg, lam, tile=ctx_len, period=ctx_len)
    agg_l = _lru_pass1_call(x, mv_l, ng, wrec, brec, cw, cb, wg, bg, lam, tile=512, period=GRID_W)
    hin = _segscan_call(agg_l, agg_c)
    x1 = _lru_pass2_call(x, mv_l, ng, wgate, bgate, wrec, brec, cw, cb, wg, bg, lam,
                         lru_w_out[0].astype(BF16), lru_b_out[0][None, :], hin, tile=512)
    x2 = _mlp_call(x1, mv_l, norm_g[0, 1][None, :], mlp_w1[0].astype(BF16), mlp_w2[0].astype(BF16), tile=512)

    mv1, _ = _mod_rows(mod[1], nb, d)
    v, xa, xb = _hyproj_call(x2, mv1, norm_g[1, 0][None, :], hy_w_in[0].astype(BF16), hy_b_in[0][None, :],
                             hy_conv_w[0], hy_conv_b[0][None, :], tile=512)
    pos = _filter_positions(seq)
    pe = _round_up(pos.shape[1], 128)
    pos = jnp.pad(pos, ((0, 0), (0, pe - pos.shape[1])))
    fw1 = jnp.pad(hy_fw1[0], ((0, pe - hy_fw1.shape[1]), (0, 0)))
    deltas = jnp.abs(jnp.linspace(math.log(FILTER_TARGET) / SLOW_DECAY_PCT,
                                  math.log(FILTER_TARGET) / FAST_DECAY_PCT, d, dtype=F32))[None, :]
    hraw, nrm = _filter_call(pos, fw1, hy_fb1[0][None, :], hy_fw2[0], hy_fb2[0][None, :], hy_fw3[0],
                             hy_fb3[0][None, :], hy_fw4[0], hy_freq[0][None, :], deltas, tile=512)
    kf = _filtfft_call(hraw, nrm, seq, ct=128)
    v1 = _longconv_call(v, xa, kf, 0, hy_skip[0, 0][None, :], seq, ct=128)
    v2 = _longconv_call(v1, xb, kf, 1, hy_skip[0, 1][None, :], seq, ct=128)
    return _mlp_call(x2, mv1, norm_g[1, 1][None, :], mlp_w1[1].astype(BF16), mlp_w2[1].astype(BF16), tile=512,
                     pre=(v2, hy_w_out[0].astype(BF16), hy_b_out[0][None, :]), final_g=final_g[None, :])
```

```python
import functools
import math

import numpy as np
import jax
import jax.numpy as jnp
from jax import lax
from jax.experimental import pallas as pl
from jax.experimental.pallas import tpu as pltpu

F32 = jnp.float32
BF16 = jnp.bfloat16
HIGHEST = lax.Precision.HIGHEST

NORM_EPS = 1e-6
GRID_W = 64
LRU_HEADS = 4
LRU_C = 8.0
LRU_CONV_LEFT = 2
HYENA_CONV_LEFT = 1
FILTER_BANDS = 16
FILTER_TARGET = 1e-2
FAST_DECAY_PCT = 0.3
SLOW_DECAY_PCT = 1.5

NSEG = 8
SUBLANES = 8
LANES = 128
SCAN_UNROLL = 4
VMEM_LIMIT = 58 * 1024 * 1024

DFT_N2 = 64
ROW_PAD = 8
ROW_PITCH = DFT_N2 + ROW_PAD
SPEC_PITCH = 2 * DFT_N2 + ROW_PAD
SLAB_UNROLL = 8
FREQ_UNROLL = 8


def _cparams(sem):
    return pltpu.CompilerParams(dimension_semantics=sem, vmem_limit_bytes=VMEM_LIMIT)


def _const_spec(shape):
    nd = len(shape)
    return pl.BlockSpec(shape, lambda *_: (0,) * nd, pipeline_mode=pl.Buffered(1))


def _round_up(a, m):
    return (a + m - 1) // m * m


def _rms_norm(x, g):
    ms = jnp.mean(x * x, axis=-1, keepdims=True)
    return (x * lax.rsqrt(ms + NORM_EPS)) * g


def _modulate(x, g, shift, scale):
    return _rms_norm(x, g) * (1.0 + scale) + shift


def _gelu_tanh(x):
    c = math.sqrt(2.0 / math.pi)
    return x * (0.5 * (1.0 + jnp.tanh(c * (x + 0.044715 * (x * x * x)))))


def _softplus(x):
    return jnp.maximum(x, 0.0) + jnp.log1p(jnp.exp(-jnp.abs(x)))


def _bdot(a, b):
    return jnp.dot(a, b, preferred_element_type=F32)


def _row_conv(z, w, b, left, period):
    T = z.shape[0]
    r = lax.broadcasted_iota(jnp.int32, z.shape, 0) & (period - 1)
    acc = b + w[left:left + 1] * z
    for k in range(w.shape[0]):
        o = k - left
        if o == 0:
            continue
        shifted = pltpu.roll(z, (-o) % T, 0)
        valid = (r >= -o) if o < 0 else (r < period - o)
        acc = acc + w[k:k + 1] * jnp.where(valid, shifted, 0.0)
    return acc


def _ada_kernel(c_ref, w_ref, b_ref, o_ref):
    c = c_ref[...]
    cond = c * jax.nn.sigmoid(c)
    o_ref[0] = jnp.dot(cond, w_ref[0], preferred_element_type=F32, precision=HIGHEST) + b_ref[0]


def _ada_call(cvec, ada_w, ada_b):
    depth, d, n = ada_w.shape
    tn = 1536
    return pl.pallas_call(
        _ada_kernel,
        grid=(depth, n // tn),
        in_specs=[
            pl.BlockSpec((SUBLANES, d), lambda i, j: (0, 0)),
            pl.BlockSpec((1, d, tn), lambda i, j: (i, 0, j)),
            pl.BlockSpec((1, 1, tn), lambda i, j: (i, 0, j)),
        ],
        out_specs=pl.BlockSpec((1, SUBLANES, tn), lambda i, j: (i, 0, j)),
        out_shape=jax.ShapeDtypeStruct((depth, SUBLANES, n), F32),
        compiler_params=_cparams(("parallel", "parallel")),
    )(cvec, ada_w, ada_b.reshape(depth, 1, n))


def _lru_coeffs(x, mv, ng, w_rec, b_rec, cw, cb, wg, bg, lam, ab_scr, *, period, seg_len, pitch):
    w = w_rec.shape[1]
    hb = w // LRU_HEADS
    u = _modulate(x, ng, mv[0:1], mv[1:2]).astype(BF16)
    zr = _bdot(u, w_rec) + b_rec
    xl = _row_conv(zr, cw, cb, LRU_CONV_LEFT, period)
    half_c_sp = (-0.5 * LRU_C) * _softplus(-lam)
    for h in range(LRU_HEADS):
        cs = slice(h * hb, (h + 1) * hb)
        xh = xl[:, cs]
        gates = _bdot(xh.astype(BF16), wg[h]) + bg[h]
        xh_half = 0.5 * xh
        for e in range(2):
            tr = jnp.tanh(gates[:, (2 * e) * hb:(2 * e + 1) * hb])
            ti = jnp.tanh(gates[:, (2 * e + 1) * hb:(2 * e + 2) * hb])
            c = half_c_sp[e:e + 1, cs]
            log_a = c * tr + c
            a = jnp.exp(log_a)
            th = jnp.tanh(log_a)
            q = (-2.0 * th) / (1.0 - th)
            wgt = jnp.where(q > 0.0, q * lax.rsqrt(q), 0.0) * xh_half
            bx = wgt * ti + wgt
            for s in range(NSEG):
                rows = slice(s * pitch, s * pitch + seg_len)
                for j in range(hb // LANES):
                    ls = slice(j * LANES, (j + 1) * LANES)
                    lt = h * (hb // LANES) + j
                    ab_scr[e, 0, lt, rows, :] = a[s * seg_len:(s + 1) * seg_len, ls]
                    ab_scr[e, 1, lt, rows, :] = bx[s * seg_len:(s + 1) * seg_len, ls]


def _seg_rows_load(scr, e, k, t, pitch):
    rows = pl.ds(t, NSEG, stride=pitch)
    return jnp.concatenate([scr[e, k, lt, rows, :] for lt in range(scr.shape[2])], axis=1)


def _seg_rows_store(scr, e, k, t, pitch, val):
    rows = pl.ds(t, NSEG, stride=pitch)
    for lt in range(scr.shape[2]):
        scr[e, k, lt, rows, :] = val[:, lt * LANES:(lt + 1) * LANES]


def _lru_pass1_kernel(x_ref, mv_ref, ng_ref, wrec_ref, brec_ref, cw_ref, cb_ref, wg_ref, bg_ref,
                      lam_ref, agg_ref, ab_scr, *, period, seg_len, pitch):
    _lru_coeffs(x_ref[0], mv_ref[0], ng_ref[...], wrec_ref[...], brec_ref[...], cw_ref[...],
                cb_ref[...], wg_ref, bg_ref, lam_ref[...], ab_scr,
                period=period, seg_len=seg_len, pitch=pitch)
    w = ab_scr.shape[2] * LANES

    def body(t, carry):
        pf, hf, pb, hb = carry
        af = _seg_rows_load(ab_scr, 0, 0, t, pitch)
        bf = _seg_rows_load(ab_scr, 0, 1, t, pitch)
        tb = seg_len - 1 - t
        ab = _seg_rows_load(ab_scr, 1, 0, tb, pitch)
        bb = _seg_rows_load(ab_scr, 1, 1, tb, pitch)
        return pf * af, af * hf + bf, pb * ab, ab * hb + bb

    one = jnp.ones((NSEG, w), F32)
    zero = jnp.zeros((NSEG, w), F32)
    pf, hf, pb, hb = lax.fori_loop(0, seg_len, body, (one, zero, one, zero), unroll=SCAN_UNROLL)
    agg_ref[0, 0, 0] = pf
    agg_ref[0, 0, 1] = hf
    agg_ref[0, 1, 0] = pb
    agg_ref[0, 1, 1] = hb


def _lru_pass1_call(x, mv, ng, wrec, brec, cw, cb, wg, bg, lam, *, tile, period):
    b, s, d = x.shape
    w = wrec.shape[1]
    seg_len = tile // NSEG
    pitch = seg_len + ROW_PAD
    nt = s // tile
    kern = functools.partial(_lru_pass1_kernel, period=period, seg_len=seg_len, pitch=pitch)
    return pl.pallas_call(
        kern,
        grid=(b, nt),
        in_specs=[
            pl.BlockSpec((1, tile, d), lambda i, j: (i, j, 0)),
            pl.BlockSpec((1, SUBLANES, d), lambda i, j: (i, 0, 0)),
            _const_spec((1, d)),
            _const_spec((d, w)),
            _const_spec((1, w)),
            _const_spec(cw.shape),
            _const_spec((1, w)),
            _const_spec(wg.shape),
            _const_spec(bg.shape),
            _const_spec(lam.shape),
        ],
        out_specs=pl.BlockSpec((1, 2, 2, NSEG, w), lambda i, j: (i, 0, 0, j, 0)),
        out_shape=jax.ShapeDtypeStruct((b, 2, 2, nt * NSEG, w), F32),
        scratch_shapes=[pltpu.VMEM((2, 2, w // LANES, NSEG * pitch, LANES), F32)],
        compiler_params=_cparams(("parallel", "parallel")),
    )(x, mv, ng, wrec, brec, cw, cb, wg, bg, lam)


def _segscan_kernel(aggl_ref, aggc_ref, hin_ref):
    nsl = aggl_ref.shape[3]
    nsc = aggc_ref.shape[3]
    w = aggl_ref.shape[-1]
    for e in range(2):
        order_c = range(nsc) if e == 0 else range(nsc - 1, -1, -1)
        order_l = range(nsl) if e == 0 else range(nsl - 1, -1, -1)
        st = jnp.zeros((1, w), F32)
        for s in order_c:
            st = aggc_ref[0, e, 0, s:s + 1, :] * st + aggc_ref[0, e, 1, s:s + 1, :]
        for s in order_l:
            hin_ref[0, e, s:s + 1, :] = st
            st = aggl_ref[0, e, 0, s:s + 1, :] * st + aggl_ref[0, e, 1, s:s + 1, :]


def _segscan_call(agg_l, agg_c):
    b, _, _, nsl, w = agg_l.shape
    nsc = agg_c.shape[3]
    return pl.pallas_call(
        _segscan_kernel,
        grid=(b,),
        in_specs=[
            pl.BlockSpec((1, 2, 2, nsl, w), lambda i: (i, 0, 0, 0, 0)),
            pl.BlockSpec((1, 2, 2, nsc, w), lambda i: (i, 0, 0, 0, 0)),
        ],
        out_specs=pl.BlockSpec((1, 2, nsl, w), lambda i: (i, 0, 0, 0)),
        out_shape=jax.ShapeDtypeStruct((b, 2, nsl, w), F32),
        compiler_params=_cparams(("parallel",)),
    )(agg_l, agg_c)


def _lru_pass2_kernel(x_ref, mv_ref, ng_ref, wgate_ref, bgate_ref, wrec_ref, brec_ref, cw_ref,
                      cb_ref, wg_ref, bg_ref, lam_ref, wout_ref, bout_ref, hin_ref, o_ref,
                      ab_scr, gate_scr, yg_scr, *, period, seg_len, pitch):
    x = x_ref[0]
    mv = mv_ref[0]
    ng = ng_ref[...]
    u = _modulate(x, ng, mv[0:1], mv[1:2]).astype(BF16)
    gate_scr[...] = _gelu_tanh(_bdot(u, wgate_ref[...]) + bgate_ref[...])
    _lru_coeffs(x, mv, ng, wrec_ref[...], brec_ref[...], cw_ref[...], cb_ref[...], wg_ref, bg_ref,
                lam_ref[...], ab_scr, period=period, seg_len=seg_len, pitch=pitch)

    def body(t, carry):
        hf, hb = carry
        hf = _seg_rows_load(ab_scr, 0, 0, t, pitch) * hf + _seg_rows_load(ab_scr, 0, 1, t, pitch)
        _seg_rows_store(ab_scr, 0, 0, t, pitch, hf)
        tb = seg_len - 1 - t
        hb = _seg_rows_load(ab_scr, 1, 0, tb, pitch) * hb + _seg_rows_load(ab_scr, 1, 1, tb, pitch)
        _seg_rows_store(ab_scr, 1, 0, tb, pitch, hb)
        return hf, hb

    lax.fori_loop(0, seg_len, body, (hin_ref[0, 0], hin_ref[0, 1]), unroll=SCAN_UNROLL)
    for s in range(NSEG):
        rows = slice(s * pitch, s * pitch + seg_len)
        trows = slice(s * seg_len, (s + 1) * seg_len)
        for lt in range(ab_scr.shape[2]):
            ls = slice(lt * LANES, (lt + 1) * LANES)
            hs = ab_scr[0, 0, lt, rows, :] + ab_scr[1, 0, lt, rows, :]
            yg_scr[trows, ls] = (hs * gate_scr[trows, ls]).astype(BF16)
    y = _bdot(yg_scr[...], wout_ref[...]) + bout_ref[...]
    o_ref[0] = x + mv[2:3] * y


def _lru_pass2_call(x, mv, ng, wgate, bgate, wrec, brec, cw, cb, wg, bg, lam, wout, bout, hin, *, tile):
    b, s, d = x.shape
    w = wrec.shape[1]
    seg_len = tile // NSEG
    pitch = seg_len + ROW_PAD
    nt = s // tile
    kern = functools.partial(_lru_pass2_kernel, period=GRID_W, seg_len=seg_len, pitch=pitch)
    return pl.pallas_call(
        kern,
        grid=(b, nt),
        in_specs=[
            pl.BlockSpec((1, tile, d), lambda i, j: (i, j, 0)),
            pl.BlockSpec((1, SUBLANES, d), lambda i, j: (i, 0, 0)),
            _const_spec((1, d)),
            _const_spec((d, w)),
            _const_spec((1, w)),
            _const_spec((d, w)),
            _const_spec((1, w)),
            _const_spec(cw.shape),
            _const_spec((1, w)),
            _const_spec(wg.shape),
            _const_spec(bg.shape),
            _const_spec(lam.shape),
            _const_spec((w, d)),
            _const_spec((1, d)),
            pl.BlockSpec((1, 2, NSEG, w), lambda i, j: (i, 0, j, 0)),
        ],
        out_specs=pl.BlockSpec((1, tile, d), lambda i, j: (i, j, 0)),
        out_shape=jax.ShapeDtypeStruct((b, s, d), F32),
        scratch_shapes=[
            pltpu.VMEM((2, 2, w // LANES, NSEG * pitch, LANES), F32),
            pltpu.VMEM((tile, w), F32),
            pltpu.VMEM((tile, w), BF16),
        ],
        compiler_params=_cparams(("parallel", "parallel")),
    )(x, mv, ng, wgate, bgate, wrec, brec, cw, cb, wg, bg, lam, wout, bout, hin)


FF_CHUNK = 1024


def _mlp_kernel(*refs, pre, final, groups):
    refs = list(refs)
    x_ref = refs.pop(0)
    mv_ref = refs.pop(0)
    ng_ref = refs.pop(0)
    w1_ref = refs.pop(0)
    w2_ref = refs.pop(0)
    if pre:
        v_ref = refs.pop(0)
        wout_ref = refs.pop(0)
        bout_ref = refs.pop(0)
    if final:
        fg_ref = refs.pop(0)
    o_ref = refs.pop(0)
    mv = mv_ref[0]
    x = x_ref[0]
    if pre:
        v = jnp.concatenate(
            [v_ref[0, g * ROW_PITCH:g * ROW_PITCH + DFT_N2, :] for g in range(groups)], axis=0)
        x = x + mv[2:3] * (_bdot(v.astype(BF16), wout_ref[...]) + bout_ref[...])
    u = _modulate(x, ng_ref[...], mv[3:4], mv[4:5]).astype(BF16)
    acc = jnp.zeros(x.shape, F32)
    for c in range(w1_ref.shape[1] // FF_CHUNK):
        cs = slice(c * FF_CHUNK, (c + 1) * FF_CHUNK)
        h = jnp.maximum(_bdot(u, w1_ref[:, cs]), 0.0)
        acc = acc + _bdot((h * h).astype(BF16), w2_ref[cs, :])
    out = x + mv[5:6] * acc
    if final:
        out = _rms_norm(out, fg_ref[...])
    o_ref[0] = out


def _mlp_call(x, mv, ng, w1, w2, *, tile, pre=None, final_g=None):
    b, s, d = x.shape
    f = w1.shape[1]
    groups = tile // DFT_N2
    args = [x, mv, ng, w1, w2]
    in_specs = [
        pl.BlockSpec((1, tile, d), lambda i, j: (i, j, 0)),
        pl.BlockSpec((1, SUBLANES, d), lambda i, j: (i, 0, 0)),
        _const_spec((1, d)),
        _const_spec((d, f)),
        _const_spec((f, d)),
    ]
    if pre is not None:
        v, wout, bout = pre
        args += [v, wout, bout]
        in_specs += [
            pl.BlockSpec((1, groups * ROW_PITCH, d), lambda i, j: (i, j, 0)),
            _const_spec((d, d)),
            _const_spec((1, d)),
        ]
    if final_g is not None:
        args.append(final_g)
        in_specs.append(_const_spec((1, d)))
    kern = functools.partial(_mlp_kernel, pre=pre is not None, final=final_g is not None,
                             groups=groups)
    return pl.pallas_call(
        kern,
        grid=(b, s // tile),
        in_specs=in_specs,
        out_specs=pl.BlockSpec((1, tile, d), lambda i, j: (i, j, 0)),
        out_shape=jax.ShapeDtypeStruct((b, s, d), F32),
        compiler_params=_cparams(("parallel", "parallel")),
    )(*args)


def _store_padded(o_ref, val, groups):
    pad = jnp.zeros((ROW_PAD, val.shape[1]), val.dtype)
    for g in range(groups):
        o_ref[0, g * ROW_PITCH:g * ROW_PITCH + DFT_N2, :] = val[g * DFT_N2:(g + 1) * DFT_N2]
        o_ref[0, g * ROW_PITCH + DFT_N2:(g + 1) * ROW_PITCH, :] = pad


def _hyproj_kernel(x_ref, mv_ref, ng_ref, win_ref, bin_ref, cw_ref, cb_ref, v_ref, xa_ref, xb_ref,
                   *, groups):
    mv = mv_ref[0]
    d = x_ref.shape[-1]
    u = _modulate(x_ref[0], ng_ref[...], mv[0:1], mv[1:2]).astype(BF16)
    for k, o_ref in enumerate((v_ref, xa_ref, xb_ref)):
        cs = slice(k * d, (k + 1) * d)
        z = _bdot(u, win_ref[:, cs]) + bin_ref[:, cs]
        z = _row_conv(z, cw_ref[:, cs], cb_ref[:, cs], HYENA_CONV_LEFT, GRID_W)
        _store_padded(o_ref, z, groups)


def _hyproj_call(x, mv, ng, win, bin_, cw, cb, *, tile):
    b, s, d = x.shape
    groups = tile // DFT_N2
    nt = s // tile
    out_sds = jax.ShapeDtypeStruct((b, nt * groups * ROW_PITCH, d), F32)
    out_spec = pl.BlockSpec((1, groups * ROW_PITCH, d), lambda i, j: (i, j, 0))
    return pl.pallas_call(
        functools.partial(_hyproj_kernel, groups=groups),
        grid=(b, nt),
        in_specs=[
            pl.BlockSpec((1, tile, d), lambda i, j: (i, j, 0)),
            pl.BlockSpec((1, SUBLANES, d), lambda i, j: (i, 0, 0)),
            _const_spec((1, d)),
            _const_spec(win.shape),
            _const_spec(bin_.shape),
            _const_spec(cw.shape),
            _const_spec(cb.shape),
        ],
        out_specs=[out_spec, out_spec, out_spec],
        out_shape=[out_sds, out_sds, out_sds],
        compiler_params=_cparams(("parallel", "parallel")),
    )(x, mv, ng, win, bin_, cw, cb)


def _filter_kernel(pos_ref, fw1_ref, fb1_ref, fw2_ref, fb2_ref, fw3_ref, fb3_ref, fw4_ref, freq_ref,
                   deltas_ref, h_ref, nrm_ref, *, groups):
    pos = pos_ref[...]
    freq = freq_ref[...]

    def hdot(a, b):
        return jnp.dot(a, b, preferred_element_type=F32, precision=HIGHEST)

    h = jnp.sin(freq * (hdot(pos, fw1_ref[...]) + fb1_ref[...]))
    h = jnp.sin(freq * (hdot(h, fw2_ref[...]) + fb2_ref[...]))
    h = jnp.sin(freq * (hdot(h, fw3_ref[...]) + fb3_ref[...]))
    d = deltas_ref.shape[1]
    decay = jnp.exp(-pos[:, 0:1] * deltas_ref[...])
    nparts = fw4_ref.shape[1] // d
    sums = []
    for p in range(nparts):
        cs = slice(p * d, (p + 1) * d)
        hp = hdot(h, fw4_ref[:, cs]) * decay
        sums.append(jnp.sum(jnp.abs(hp), axis=0, keepdims=True))
        pad = jnp.zeros((ROW_PAD, d), F32)
        for g in range(groups):
            h_ref[g * ROW_PITCH:g * ROW_PITCH + DFT_N2, cs] = hp[g * DFT_N2:(g + 1) * DFT_N2]
            h_ref[g * ROW_PITCH + DFT_N2:(g + 1) * ROW_PITCH, cs] = pad
    half = nparts // 2
    tot = jnp.concatenate([sums[p] + sums[p + half] for p in range(half)], axis=1)

    @pl.when(pl.program_id(0) == 0)
    def _():
        nrm_ref[...] = jnp.zeros_like(nrm_ref)

    nrm_ref[...] += tot


def _filter_call(pos, fw1, fb1, fw2, fb2, fw3, fb3, fw4, freq, deltas, *, tile):
    l, pe = pos.shape
    fh = fw2.shape[0]
    n4 = fw4.shape[1]
    d = deltas.shape[1]
    groups = tile // DFT_N2
    nt = l // tile
    return pl.pallas_call(
        functools.partial(_filter_kernel, groups=groups),
        grid=(nt,),
        in_specs=[
            pl.BlockSpec((tile, pe), lambda j: (j, 0)),
            _const_spec((pe, fh)), _const_spec((1, fh)),
            _const_spec((fh, fh)), _const_spec((1, fh)),
            _const_spec((fh, fh)), _const_spec((1, fh)),
            _const_spec((fh, n4)), _const_spec((1, fh)),
            _const_spec((1, d)),
        ],
        out_specs=[
            pl.BlockSpec((groups * ROW_PITCH, n4), lambda j: (j, 0)),
            pl.BlockSpec((1, n4 // 2), lambda j: (0, 0)),
        ],
        out_shape=[
            jax.ShapeDtypeStruct((nt * groups * ROW_PITCH, n4), F32),
            jax.ShapeDtypeStruct((1, n4 // 2), F32),
        ],
        compiler_params=_cparams(("arbitrary",)),
    )(pos, fw1, fb1, fw2, fb2, fw3, fb3, fw4, freq, deltas)


@functools.lru_cache(maxsize=None)
def _dft_constants(seq_len):
    n = 2 * seq_len
    n2 = DFT_N2
    n1 = n // n2
    nt1 = n1 // 2
    nf = n1 // 2 + 1
    nfp = _round_up(nf, SUBLANES)
    t1 = np.arange(nt1)[None, :]
    f1 = np.arange(nf)[:, None]
    ang1 = 2.0 * np.pi * (t1 * f1 % n1) / n1
    f1h = np.zeros((2 * nfp, nt1))
    f1h[:nf] = np.cos(ang1)
    f1h[nfp:nfp + nf] = -np.sin(ang1)
    t2 = np.arange(n2)[None, None, :]
    f2 = np.arange(n2)[None, :, None]
    ff1 = np.arange(nf)[:, None, None]
    ang2 = 2.0 * np.pi * ((t2 * (ff1 + n1 * f2)) % n) / n
    gr, gim = np.cos(ang2), -np.sin(ang2)
    g = np.concatenate([np.concatenate([gr, -gim], axis=2),
                        np.concatenate([gim, gr], axis=2)], axis=1)
    gi = np.transpose(g, (0, 2, 1))
    cf = np.full((nf,), 2.0)
    cf[0] = 1.0
    cf[-1] = 1.0
    f1i = np.zeros((nt1, 2 * nfp))
    f1i[:, :nf] = (np.cos(ang1) * cf[:, None]).T
    f1i[:, nfp:nfp + nf] = (-np.sin(ang1) * cf[:, None]).T
    as32 = lambda a: np.ascontiguousarray(a, dtype=np.float32)
    return as32(f1h), as32(g), as32(gi), as32(f1i), nt1, nf, nfp


def _dft_stage1(load_slab, f1h_ref, spec_scr, nfp):
    def body(t2, carry):
        a = _bdot(f1h_ref[...], load_slab(t2).astype(BF16))
        spec_scr[pl.ds(t2, nfp, stride=SPEC_PITCH), :] = a[:nfp]
        spec_scr[pl.ds(t2 + DFT_N2, nfp, stride=SPEC_PITCH), :] = a[nfp:]
        return carry

    lax.fori_loop(0, DFT_N2, body, 0, unroll=SLAB_UNROLL)


def _filtfft_kernel(hf_ref, hb_ref, nrm_ref, f1h_ref, g_ref, k_ref, spec_scr, *, nt1, nf, nfp):
    inv = 1.0 / nrm_ref[...]
    for p, src in enumerate((hf_ref, hb_ref)):
        _dft_stage1(lambda t2: src[pl.ds(t2, nt1, stride=ROW_PITCH), :], f1h_ref, spec_scr, nfp)

        def body(f1, carry):
            base = pl.multiple_of(f1 * SPEC_PITCH, SUBLANES)
            a = spec_scr[pl.ds(base, 2 * DFT_N2), :].astype(BF16)
            xs = _bdot(g_ref[f1], a) * inv
            if p == 0:
                k_ref[0, f1] = xs
            else:
                prev = k_ref[0, f1]
                k_ref[0, f1] = jnp.concatenate(
                    [prev[:DFT_N2] + xs[:DFT_N2], prev[DFT_N2:] - xs[DFT_N2:]], axis=0)
            return carry

        lax.fori_loop(0, nf, body, 0, unroll=FREQ_UNROLL)


def _filtfft_call(hraw, nrm, seq_len, *, ct):
    f1h, g, _, _, nt1, nf, nfp = _dft_constants(seq_len)
    rows, n4 = hraw.shape
    half = n4 // 2
    d = half // 2
    nct = d // ct
    kern = functools.partial(_filtfft_kernel, nt1=nt1, nf=nf, nfp=nfp)
    return pl.pallas_call(
        kern,
        grid=(2, nct),
        in_specs=[
            pl.BlockSpec((rows, ct), lambda o, c: (0, o * nct + c)),
            pl.BlockSpec((rows, ct), lambda o, c: (0, 2 * nct + o * nct + c)),
            pl.BlockSpec((1, ct), lambda o, c: (0, o * nct + c)),
            _const_spec(f1h.shape),
            _const_spec(g.shape),
        ],
        out_specs=pl.BlockSpec((1, nf, 2 * DFT_N2, ct), lambda o, c: (o, 0, 0, c)),
        out_shape=jax.ShapeDtypeStruct((2, nf, 2 * DFT_N2, d), F32),
        scratch_shapes=[pltpu.VMEM((nfp * SPEC_PITCH, ct), F32)],
        compiler_params=_cparams(("parallel", "parallel")),
    )(hraw, hraw, nrm, jnp.asarray(f1h).astype(BF16), jnp.asarray(g).astype(BF16))


def _longconv_kernel(v_ref, m_ref, k_ref, skip_ref, f1h_ref, g_ref, gi_ref, f1i_ref, o_ref, spec_scr,
                     *, nt1, nf, nfp):
    _dft_stage1(lambda t2: v_ref[0, pl.ds(t2, nt1, stride=ROW_PITCH), :], f1h_ref, spec_scr, nfp)

    def mid(f1, carry):
        base = pl.multiple_of(f1 * SPEC_PITCH, SUBLANES)
        a = spec_scr[pl.ds(base, 2 * DFT_N2), :].astype(BF16)
        xs = _bdot(g_ref[f1], a)
        kf = k_ref[0, f1]
        xr, xi = xs[:DFT_N2], xs[DFT_N2:]
        kr, ki = kf[:DFT_N2], kf[DFT_N2:]
        ys = jnp.concatenate([xr * kr - xi * ki, xr * ki + xi * kr], axis=0).astype(BF16)
        spec_scr[pl.ds(base, 2 * DFT_N2), :] = _bdot(gi_ref[f1], ys)
        return carry

    lax.fori_loop(0, nf, mid, 0, unroll=FREQ_UNROLL)
    skip = skip_ref[...]

    def last(t2, carry):
        re = spec_scr[pl.ds(t2, nfp, stride=SPEC_PITCH), :]
        im = spec_scr[pl.ds(t2 + DFT_N2, nfp, stride=SPEC_PITCH), :]
        y = _bdot(f1i_ref[...], jnp.concatenate([re, im], axis=0).astype(BF16))
        rows = pl.ds(t2, nt1, stride=ROW_PITCH)
        vs = v_ref[0, rows, :]
        o_ref[0, rows, :] = m_ref[0, rows, :] * (y + vs * skip)
        return carry

    lax.fori_loop(0, DFT_N2, last, 0, unroll=SLAB_UNROLL)
    pad = jnp.zeros((ROW_PAD, o_ref.shape[-1]), F32)
    for g in range(nt1):
        o_ref[0, g * ROW_PITCH + DFT_N2:(g + 1) * ROW_PITCH, :] = pad


def _longconv_call(v, m, kf, order, skip, seq_len, *, ct):
    f1h, g, gi, f1i, nt1, nf, nfp = _dft_constants(seq_len)
    b, rows, d = v.shape
    nct = d // ct
    kern = functools.partial(_longconv_kernel, nt1=nt1, nf=nf, nfp=nfp)
    seq_spec = pl.BlockSpec((1, rows, ct), lambda c, i: (i, 0, c))
    mul_spec = pl.BlockSpec((1, rows, ct), lambda c, i: (i, 0, c), pipeline_mode=pl.Buffered(1))
    scale = 1.0 / (2 * seq_len)
    return pl.pallas_call(
        kern,
        grid=(nct, b),
        in_specs=[
            seq_spec,
            mul_spec,
            pl.BlockSpec((1, nf, 2 * DFT_N2, ct), lambda c, i: (order, 0, 0, c),
                         pipeline_mode=pl.Buffered(1)),
            pl.BlockSpec((1, ct), lambda c, i: (0, c)),
            _const_spec(f1h.shape),
            _const_spec(g.shape),
            _const_spec(gi.shape),
            _const_spec(f1i.shape),
        ],
        out_specs=seq_spec,
        out_shape=jax.ShapeDtypeStruct((b, rows, d), F32),
        scratch_shapes=[pltpu.VMEM((nfp * SPEC_PITCH, ct), F32)],
        compiler_params=_cparams(("parallel", "parallel")),
    )(v, m, kf, skip, jnp.asarray(f1h).astype(BF16), jnp.asarray(g).astype(BF16),
      jnp.asarray(gi).astype(BF16), jnp.asarray(f1i * scale).astype(BF16))


def _mod_rows(mod_layer, nb, d):
    m = mod_layer.reshape(SUBLANES, 6, d)
    m = jnp.concatenate([m, jnp.zeros((SUBLANES, SUBLANES - 6, d), F32)], axis=1)
    return m[:nb], jnp.broadcast_to(m[nb:nb + 1], (nb, SUBLANES, d))


def _filter_positions(seq_len):
    t = jnp.linspace(0.0, 1.0, seq_len, dtype=F32)[:, None]
    w = (2.0 * math.pi / seq_len) * jnp.arange(seq_len, dtype=F32)[:, None]
    bands = jnp.linspace(1e-4, FILTER_BANDS - 1, FILTER_BANDS, dtype=F32)
    return jnp.concatenate([t, jnp.cos(bands * w), -jnp.sin(bands * w)], axis=-1)


def kernel(x, c, ctx, c_ctx, ada_w, ada_b, norm_g, mlp_w1, mlp_w2, lru_w_in, lru_b_in, lru_conv_w, lru_conv_b, lru_w_a, lru_b_a, lru_w_i, lru_b_i, lru_lambda, lru_w_out, lru_b_out, hy_w_in, hy_b_in, hy_conv_w, hy_conv_b, hy_fw1, hy_fb1, hy_fw2, hy_fb2, hy_fw3, hy_fb3, hy_fw4, hy_freq, hy_skip, hy_w_out, hy_b_out, final_g):
    nb, seq, d = x.shape
    ctx_len = ctx.shape[1]
    w = lru_w_out.shape[1]
    assert nb + 1 <= SUBLANES and seq % 512 == 0 and ctx_len % (NSEG * SUBLANES) == 0

    cvec = jnp.concatenate([c, c_ctx[None, :], jnp.zeros((SUBLANES - nb - 1, d), F32)], axis=0)
    mod = _ada_call(cvec, ada_w, ada_b)

    mv_l, mv_c = _mod_rows(mod[0], nb, d)
    ng = norm_g[0, 0][None, :]
    wgate = lru_w_in[0, :, :w].astype(BF16)
    wrec = lru_w_in[0, :, w:].astype(BF16)
    bgate = lru_b_in[0, :w][None, :]
    brec = lru_b_in[0, w:][None, :]
    cw = lru_conv_w[0]
    cb = lru_conv_b[0][None, :]
    wg = (0.5 * jnp.concatenate([lru_w_a[0, 0], lru_w_i[0, 0], lru_w_a[0, 1], lru_w_i[0, 1]], axis=-1)).astype(BF16)
    bg = 0.5 * jnp.concatenate([lru_b_a[0, 0], lru_b_i[0, 0], lru_b_a[0, 1], lru_b_i[0, 1]], axis=-1)[:, None, :]
    lam = lru_lambda[0]
    agg_c = _lru_pass1_call(ctx, mv_c, ng, wrec, brec, cw, cb, wg, bg, lam, tile=ctx_len, period=ctx_len)
    agg_l = _lru_pass1_call(x, mv_l, ng, wrec, brec, cw, cb, wg, bg, lam, tile=512, period=GRID_W)
    hin = _segscan_call(agg_l, agg_c)
    x1 = _lru_pass2_call(x, mv_l, ng, wgate, bgate, wrec, brec, cw, cb, wg, bg, lam,
                         lru_w_out[0].astype(BF16), lru_b_out[0][None, :], hin, tile=512)
    x2 = _mlp_call(x1, mv_l, norm_g[0, 1][None, :], mlp_w1[0].astype(BF16), mlp_w2[0].astype(BF16), tile=512)

    mv1, _ = _mod_rows(mod[1], nb, d)
    v, xa, xb = _hyproj_call(x2, mv1, norm_g[1, 0][None, :], hy_w_in[0].astype(BF16), hy_b_in[0][None, :],
                             hy_conv_w[0], hy_conv_b[0][None, :], tile=512)
    pos = _filter_positions(seq)
    pe = _round_up(pos.shape[1], 128)
    pos = jnp.pad(pos, ((0, 0), (0, pe - pos.shape[1])))
    fw1 = jnp.pad(hy_fw1[0], ((0, pe - hy_fw1.shape[1]), (0, 0)))
    deltas = jnp.abs(jnp.linspace(math.log(FILTER_TARGET) / SLOW_DECAY_PCT,
                                  math.log(FILTER_TARGET) / FAST_DECAY_PCT, d, dtype=F32))[None, :]
    hraw, nrm = _filter_call(pos, fw1, hy_fb1[0][None, :], hy_fw2[0], hy_fb2[0][None, :], hy_fw3[0],
                             hy_fb3[0][None, :], hy_fw4[0], hy_freq[0][None, :], deltas, tile=512)
    kf = _filtfft_call(hraw, nrm, seq, ct=128)
    v1 = _longconv_call(v, xa, kf, 0, hy_skip[0, 0][None, :], seq, ct=128)
    v2 = _longconv_call(v1, xb, kf, 1, hy_skip[0, 1][None, :], seq, ct=128)
    return _mlp_call(x2, mv1, norm_g[1, 1][None, :], mlp_w1[1].astype(BF16), mlp_w2[1].astype(BF16), tile=512,
                     pre=(v2, hy_w_out[0].astype(BF16), hy_b_out[0][None, :]), final_g=final_g[None, :])
```

```python
import functools
import math

import numpy as np
import jax
import jax.numpy as jnp
from jax import lax
from jax.experimental import pallas as pl
from jax.experimental.pallas import tpu as pltpu

F32 = jnp.float32
BF16 = jnp.bfloat16
HIGHEST = lax.Precision.HIGHEST

NORM_EPS = 1e-6
GRID_W = 64
LRU_HEADS = 4
LRU_C = 8.0
LRU_CONV_LEFT = 2
HYENA_CONV_LEFT = 1
FILTER_BANDS = 16
FILTER_TARGET = 1e-2
FAST_DECAY_PCT = 0.3
SLOW_DECAY_PCT = 1.5

SUBLANES = 8
LANES = 128
NSEG = SUBLANES
BF16_ROWS = 16
SCAN_UNROLL = 8
VMEM_LIMIT = 58 * 1024 * 1024

DFT_N2 = GRID_W
DFT_BLOCKS = 2
SPEC_PAD = 8
SPEC_PITCH = 2 * DFT_N2 + SPEC_PAD
SLAB_UNROLL = 8
FREQ_UNROLL = 8


def _cparams(sem):
    return pltpu.CompilerParams(dimension_semantics=sem, vmem_limit_bytes=VMEM_LIMIT)


def _const_spec(shape):
    nd = len(shape)
    return pl.BlockSpec(shape, lambda *_: (0,) * nd, pipeline_mode=pl.Buffered(1))


def _round_up(a, m):
    return (a + m - 1) // m * m


def _rms_norm(x, g):
    ms = jnp.mean(x * x, axis=-1, keepdims=True)
    return (x * lax.rsqrt(ms + NORM_EPS)) * g


def _modulate(x, g, shift, scale):
    return _rms_norm(x, g) * (1.0 + scale) + shift


def _gelu_tanh(x):
    c = math.sqrt(2.0 / math.pi)
    return x * (0.5 * (1.0 + jnp.tanh(c * (x + 0.044715 * (x * x * x)))))


def _softplus(x):
    return jnp.maximum(x, 0.0) + jnp.log1p(jnp.exp(-jnp.abs(x)))


def _bdot(a, b):
    return jnp.dot(a, b, preferred_element_type=F32)


def _to_tile_order(x, groups):
    n, d = x.shape
    return pltpu.einshape("gtd->tgd", x.reshape(groups, n // groups, d)).reshape(n, d)


def _from_tile_order(x, groups):
    n, d = x.shape
    return pltpu.einshape("tgd->gtd", x.reshape(n // groups, groups, d)).reshape(n, d)


def _wrapped_edge(edge, step):
    pieces = []
    for p in range(edge.shape[0] // SUBLANES):
        piece = edge[p * SUBLANES:(p + 1) * SUBLANES]
        sub = lax.broadcasted_iota(jnp.int32, piece.shape, 0)
        if step > 0:
            pieces.append(jnp.where(sub == 0, 0.0, pltpu.roll(piece, 1, 0)))
        else:
            pieces.append(jnp.where(sub == SUBLANES - 1, 0.0, pltpu.roll(piece, SUBLANES - 1, 0)))
    return jnp.concatenate(pieces, axis=0) if len(pieces) > 1 else pieces[0]


def _shift_tokens(z, o, groups, wrap):
    n = abs(o) * groups
    rows = z.shape[0]
    if o < 0:
        edge = _wrapped_edge(z[rows - n:], 1) if wrap else jnp.zeros((n, z.shape[1]), z.dtype)
        return jnp.concatenate([edge, z[:rows - n]], axis=0)
    edge = _wrapped_edge(z[:n], -1) if wrap else jnp.zeros((n, z.shape[1]), z.dtype)
    return jnp.concatenate([z[n:], edge], axis=0)


def _row_conv(z, w, b, left, groups, wrap=False):
    acc = b + w[left:left + 1] * z
    for k in range(w.shape[0]):
        if k != left:
            acc = acc + w[k:k + 1] * _shift_tokens(z, k - left, groups, wrap)
    return acc


def _ada_kernel(c_ref, w_ref, b_ref, o_ref):
    c = c_ref[...]
    cond = c * jax.nn.sigmoid(c)
    o_ref[0] = jnp.dot(cond, w_ref[0], preferred_element_type=F32, precision=HIGHEST) + b_ref[0]


def _ada_call(cvec, ada_w, ada_b):
    depth, d, n = ada_w.shape
    tn = 1536
    return pl.pallas_call(
        _ada_kernel,
        grid=(depth, n // tn),
        in_specs=[
            pl.BlockSpec((SUBLANES, d), lambda i, j: (0, 0)),
            pl.BlockSpec((1, d, tn), lambda i, j: (i, 0, j)),
            pl.BlockSpec((1, 1, tn), lambda i, j: (i, 0, j)),
        ],
        out_specs=pl.BlockSpec((1, SUBLANES, tn), lambda i, j: (i, 0, j)),
        out_shape=jax.ShapeDtypeStruct((depth, SUBLANES, n), F32),
        compiler_params=_cparams(("parallel", "parallel")),
    )(cvec, ada_w, ada_b.reshape(depth, 1, n))


def _lru_coeffs(xp, mv, ng, w_rec, b_rec, cw, cb, wg, bg, lam, ab_scr, *, wrap):
    w = w_rec.shape[1]
    hb = w // LRU_HEADS
    u = _modulate(xp, ng, mv[0:1], mv[1:2]).astype(BF16)
    zr = _bdot(u, w_rec) + b_rec
    xl = _row_conv(zr, cw, cb, LRU_CONV_LEFT, NSEG, wrap)
    half_c_sp = (-0.5 * LRU_C) * _softplus(-lam)
    for h in range(LRU_HEADS):
        cs = slice(h * hb, (h + 1) * hb)
        xh = xl[:, cs]
        gates = _bdot(xh.astype(BF16), wg[h]) + bg[h]
        xh_half = 0.5 * xh
        for e in range(2):
            tr = jnp.tanh(gates[:, (2 * e) * hb:(2 * e + 1) * hb])
            ti = jnp.tanh(gates[:, (2 * e + 1) * hb:(2 * e + 2) * hb])
            c = half_c_sp[e:e + 1, cs]
            log_a = c * tr + c
            th = jnp.tanh(log_a)
            q = (-2.0 * th) / (1.0 - th)
            wgt = jnp.where(q > 0.0, q * lax.rsqrt(q), 0.0) * xh_half
            ab_scr[e, 0, :, cs] = jnp.exp(log_a)
            ab_scr[e, 1, :, cs] = wgt * ti + wgt


def _seg_rows(i):
    return pl.ds(pl.multiple_of(i * NSEG, NSEG), NSEG)


def _lru_pass1_kernel(x_ref, mv_ref, ng_ref, wrec_ref, brec_ref, cw_ref, cb_ref, wg_ref, bg_ref,
                      lam_ref, agg_ref, ab_scr, *, wrap):
    seg_len = x_ref.shape[1] // NSEG
    xp = _to_tile_order(x_ref[0], NSEG)
    _lru_coeffs(xp, mv_ref[0], ng_ref[...], wrec_ref[...], brec_ref[...], cw_ref[...], cb_ref[...],
                wg_ref, bg_ref, lam_ref[...], ab_scr, wrap=wrap)
    w = ab_scr.shape[-1]

    def body(i, carry):
        pf, hf, pb, hb = carry
        rf = _seg_rows(i)
        rb = _seg_rows(seg_len - 1 - i)
        af = ab_scr[0, 0, rf, :]
        ab = ab_scr[1, 0, rb, :]
        return pf * af, af * hf + ab_scr[0, 1, rf, :], pb * ab, ab * hb + ab_scr[1, 1, rb, :]

    one = jnp.ones((NSEG, w), F32)
    zero = jnp.zeros((NSEG, w), F32)
    pf, hf, pb, hb = lax.fori_loop(0, seg_len, body, (one, zero, one, zero), unroll=SCAN_UNROLL)
    agg_ref[0, 0, 0] = pf
    agg_ref[0, 0, 1] = hf
    agg_ref[0, 1, 0] = pb
    agg_ref[0, 1, 1] = hb


def _lru_weight_specs(d, w, cw, wg, bg, lam):
    return [
        _const_spec((1, d)),
        _const_spec((d, w)),
        _const_spec((1, w)),
        _const_spec(cw.shape),
        _const_spec((1, w)),
        _const_spec(wg.shape),
        _const_spec(bg.shape),
        _const_spec(lam.shape),
    ]


def _lru_pass1_call(x, mv, ng, wrec, brec, cw, cb, wg, bg, lam, *, tile, wrap):
    b, s, d = x.shape
    w = wrec.shape[1]
    nt = s // tile
    return pl.pallas_call(
        functools.partial(_lru_pass1_kernel, wrap=wrap),
        grid=(b, nt),
        in_specs=[
            pl.BlockSpec((1, tile, d), lambda i, j: (i, j, 0)),
            pl.BlockSpec((1, SUBLANES, d), lambda i, j: (i, 0, 0)),
        ] + _lru_weight_specs(d, w, cw, wg, bg, lam),
        out_specs=pl.BlockSpec((1, 2, 2, NSEG, w), lambda i, j: (i, 0, 0, j, 0)),
        out_shape=jax.ShapeDtypeStruct((b, 2, 2, nt * NSEG, w), F32),
        scratch_shapes=[pltpu.VMEM((2, 2, tile, w), F32)],
        compiler_params=_cparams(("parallel", "parallel")),
    )(x, mv, ng, wrec, brec, cw, cb, wg, bg, lam)


def _segscan_kernel(aggl_ref, aggc_ref, hin_ref):
    nsl = aggl_ref.shape[3]
    nsc = aggc_ref.shape[3]
    w = aggl_ref.shape[-1]
    for e in range(2):
        order_c = range(nsc) if e == 0 else range(nsc - 1, -1, -1)
        order_l = range(nsl) if e == 0 else range(nsl - 1, -1, -1)
        st = jnp.zeros((1, w), F32)
        for s in order_c:
            st = aggc_ref[0, e, 0, s:s + 1, :] * st + aggc_ref[0, e, 1, s:s + 1, :]
        for s in order_l:
            hin_ref[0, e, s:s + 1, :] = st
            st = aggl_ref[0, e, 0, s:s + 1, :] * st + aggl_ref[0, e, 1, s:s + 1, :]


def _segscan_call(agg_l, agg_c):
    b, _, _, nsl, w = agg_l.shape
    nsc = agg_c.shape[3]
    return pl.pallas_call(
        _segscan_kernel,
        grid=(b,),
        in_specs=[
            pl.BlockSpec((1, 2, 2, nsl, w), lambda i: (i, 0, 0, 0, 0)),
            pl.BlockSpec((1, 2, 2, nsc, w), lambda i: (i, 0, 0, 0, 0)),
        ],
        out_specs=pl.BlockSpec((1, 2, nsl, w), lambda i: (i, 0, 0, 0)),
        out_shape=jax.ShapeDtypeStruct((b, 2, nsl, w), F32),
        compiler_params=_cparams(("parallel",)),
    )(agg_l, agg_c)


def _lru_pass2_kernel(x_ref, mv_ref, ng_ref, wrec_ref, brec_ref, cw_ref, cb_ref, wg_ref, bg_ref,
                      lam_ref, wgate_ref, bgate_ref, wout_ref, bout_ref, hin_ref, o_ref,
                      ab_scr, gate_scr):
    seg_len = x_ref.shape[1] // NSEG
    xp = _to_tile_order(x_ref[0], NSEG)
    mv = mv_ref[0]
    ng = ng_ref[...]
    u = _modulate(xp, ng, mv[0:1], mv[1:2]).astype(BF16)
    gate_scr[...] = _gelu_tanh(_bdot(u, wgate_ref[...]) + bgate_ref[...])
    _lru_coeffs(xp, mv, ng, wrec_ref[...], brec_ref[...], cw_ref[...], cb_ref[...], wg_ref, bg_ref,
                lam_ref[...], ab_scr, wrap=False)

    def body(i, carry):
        hf, hb = carry
        rf = _seg_rows(i)
        rb = _seg_rows(seg_len - 1 - i)
        hf = ab_scr[0, 0, rf, :] * hf + ab_scr[0, 1, rf, :]
        ab_scr[0, 0, rf, :] = hf
        hb = ab_scr[1, 0, rb, :] * hb + ab_scr[1, 1, rb, :]
        ab_scr[1, 0, rb, :] = hb
        return hf, hb

    lax.fori_loop(0, seg_len, body, (hin_ref[0, 0], hin_ref[0, 1]), unroll=SCAN_UNROLL)
    yg = ((ab_scr[0, 0] + ab_scr[1, 0]) * gate_scr[...]).astype(BF16)
    y = _bdot(yg, wout_ref[...]) + bout_ref[...]
    o_ref[0] = (xp + mv[2:3] * y).reshape(o_ref.shape[1:])


def _lru_pass2_call(x, mv, ng, wrec, brec, cw, cb, wg, bg, lam, wgate, bgate, wout, bout, hin, *, tile):
    b, s, d = x.shape
    w = wrec.shape[1]
    nt = s // tile
    return pl.pallas_call(
        _lru_pass2_kernel,
        grid=(b, nt),
        in_specs=[
            pl.BlockSpec((1, tile, d), lambda i, j: (i, j, 0)),
            pl.BlockSpec((1, SUBLANES, d), lambda i, j: (i, 0, 0)),
        ] + _lru_weight_specs(d, w, cw, wg, bg, lam) + [
            _const_spec((d, w)),
            _const_spec((1, w)),
            _const_spec((w, d)),
            _const_spec((1, d)),
            pl.BlockSpec((1, 2, NSEG, w), lambda i, j: (i, 0, j, 0)),
        ],
        out_specs=pl.BlockSpec((1, GRID_W, NSEG, d), lambda i, j: (i, 0, j, 0)),
        out_shape=jax.ShapeDtypeStruct((b, GRID_W, s // GRID_W, d), F32),
        scratch_shapes=[
            pltpu.VMEM((2, 2, tile, w), F32),
            pltpu.VMEM((tile, w), F32),
        ],
        compiler_params=_cparams(("parallel", "parallel")),
    )(x, mv, ng, wrec, brec, cw, cb, wg, bg, lam, wgate, bgate, wout, bout, hin)


FF_CHUNK = 1024


def _mlp_kernel(*refs, pre, final):
    refs = list(refs)
    x_ref = refs.pop(0)
    mv_ref = refs.pop(0)
    ng_ref = refs.pop(0)
    w1_ref = refs.pop(0)
    w2_ref = refs.pop(0)
    if pre:
        v_ref = refs.pop(0)
        wout_ref = refs.pop(0)
        bout_ref = refs.pop(0)
    if final:
        fg_ref = refs.pop(0)
    o_ref = refs.pop(0)
    mv = mv_ref[0]
    groups = x_ref.shape[2]
    d = x_ref.shape[3]
    x = x_ref[0].reshape(GRID_W * groups, d)
    if pre:
        v = v_ref[0].reshape(GRID_W * groups, d)
        x = x + mv[2:3] * (_bdot(v, wout_ref[...]) + bout_ref[...])
    u = _modulate(x, ng_ref[...], mv[3:4], mv[4:5]).astype(BF16)
    acc = jnp.zeros(x.shape, F32)
    for c in range(w1_ref.shape[1] // FF_CHUNK):
        cs = slice(c * FF_CHUNK, (c + 1) * FF_CHUNK)
        h = jnp.maximum(_bdot(u, w1_ref[:, cs]), 0.0)
        acc = acc + _bdot((h * h).astype(BF16), w2_ref[cs, :])
    out = x + mv[5:6] * acc
    if final:
        o_ref[0] = _from_tile_order(_rms_norm(out, fg_ref[...]), groups)
    else:
        o_ref[0] = out.reshape(o_ref.shape[1:])


def _mlp_call(x, mv, ng, w1, w2, *, groups, pre=None, final_g=None):
    b, _, rows, d = x.shape
    f = w1.shape[1]
    tile_spec = pl.BlockSpec((1, GRID_W, groups, d), lambda i, j: (i, 0, j, 0))
    args = [x, mv, ng, w1, w2]
    in_specs = [
        tile_spec,
        pl.BlockSpec((1, SUBLANES, d), lambda i, j: (i, 0, 0)),
        _const_spec((1, d)),
        _const_spec((d, f)),
        _const_spec((f, d)),
    ]
    if pre is not None:
        v, wout, bout = pre
        args += [v, wout, bout]
        in_specs += [tile_spec, _const_spec((d, d)), _const_spec((1, d))]
    if final_g is not None:
        args.append(final_g)
        in_specs.append(_const_spec((1, d)))
        out_spec = pl.BlockSpec((1, GRID_W * groups, d), lambda i, j: (i, j, 0))
        out_shape = jax.ShapeDtypeStruct((b, GRID_W * rows, d), F32)
    else:
        out_spec = tile_spec
        out_shape = jax.ShapeDtypeStruct(x.shape, F32)
    return pl.pallas_call(
        functools.partial(_mlp_kernel, pre=pre is not None, final=final_g is not None),
        grid=(b, rows // groups),
        in_specs=in_specs,
        out_specs=out_spec,
        out_shape=out_shape,
        compiler_params=_cparams(("parallel", "parallel")),
    )(*args)


def _hyproj_kernel(x_ref, mv_ref, ng_ref, win_ref, bin_ref, cw_ref, cb_ref, v_ref, xa_ref, xb_ref):
    mv = mv_ref[0]
    groups = x_ref.shape[2]
    d = x_ref.shape[3]
    x = x_ref[0].reshape(GRID_W * groups, d)
    u = _modulate(x, ng_ref[...], mv[0:1], mv[1:2]).astype(BF16)
    for k, o_ref in enumerate((v_ref, xa_ref, xb_ref)):
        cs = slice(k * d, (k + 1) * d)
        z = _bdot(u, win_ref[:, cs]) + bin_ref[:, cs]
        z = _row_conv(z, cw_ref[:, cs], cb_ref[:, cs], HYENA_CONV_LEFT, groups)
        o_ref[0] = z.astype(BF16).reshape(o_ref.shape[1:])


def _hyproj_call(x, mv, ng, win, bin_, cw, cb, *, groups):
    b, _, rows, d = x.shape
    tile_spec = pl.BlockSpec((1, GRID_W, groups, d), lambda i, j: (i, 0, j, 0))
    out_sds = jax.ShapeDtypeStruct(x.shape, BF16)
    return pl.pallas_call(
        _hyproj_kernel,
        grid=(b, rows // groups),
        in_specs=[
            tile_spec,
            pl.BlockSpec((1, SUBLANES, d), lambda i, j: (i, 0, 0)),
            _const_spec((1, d)),
            _const_spec(win.shape),
            _const_spec(bin_.shape),
            _const_spec(cw.shape),
            _const_spec(cb.shape),
        ],
        out_specs=[tile_spec, tile_spec, tile_spec],
        out_shape=[out_sds, out_sds, out_sds],
        compiler_params=_cparams(("parallel", "parallel")),
    )(x, mv, ng, win, bin_, cw, cb)


def _filter_kernel(pos_ref, fw1_ref, fb1_ref, fw2_ref, fb2_ref, fw3_ref, fb3_ref, fw4_ref, freq_ref,
                   deltas_ref, h_ref, nrm_ref):
    groups = pos_ref.shape[1]
    pos = pos_ref[...].reshape(GRID_W * groups, pos_ref.shape[2])
    freq = freq_ref[...]

    def hdot(a, b):
        return jnp.dot(a, b, preferred_element_type=F32, precision=HIGHEST)

    h = jnp.sin(freq * (hdot(pos, fw1_ref[...]) + fb1_ref[...]))
    h = jnp.sin(freq * (hdot(h, fw2_ref[...]) + fb2_ref[...]))
    h = jnp.sin(freq * (hdot(h, fw3_ref[...]) + fb3_ref[...]))
    d = deltas_ref.shape[1]
    decay = jnp.exp(-pos[:, 0:1] * deltas_ref[...])
    nparts = fw4_ref.shape[1] // d
    sums = []
    for p in range(nparts):
        cs = slice(p * d, (p + 1) * d)
        hp = _bdot(h.astype(BF16), fw4_ref[:, cs]) * decay
        sums.append(jnp.sum(jnp.abs(hp), axis=0, keepdims=True))
        h_ref[:, :, cs] = hp.astype(BF16).reshape(GRID_W, groups, d)
    half = nparts // 2
    tot = jnp.concatenate([sums[p] + sums[p + half] for p in range(half)], axis=1)

    @pl.when(pl.program_id(0) == 0)
    def _():
        nrm_ref[...] = jnp.zeros_like(nrm_ref)

    nrm_ref[...] += tot


def _filter_call(pos, fw1, fb1, fw2, fb2, fw3, fb3, fw4, freq, deltas, *, groups):
    _, rows, pe = pos.shape
    fh = fw2.shape[0]
    n4 = fw4.shape[1]
    d = deltas.shape[1]
    return pl.pallas_call(
        _filter_kernel,
        grid=(rows // groups,),
        in_specs=[
            pl.BlockSpec((GRID_W, groups, pe), lambda j: (0, j, 0)),
            _const_spec((pe, fh)), _const_spec((1, fh)),
            _const_spec((fh, fh)), _const_spec((1, fh)),
            _const_spec((fh, fh)), _const_spec((1, fh)),
            _const_spec((fh, n4)), _const_spec((1, fh)),
            _const_spec((1, d)),
        ],
        out_specs=[
            pl.BlockSpec((GRID_W, groups, n4), lambda j: (0, j, 0)),
            pl.BlockSpec((1, n4 // 2), lambda j: (0, 0)),
        ],
        out_shape=[
            jax.ShapeDtypeStruct((GRID_W, rows, n4), BF16),
            jax.ShapeDtypeStruct((1, n4 // 2), F32),
        ],
        compiler_params=_cparams(("arbitrary",)),
    )(pos, fw1, fb1, fw2, fb2, fw3, fb3, fw4, freq, deltas)


@functools.lru_cache(maxsize=None)
def _dft_constants(seq_len):
    n = 2 * seq_len
    n2 = DFT_N2
    n1 = n // n2
    nt1 = n1 // 2
    nf = n1 // 2 + 1
    slots = _round_up(-(-nf // DFT_BLOCKS), SUBLANES)
    nfp = DFT_BLOCKS * slots
    t1 = np.arange(nt1)[None, :]
    f1 = np.arange(nfp)[:, None]
    live = (f1 < nf).astype(np.float64)
    ang1 = 2.0 * np.pi * (t1 * f1 % n1) / n1
    cos1, sin1 = np.cos(ang1) * live, np.sin(ang1) * live
    cf = np.full((nfp, 1), 2.0)
    cf[0] = 1.0
    cf[nf - 1] = 1.0
    f1h = np.zeros((DFT_BLOCKS, 2 * slots, nt1))
    f1i = np.zeros((DFT_BLOCKS, nt1, 2 * slots))
    for kb in range(DFT_BLOCKS):
        blk = slice(kb * slots, (kb + 1) * slots)
        f1h[kb, :slots] = cos1[blk]
        f1h[kb, slots:] = -sin1[blk]
        f1i[kb, :, :slots] = (cos1[blk] * cf[blk]).T
        f1i[kb, :, slots:] = (-sin1[blk] * cf[blk]).T
    t2 = np.arange(n2)[None, None, :]
    f2 = np.arange(n2)[None, :, None]
    ff1 = np.arange(nf)[:, None, None]
    ang2 = 2.0 * np.pi * ((t2 * (ff1 + n1 * f2)) % n) / n
    gr, gim = np.cos(ang2), -np.sin(ang2)
    g = np.concatenate([np.concatenate([gr, -gim], axis=2),
                        np.concatenate([gim, gr], axis=2)], axis=1)
    as32 = lambda a: np.ascontiguousarray(a, dtype=np.float32)
    return as32(f1h), as32(g), as32(f1i), nt1, nf, slots


def _spec_rows(slot0, t2, slots):
    return pl.ds(slot0 * SPEC_PITCH + t2, slots, stride=SPEC_PITCH)


def _dft_stage1(load_slab, f1h_ref, kb, spec_scr, slot0, slots):
    def body(t2, carry):
        a = _bdot(f1h_ref[kb], load_slab(t2))
        for lt in range(2):
            ls = slice(lt * LANES, (lt + 1) * LANES)
            spec_scr[lt, _spec_rows(slot0, t2, slots), :] = a[:slots, ls]
            spec_scr[lt, _spec_rows(slot0, t2 + DFT_N2, slots), :] = a[slots:, ls]
        return carry

    lax.fori_loop(0, DFT_N2, body, 0, unroll=SLAB_UNROLL)


def _spec_slot_load(spec_scr, slot):
    base = pl.multiple_of(slot * SPEC_PITCH, SUBLANES)
    rows = pl.ds(base, 2 * DFT_N2)
    return jnp.concatenate([spec_scr[0, rows, :], spec_scr[1, rows, :]], axis=1), rows


def _filtfft_kernel(hf_ref, hb_ref, nrm_ref, f1h_ref, g_ref, k_ref, spec_scr, *, nf, slots):
    inv = 1.0 / nrm_ref[...]

    def slab(t2):
        return jnp.concatenate([hf_ref[t2], hb_ref[t2]], axis=1)

    for kb in range(DFT_BLOCKS):
        _dft_stage1(slab, f1h_ref, kb, spec_scr, kb * slots, slots)

    def body(f1, carry):
        a, _ = _spec_slot_load(spec_scr, f1)
        xs = _bdot(g_ref[f1], a.astype(BF16))
        fwd, bwd = xs[:, :LANES], xs[:, LANES:]
        k_ref[0, f1] = jnp.concatenate(
            [fwd[:DFT_N2] + bwd[:DFT_N2], fwd[DFT_N2:] - bwd[DFT_N2:]], axis=0) * inv
        return carry

    lax.fori_loop(0, nf, body, 0, unroll=FREQ_UNROLL)


def _filtfft_call(hraw, nrm, seq_len):
    f1h, g, _, nt1, nf, slots = _dft_constants(seq_len)
    _, rows, n4 = hraw.shape
    d = n4 // 4
    nct = d // LANES
    return pl.pallas_call(
        functools.partial(_filtfft_kernel, nf=nf, slots=slots),
        grid=(2, nct),
        in_specs=[
            pl.BlockSpec((GRID_W, rows, LANES), lambda o, c: (0, 0, o * nct + c)),
            pl.BlockSpec((GRID_W, rows, LANES), lambda o, c: (0, 0, 2 * nct + o * nct + c)),
            pl.BlockSpec((1, LANES), lambda o, c: (0, o * nct + c)),
            _const_spec(f1h.shape),
            _const_spec(g.shape),
        ],
        out_specs=pl.BlockSpec((1, nf, 2 * DFT_N2, LANES), lambda o, c: (o, 0, 0, c)),
        out_shape=jax.ShapeDtypeStruct((2, nf, 2 * DFT_N2, d), F32),
        scratch_shapes=[pltpu.VMEM((2, DFT_BLOCKS * slots * SPEC_PITCH, LANES), F32)],
        compiler_params=_cparams(("parallel", "parallel")),
    )(hraw, hraw, nrm, jnp.asarray(f1h).astype(BF16), jnp.asarray(g).astype(BF16))


def _longconv_kernel(v_ref, m_ref, k_ref, skip_ref, f1h_ref, g_ref, f1i_ref, o_ref,
                     spec_scr, acc_scr, *, nf, slots):
    def slab(t2):
        return jnp.concatenate([v_ref[0, t2], v_ref[1, t2]], axis=1)

    skip = skip_ref[...]
    for kb in range(DFT_BLOCKS):
        _dft_stage1(slab, f1h_ref, kb, spec_scr, 0, slots)
        f1_lo = kb * slots

        def mid(s, carry):
            a, rows = _spec_slot_load(spec_scr, s)
            xs = _bdot(g_ref[f1_lo + s], a.astype(BF16))
            kf = k_ref[0, f1_lo + s]
            kr = jnp.concatenate([kf[:DFT_N2]] * 2, axis=1)
            ki = jnp.concatenate([kf[DFT_N2:]] * 2, axis=1)
            xr, xi = xs[:DFT_N2], xs[DFT_N2:]
            ys = jnp.concatenate([xr * kr - xi * ki, xr * ki + xi * kr], axis=0).astype(BF16)
            bs = lax.dot_general(g_ref[f1_lo + s], ys, (((0,), (0,)), ((), ())),
                                 preferred_element_type=F32)
            spec_scr[0, rows, :] = bs[:, :LANES]
            spec_scr[1, rows, :] = bs[:, LANES:]
            return carry

        lax.fori_loop(0, min(slots, nf - f1_lo), mid, 0, unroll=FREQ_UNROLL)

        def last(t2, carry):
            halves = []
            for lt in range(2):
                re = spec_scr[lt, _spec_rows(0, t2, slots), :]
                im = spec_scr[lt, _spec_rows(0, t2 + DFT_N2, slots), :]
                halves.append(jnp.concatenate([re, im], axis=0))
            y = _bdot(f1i_ref[kb], jnp.concatenate(halves, axis=1).astype(BF16))
            if kb == 0:
                acc_scr[t2] = y
            elif kb < DFT_BLOCKS - 1:
                acc_scr[t2] = acc_scr[t2] + y
            else:
                tot = acc_scr[t2] + y
                for b in range(2):
                    conv = tot[:, b * LANES:(b + 1) * LANES]
                    vs = v_ref[b, t2].astype(F32)
                    o_ref[b, t2] = (m_ref[b, t2].astype(F32) * (conv + vs * skip)).astype(BF16)
            return carry

        lax.fori_loop(0, DFT_N2, last, 0, unroll=SLAB_UNROLL)


def _longconv_call(v, m, kf, order, skip, seq_len):
    f1h, g, f1i, nt1, nf, slots = _dft_constants(seq_len)
    b, _, rows, d = v.shape
    nct = d // LANES
    seq_block = (2, GRID_W, rows, LANES)
    scale = 1.0 / (2 * seq_len)
    return pl.pallas_call(
        functools.partial(_longconv_kernel, nf=nf, slots=slots),
        grid=(nct, b // 2),
        in_specs=[
            pl.BlockSpec(seq_block, lambda c, i: (i, 0, 0, c)),
            pl.BlockSpec(seq_block, lambda c, i: (i, 0, 0, c), pipeline_mode=pl.Buffered(1)),
            pl.BlockSpec((1, nf, 2 * DFT_N2, LANES), lambda c, i: (order, 0, 0, c),
                         pipeline_mode=pl.Buffered(1)),
            pl.BlockSpec((1, LANES), lambda c, i: (0, c)),
            _const_spec(f1h.shape),
            _const_spec(g.shape),
            _const_spec(f1i.shape),
        ],
        out_specs=pl.BlockSpec(seq_block, lambda c, i: (i, 0, 0, c)),
        out_shape=jax.ShapeDtypeStruct(v.shape, BF16),
        scratch_shapes=[
            pltpu.VMEM((2, slots * SPEC_PITCH, LANES), F32),
            pltpu.VMEM((GRID_W, rows, 2 * LANES), F32),
        ],
        compiler_params=_cparams(("parallel", "parallel")),
    )(v, m, kf, skip, jnp.asarray(f1h).astype(BF16), jnp.asarray(g).astype(BF16),
      jnp.asarray(f1i * scale).astype(BF16))


def _mod_rows(mod_layer, nb, d):
    m = mod_layer.reshape(SUBLANES, 6, d)
    m = jnp.concatenate([m, jnp.zeros((SUBLANES, SUBLANES - 6, d), F32)], axis=1)
    return m[:nb], jnp.broadcast_to(m[nb:nb + 1], (nb, SUBLANES, d))


def _filter_positions(seq_len):
    t = jnp.linspace(0.0, 1.0, seq_len, dtype=F32)[:, None]
    w = (2.0 * math.pi / seq_len) * jnp.arange(seq_len, dtype=F32)[:, None]
    bands = jnp.linspace(1e-4, FILTER_BANDS - 1, FILTER_BANDS, dtype=F32)
    return jnp.concatenate([t, jnp.cos(bands * w), -jnp.sin(bands * w)], axis=-1)


def kernel(x, c, ctx, c_ctx, ada_w, ada_b, norm_g, mlp_w1, mlp_w2, lru_w_in, lru_b_in, lru_conv_w, lru_conv_b, lru_w_a, lru_b_a, lru_w_i, lru_b_i, lru_lambda, lru_w_out, lru_b_out, hy_w_in, hy_b_in, hy_conv_w, hy_conv_b, hy_fw1, hy_fb1, hy_fw2, hy_fb2, hy_fw3, hy_fb3, hy_fw4, hy_freq, hy_skip, hy_w_out, hy_b_out, final_g):
    nb, seq, d = x.shape
    ctx_len = ctx.shape[1]
    w = lru_w_out.shape[1]
    lru_tile = NSEG * GRID_W
    assert nb + 1 <= SUBLANES and nb % 2 == 0
    assert seq % (BF16_ROWS * GRID_W) == 0 and ctx_len % (NSEG * SUBLANES) == 0

    cvec = jnp.concatenate([c, c_ctx[None, :], jnp.zeros((SUBLANES - nb - 1, d), F32)], axis=0)
    mod = _ada_call(cvec, ada_w, ada_b)

    mv_l, mv_c = _mod_rows(mod[0], nb, d)
    ng = norm_g[0, 0][None, :]
    wgate = lru_w_in[0, :, :w].astype(BF16)
    wrec = lru_w_in[0, :, w:].astype(BF16)
    bgate = lru_b_in[0, :w][None, :]
    brec = lru_b_in[0, w:][None, :]
    cw = lru_conv_w[0]
    cb = lru_conv_b[0][None, :]
    wg = (0.5 * jnp.concatenate([lru_w_a[0, 0], lru_w_i[0, 0], lru_w_a[0, 1], lru_w_i[0, 1]], axis=-1)).astype(BF16)
    bg = 0.5 * jnp.concatenate([lru_b_a[0, 0], lru_b_i[0, 0], lru_b_a[0, 1], lru_b_i[0, 1]], axis=-1)[:, None, :]
    lam = lru_lambda[0]
    lru_w = (ng, wrec, brec, cw, cb, wg, bg, lam)
    agg_c = _lru_pass1_call(ctx, mv_c, *lru_w, tile=ctx_len, wrap=True)
    agg_l = _lru_pass1_call(x, mv_l, *lru_w, tile=lru_tile, wrap=False)
    hin = _segscan_call(agg_l, agg_c)
    x1 = _lru_pass2_call(x, mv_l, *lru_w, wgate, bgate, lru_w_out[0].astype(BF16), lru_b_out[0][None, :], hin,
                         tile=lru_tile)
    x2 = _mlp_call(x1, mv_l, norm_g[0, 1][None, :], mlp_w1[0].astype(BF16), mlp_w2[0].astype(BF16), groups=NSEG)

    mv1, _ = _mod_rows(mod[1], nb, d)
    v, xa, xb = _hyproj_call(x2, mv1, norm_g[1, 0][None, :], hy_w_in[0].astype(BF16), hy_b_in[0][None, :],
                             hy_conv_w[0], hy_conv_b[0][None, :], groups=BF16_ROWS)
    pos = _filter_positions(seq)
    pe = _round_up(pos.shape[1], LANES)
    pos = jnp.pad(pos, ((0, 0), (0, pe - pos.shape[1])))
    pos = pos.reshape(seq // GRID_W, GRID_W, pe).transpose(1, 0, 2)
    fw1 = jnp.pad(hy_fw1[0], ((0, pe - hy_fw1.shape[1]), (0, 0)))
    deltas = jnp.abs(jnp.linspace(math.log(FILTER_TARGET) / SLOW_DECAY_PCT,
                                  math.log(FILTER_TARGET) / FAST_DECAY_PCT, d, dtype=F32))[None, :]
    hraw, nrm = _filter_call(pos, fw1, hy_fb1[0][None, :], hy_fw2[0], hy_fb2[0][None, :], hy_fw3[0],
                             hy_fb3[0][None, :], hy_fw4[0].astype(BF16), hy_freq[0][None, :], deltas,
                             groups=BF16_ROWS)
    kf = _filtfft_call(hraw, nrm, seq)
    v1 = _longconv_call(v, xa, kf, 0, hy_skip[0, 0][None, :], seq)
    v2 = _longconv_call(v1, xb, kf, 1, hy_skip[0, 1][None, :], seq)
    return _mlp_call(x2, mv1, norm_g[1, 1][None, :], mlp_w1[1].astype(BF16), mlp_w2[1].astype(BF16),
                     groups=BF16_ROWS, pre=(v2, hy_w_out[0].astype(BF16), hy_b_out[0][None, :]),
                     final_g=final_g[None, :])
```

```python
import functools
import math

import numpy as np
import jax
import jax.numpy as jnp
from jax import lax
from jax.experimental import pallas as pl
from jax.experimental.pallas import tpu as pltpu

F32 = jnp.float32
BF16 = jnp.bfloat16
HIGHEST = lax.Precision.HIGHEST

NORM_EPS = 1e-6
GRID_W = 64
LRU_HEADS = 4
LRU_C = 8.0
LRU_CONV_LEFT = 2
HYENA_CONV_LEFT = 1
FILTER_BANDS = 16
FILTER_TARGET = 1e-2
FAST_DECAY_PCT = 0.3
SLOW_DECAY_PCT = 1.5

SUBLANES = 8
LANES = 128
NSEG = SUBLANES
BF16_ROWS = 16
SCAN_UNROLL = 8
VMEM_LIMIT = 58 * 1024 * 1024

DFT_N2 = GRID_W
DFT_BLOCKS = 2
SPEC_PAD = 8
SPEC_PITCH = 2 * DFT_N2 + SPEC_PAD
SLAB_UNROLL = 32
FREQ_UNROLL = 36


def _cparams(sem):
    return pltpu.CompilerParams(dimension_semantics=sem, vmem_limit_bytes=VMEM_LIMIT)


def _const_spec(shape):
    nd = len(shape)
    return pl.BlockSpec(shape, lambda *_: (0,) * nd, pipeline_mode=pl.Buffered(1))


def _round_up(a, m):
    return (a + m - 1) // m * m


def _rms_norm(x, g):
    ms = jnp.mean(x * x, axis=-1, keepdims=True)
    return (x * lax.rsqrt(ms + NORM_EPS)) * g


def _modulate(x, g, shift, scale):
    return _rms_norm(x, g) * (1.0 + scale) + shift


def _gelu_tanh(x):
    c = math.sqrt(2.0 / math.pi)
    return x * (0.5 * (1.0 + jnp.tanh(c * (x + 0.044715 * (x * x * x)))))


def _softplus(x):
    return jnp.maximum(x, 0.0) + jnp.log1p(jnp.exp(-jnp.abs(x)))


def _bdot(a, b):
    return jnp.dot(a, b, preferred_element_type=F32)


def _to_tile_order(x, groups):
    n, d = x.shape
    return jnp.swapaxes(x.reshape(groups, n // groups, d), 0, 1).reshape(n, d)


def _from_tile_order(x, groups):
    n, d = x.shape
    return jnp.swapaxes(x.reshape(n // groups, groups, d), 0, 1).reshape(n, d)


def _wrapped_edge(edge, step):
    pieces = []
    for p in range(edge.shape[0] // SUBLANES):
        piece = edge[p * SUBLANES:(p + 1) * SUBLANES]
        sub = lax.broadcasted_iota(jnp.int32, piece.shape, 0)
        if step > 0:
            pieces.append(jnp.where(sub == 0, 0.0, pltpu.roll(piece, 1, 0)))
        else:
            pieces.append(jnp.where(sub == SUBLANES - 1, 0.0, pltpu.roll(piece, SUBLANES - 1, 0)))
    return jnp.concatenate(pieces, axis=0) if len(pieces) > 1 else pieces[0]


def _shift_tokens(z, o, groups, wrap):
    n = abs(o) * groups
    rows = z.shape[0]
    if o < 0:
        edge = _wrapped_edge(z[rows - n:], 1) if wrap else jnp.zeros((n, z.shape[1]), z.dtype)
        return jnp.concatenate([edge, z[:rows - n]], axis=0)
    edge = _wrapped_edge(z[:n], -1) if wrap else jnp.zeros((n, z.shape[1]), z.dtype)
    return jnp.concatenate([z[n:], edge], axis=0)


def _row_conv(z, w, b, left, groups, wrap=False):
    acc = b + w[left:left + 1] * z
    for k in range(w.shape[0]):
        if k != left:
            acc = acc + w[k:k + 1] * _shift_tokens(z, k - left, groups, wrap)
    return acc


def _ada_kernel(c_ref, w_ref, b_ref, o_ref):
    c = c_ref[...]
    cond = c * jax.nn.sigmoid(c)
    o_ref[0] = jnp.dot(cond, w_ref[0], preferred_element_type=F32, precision=HIGHEST) + b_ref[0]


def _ada_call(cvec, ada_w, ada_b):
    depth, d, n = ada_w.shape
    tn = 1536
    return pl.pallas_call(
        _ada_kernel,
        grid=(depth, n // tn),
        in_specs=[
            pl.BlockSpec((SUBLANES, d), lambda i, j: (0, 0)),
            pl.BlockSpec((1, d, tn), lambda i, j: (i, 0, j)),
            pl.BlockSpec((1, 1, tn), lambda i, j: (i, 0, j)),
        ],
        out_specs=pl.BlockSpec((1, SUBLANES, tn), lambda i, j: (i, 0, j)),
        out_shape=jax.ShapeDtypeStruct((depth, SUBLANES, n), F32),
        compiler_params=_cparams(("parallel", "parallel")),
    )(cvec, ada_w, ada_b.reshape(depth, 1, n))


def _lru_coeffs(u, w_rec, b_rec, cw, cb, wg, bg, lam, ab_scr, *, wrap):
    w = w_rec.shape[1]
    hb = w // LRU_HEADS
    zr = _bdot(u, w_rec) + b_rec
    xl = _row_conv(zr, cw, cb, LRU_CONV_LEFT, NSEG, wrap)
    half_c_sp = (-0.5 * LRU_C) * _softplus(-lam)
    for h in range(LRU_HEADS):
        cs = slice(h * hb, (h + 1) * hb)
        xh = xl[:, cs]
        gates = _bdot(xh.astype(BF16), wg[h]) + bg[h]
        xh_half = 0.5 * xh
        for e in range(2):
            tr = jnp.tanh(gates[:, (2 * e) * hb:(2 * e + 1) * hb])
            ti = jnp.tanh(gates[:, (2 * e + 1) * hb:(2 * e + 2) * hb])
            c = half_c_sp[e:e + 1, cs]
            log_a = c * tr + c
            th = jnp.tanh(log_a)
            q = (-2.0 * th) / (1.0 - th)
            wgt = jnp.where(q > 0.0, q * lax.rsqrt(q), 0.0) * xh_half
            ab_scr[e, 0, :, cs] = jnp.exp(log_a)
            ab_scr[e, 1, :, cs] = wgt * ti + wgt


def _seg_rows(i):
    return pl.ds(pl.multiple_of(i * NSEG, NSEG), NSEG)


def _lru_local_scan(u, wrec_ref, brec_ref, cw_ref, cb_ref, wg_ref, bg_ref, lam_ref, agg_ref, ab_scr,
                    *, wrap, keep):
    seg_len = u.shape[0] // NSEG
    _lru_coeffs(u, wrec_ref[...], brec_ref[...], cw_ref[...], cb_ref[...], wg_ref, bg_ref,
                lam_ref[...], ab_scr, wrap=wrap)
    w = ab_scr.shape[-1]

    def body(i, carry):
        pf, hf, pb, hb = carry
        rf = _seg_rows(i)
        rb = _seg_rows(seg_len - 1 - i)
        af = ab_scr[0, 0, rf, :]
        ab = ab_scr[1, 0, rb, :]
        pf, hf = pf * af, af * hf + ab_scr[0, 1, rf, :]
        pb, hb = pb * ab, ab * hb + ab_scr[1, 1, rb, :]
        if keep:
            ab_scr[0, 0, rf, :] = hf
            ab_scr[0, 1, rf, :] = pf
            ab_scr[1, 0, rb, :] = hb
            ab_scr[1, 1, rb, :] = pb
        return pf, hf, pb, hb

    one = jnp.ones((NSEG, w), F32)
    zero = jnp.zeros((NSEG, w), F32)
    pf, hf, pb, hb = lax.fori_loop(0, seg_len, body, (one, zero, one, zero), unroll=SCAN_UNROLL)
    agg_ref[0, 0, 0] = pf
    agg_ref[0, 0, 1] = hf
    agg_ref[0, 1, 0] = pb
    agg_ref[0, 1, 1] = hb


def _lru_pass1_kernel(x_ref, mv_ref, ng_ref, wrec_ref, brec_ref, cw_ref, cb_ref, wg_ref, bg_ref,
                      lam_ref, agg_ref, ab_scr, *, wrap):
    mv = mv_ref[0]
    xp = _to_tile_order(x_ref[0], NSEG)
    u = _modulate(xp, ng_ref[...], mv[0:1], mv[1:2]).astype(BF16)
    _lru_local_scan(u, wrec_ref, brec_ref, cw_ref, cb_ref, wg_ref, bg_ref, lam_ref, agg_ref, ab_scr,
                    wrap=wrap, keep=False)


def _lru_mix_kernel(x_ref, mv_ref, ng_ref, wrec_ref, brec_ref, cw_ref, cb_ref, wg_ref, bg_ref,
                    lam_ref, wgate_ref, bgate_ref, agg_ref, pq_ref, ab_scr):
    mv = mv_ref[0]
    xp = _to_tile_order(x_ref[0], NSEG)
    u = _modulate(xp, ng_ref[...], mv[0:1], mv[1:2]).astype(BF16)
    _lru_local_scan(u, wrec_ref, brec_ref, cw_ref, cb_ref, wg_ref, bg_ref, lam_ref, agg_ref, ab_scr,
                    wrap=False, keep=True)
    gate = _gelu_tanh(_bdot(u, wgate_ref[...]) + bgate_ref[...])
    pq_ref[0, 0, 0] = ((ab_scr[0, 0] + ab_scr[1, 0]) * gate).astype(BF16)
    pq_ref[0, 0, 1] = (ab_scr[0, 1] * gate).astype(BF16)
    pq_ref[0, 0, 2] = (ab_scr[1, 1] * gate).astype(BF16)


def _lru_weight_specs(d, w, cw, wg, bg, lam):
    return [
        _const_spec((1, d)),
        _const_spec((d, w)),
        _const_spec((1, w)),
        _const_spec(cw.shape),
        _const_spec((1, w)),
        _const_spec(wg.shape),
        _const_spec(bg.shape),
        _const_spec(lam.shape),
    ]


def _lru_pass1_call(x, mv, ng, wrec, brec, cw, cb, wg, bg, lam, *, tile, wrap):
    b, s, d = x.shape
    w = wrec.shape[1]
    nt = s // tile
    return pl.pallas_call(
        functools.partial(_lru_pass1_kernel, wrap=wrap),
        grid=(b, nt),
        in_specs=[
            pl.BlockSpec((1, tile, d), lambda i, j: (i, j, 0)),
            pl.BlockSpec((1, SUBLANES, d), lambda i, j: (i, 0, 0)),
        ] + _lru_weight_specs(d, w, cw, wg, bg, lam),
        out_specs=pl.BlockSpec((1, 2, 2, NSEG, w), lambda i, j: (i, 0, 0, j, 0)),
        out_shape=jax.ShapeDtypeStruct((b, 2, 2, nt * NSEG, w), F32),
        scratch_shapes=[pltpu.VMEM((2, 2, tile, w), F32)],
        compiler_params=_cparams(("parallel", "parallel")),
    )(x, mv, ng, wrec, brec, cw, cb, wg, bg, lam)


def _segscan_kernel(aggl_ref, aggc_ref, hin_ref):
    nsl = aggl_ref.shape[3]
    nsc = aggc_ref.shape[3]
    w = aggl_ref.shape[-1]
    for e in range(2):
        order_c = range(nsc) if e == 0 else range(nsc - 1, -1, -1)
        order_l = range(nsl) if e == 0 else range(nsl - 1, -1, -1)
        st = jnp.zeros((1, w), F32)
        for s in order_c:
            st = aggc_ref[0, e, 0, s:s + 1, :] * st + aggc_ref[0, e, 1, s:s + 1, :]
        for s in order_l:
            hin_ref[0, e, s:s + 1, :] = st
            st = aggl_ref[0, e, 0, s:s + 1, :] * st + aggl_ref[0, e, 1, s:s + 1, :]


def _segscan_call(agg_l, agg_c):
    b, _, _, nsl, w = agg_l.shape
    nsc = agg_c.shape[3]
    return pl.pallas_call(
        _segscan_kernel,
        grid=(b,),
        in_specs=[
            pl.BlockSpec((1, 2, 2, nsl, w), lambda i: (i, 0, 0, 0, 0)),
            pl.BlockSpec((1, 2, 2, nsc, w), lambda i: (i, 0, 0, 0, 0)),
        ],
        out_specs=pl.BlockSpec((1, 2, nsl, w), lambda i: (i, 0, 0, 0)),
        out_shape=jax.ShapeDtypeStruct((b, 2, nsl, w), F32),
        compiler_params=_cparams(("parallel",)),
    )(agg_l, agg_c)


def _lru_mix_call(x, mv, ng, wrec, brec, cw, cb, wg, bg, lam, wgate, bgate, *, tile):
    b, s, d = x.shape
    w = wrec.shape[1]
    nt = s // tile
    return pl.pallas_call(
        _lru_mix_kernel,
        grid=(b, nt),
        in_specs=[
            pl.BlockSpec((1, tile, d), lambda i, j: (i, j, 0)),
            pl.BlockSpec((1, SUBLANES, d), lambda i, j: (i, 0, 0)),
        ] + _lru_weight_specs(d, w, cw, wg, bg, lam) + [
            _const_spec((d, w)),
            _const_spec((1, w)),
        ],
        out_specs=[
            pl.BlockSpec((1, 2, 2, NSEG, w), lambda i, j: (i, 0, 0, j, 0)),
            pl.BlockSpec((1, 1, 3, tile, w), lambda i, j: (i, j, 0, 0, 0)),
        ],
        out_shape=[
            jax.ShapeDtypeStruct((b, 2, 2, nt * NSEG, w), F32),
            jax.ShapeDtypeStruct((b, nt, 3, tile, w), BF16),
        ],
        scratch_shapes=[pltpu.VMEM((2, 2, tile, w), F32)],
        compiler_params=_cparams(("parallel", "parallel")),
    )(x, mv, ng, wrec, brec, cw, cb, wg, bg, lam, wgate, bgate)


FF_CHUNK = 1024


def _mlp_kernel(*refs, pre, final):
    refs = list(refs)
    x_ref = refs.pop(0)
    mv_ref = refs.pop(0)
    ng_ref = refs.pop(0)
    w1_ref = refs.pop(0)
    w2_ref = refs.pop(0)
    if pre == "hyena":
        v_ref = refs.pop(0)
    if pre == "lru":
        pq_ref = refs.pop(0)
        hin_ref = refs.pop(0)
    if pre:
        wout_ref = refs.pop(0)
        bout_ref = refs.pop(0)
    if final:
        fg_ref = refs.pop(0)
    o_ref = refs.pop(0)
    mv = mv_ref[0]
    if pre == "lru":
        groups = NSEG
        x = _to_tile_order(x_ref[0], groups)
        rows, w = pq_ref.shape[-2:]

        def times_entering(plane, e):
            running = pq_ref[0, 0, plane].astype(F32).reshape(rows // NSEG, NSEG, w)
            return (running * hin_ref[0, e][None]).reshape(rows, w)

        mixed = pq_ref[0, 0, 0].astype(F32) + times_entering(1, 0) + times_entering(2, 1)
        x = x + mv[2:3] * (_bdot(mixed.astype(BF16), wout_ref[...]) + bout_ref[...])
    else:
        groups = x_ref.shape[2]
        x = x_ref[0].reshape(GRID_W * groups, x_ref.shape[3])
    if pre == "hyena":
        v = v_ref[0].reshape(x.shape)
        x = x + mv[2:3] * (_bdot(v, wout_ref[...]) + bout_ref[...])
    u = _modulate(x, ng_ref[...], mv[3:4], mv[4:5]).astype(BF16)
    acc = jnp.zeros(x.shape, F32)
    for c in range(w1_ref.shape[1] // FF_CHUNK):
        cs = slice(c * FF_CHUNK, (c + 1) * FF_CHUNK)
        h = jnp.maximum(_bdot(u, w1_ref[:, cs]), 0.0)
        acc = acc + _bdot((h * h).astype(BF16), w2_ref[cs, :])
    out = x + mv[5:6] * acc
    if final:
        o_ref[0] = _from_tile_order(_rms_norm(out, fg_ref[...]), groups)
    else:
        o_ref[0] = out.reshape(o_ref.shape[1:])


def _mlp_call(x, mv, ng, w1, w2, *, groups, hyena=None, lru=None, final_g=None):
    f = w1.shape[1]
    if lru is not None:
        b, s, d = x.shape
        rows = s // GRID_W
        x_spec = pl.BlockSpec((1, GRID_W * groups, d), lambda i, j: (i, j, 0))
    else:
        b, _, rows, d = x.shape
    tile_spec = pl.BlockSpec((1, GRID_W, groups, d), lambda i, j: (i, 0, j, 0))
    args = [x, mv, ng, w1, w2]
    in_specs = [
        x_spec if lru is not None else tile_spec,
        pl.BlockSpec((1, SUBLANES, d), lambda i, j: (i, 0, 0)),
        _const_spec((1, d)),
        _const_spec((d, f)),
        _const_spec((f, d)),
    ]
    pre = None
    if hyena is not None:
        pre = "hyena"
        v, wout, bout = hyena
        args += [v, wout, bout]
        in_specs += [tile_spec, _const_spec(wout.shape), _const_spec((1, d))]
    if lru is not None:
        pre = "lru"
        pq, hin, wout, bout = lru
        args += [pq, hin, wout, bout]
        in_specs += [
            pl.BlockSpec((1, 1) + pq.shape[2:], lambda i, j: (i, j, 0, 0, 0)),
            pl.BlockSpec((1, 2, NSEG, hin.shape[-1]), lambda i, j: (i, 0, j, 0)),
            _const_spec(wout.shape),
            _const_spec((1, d)),
        ]
    if final_g is not None:
        args.append(final_g)
        in_specs.append(_const_spec((1, d)))
        out_spec = pl.BlockSpec((1, GRID_W * groups, d), lambda i, j: (i, j, 0))
        out_shape = jax.ShapeDtypeStruct((b, GRID_W * rows, d), F32)
    else:
        out_spec = tile_spec
        out_shape = jax.ShapeDtypeStruct((b, GRID_W, rows, d), F32)
    return pl.pallas_call(
        functools.partial(_mlp_kernel, pre=pre, final=final_g is not None),
        grid=(b, rows // groups),
        in_specs=in_specs,
        out_specs=out_spec,
        out_shape=out_shape,
        compiler_params=_cparams(("parallel", "parallel")),
    )(*args)


def _hyproj_kernel(x_ref, mv_ref, ng_ref, win_ref, bin_ref, cw_ref, cb_ref, v_ref, xa_ref, xb_ref):
    mv = mv_ref[0]
    groups = x_ref.shape[2]
    d = x_ref.shape[3]
    x = x_ref[0].reshape(GRID_W * groups, d)
    u = _modulate(x, ng_ref[...], mv[0:1], mv[1:2]).astype(BF16)
    for k, o_ref in enumerate((v_ref, xa_ref, xb_ref)):
        cs = slice(k * d, (k + 1) * d)
        z = _bdot(u, win_ref[:, cs]) + bin_ref[:, cs]
        z = _row_conv(z, cw_ref[:, cs], cb_ref[:, cs], HYENA_CONV_LEFT, groups)
        o_ref[0] = z.astype(BF16).reshape(o_ref.shape[1:])


def _hyproj_call(x, mv, ng, win, bin_, cw, cb, *, groups):
    b, _, rows, d = x.shape
    tile_spec = pl.BlockSpec((1, GRID_W, groups, d), lambda i, j: (i, 0, j, 0))
    out_sds = jax.ShapeDtypeStruct(x.shape, BF16)
    return pl.pallas_call(
        _hyproj_kernel,
        grid=(b, rows // groups),
        in_specs=[
            tile_spec,
            pl.BlockSpec((1, SUBLANES, d), lambda i, j: (i, 0, 0)),
            _const_spec((1, d)),
            _const_spec(win.shape),
            _const_spec(bin_.shape),
            _const_spec(cw.shape),
            _const_spec(cb.shape),
        ],
        out_specs=[tile_spec, tile_spec, tile_spec],
        out_shape=[out_sds, out_sds, out_sds],
        compiler_params=_cparams(("parallel", "parallel")),
    )(x, mv, ng, win, bin_, cw, cb)


def _filter_kernel(pos_ref, fw1_ref, fb1_ref, fw2_ref, fb2_ref, fw3_ref, fb3_ref, fw4_ref, freq_ref,
                   deltas_ref, h_ref, nrm_ref):
    groups = pos_ref.shape[1]
    pos = pos_ref[...].reshape(GRID_W * groups, pos_ref.shape[2])
    freq = freq_ref[...]

    def hdot(a, b):
        return jnp.dot(a, b, preferred_element_type=F32, precision=HIGHEST)

    h = jnp.sin(freq * (hdot(pos, fw1_ref[...]) + fb1_ref[...]))
    h = jnp.sin(freq * (hdot(h, fw2_ref[...]) + fb2_ref[...]))
    h = jnp.sin(freq * (hdot(h, fw3_ref[...]) + fb3_ref[...]))
    d = deltas_ref.shape[1]
    decay = jnp.exp(-pos[:, 0:1] * deltas_ref[...])
    nparts = fw4_ref.shape[1] // d
    sums = []
    for p in range(nparts):
        cs = slice(p * d, (p + 1) * d)
        hp = _bdot(h.astype(BF16), fw4_ref[:, cs]) * decay
        sums.append(jnp.sum(jnp.abs(hp), axis=0, keepdims=True))
        h_ref[:, :, cs] = hp.astype(BF16).reshape(GRID_W, groups, d)
    half = nparts // 2
    tot = jnp.concatenate([sums[p] + sums[p + half] for p in range(half)], axis=1)

    @pl.when(pl.program_id(0) == 0)
    def _():
        nrm_ref[...] = jnp.zeros_like(nrm_ref)

    nrm_ref[...] += tot


def _filter_call(pos, fw1, fb1, fw2, fb2, fw3, fb3, fw4, freq, deltas, *, groups):
    _, rows, pe = pos.shape
    fh = fw2.shape[0]
    n4 = fw4.shape[1]
    d = deltas.shape[1]
    return pl.pallas_call(
        _filter_kernel,
        grid=(rows // groups,),
        in_specs=[
            pl.BlockSpec((GRID_W, groups, pe), lambda j: (0, j, 0)),
            _const_spec((pe, fh)), _const_spec((1, fh)),
            _const_spec((fh, fh)), _const_spec((1, fh)),
            _const_spec((fh, fh)), _const_spec((1, fh)),
            _const_spec((fh, n4)), _const_spec((1, fh)),
            _const_spec((1, d)),
        ],
        out_specs=[
            pl.BlockSpec((GRID_W, groups, n4), lambda j: (0, j, 0)),
            pl.BlockSpec((1, n4 // 2), lambda j: (0, 0)),
        ],
        out_shape=[
            jax.ShapeDtypeStruct((GRID_W, rows, n4), BF16),
            jax.ShapeDtypeStruct((1, n4 // 2), F32),
        ],
        compiler_params=_cparams(("arbitrary",)),
    )(pos, fw1, fb1, fw2, fb2, fw3, fb3, fw4, freq, deltas)


@functools.lru_cache(maxsize=None)
def _dft_constants(seq_len):
    n = 2 * seq_len
    n2 = DFT_N2
    n1 = n // n2
    nt1 = n1 // 2
    nf = n1 // 2 + 1
    slots = _round_up(-(-nf // DFT_BLOCKS), SUBLANES)
    nfp = DFT_BLOCKS * slots
    t1 = np.arange(nt1)[None, :]
    f1 = np.arange(nfp)[:, None]
    live = (f1 < nf).astype(np.float64)
    ang1 = 2.0 * np.pi * (t1 * f1 % n1) / n1
    cos1, sin1 = np.cos(ang1) * live, np.sin(ang1) * live
    cf = np.full((nfp, 1), 2.0)
    cf[0] = 1.0
    cf[nf - 1] = 1.0
    f1h = np.zeros((DFT_BLOCKS, 2 * slots, nt1))
    f1i = np.zeros((DFT_BLOCKS, nt1, 2 * slots))
    for kb in range(DFT_BLOCKS):
        blk = slice(kb * slots, (kb + 1) * slots)
        f1h[kb, :slots] = cos1[blk]
        f1h[kb, slots:] = -sin1[blk]
        f1i[kb, :, :slots] = (cos1[blk] * cf[blk]).T
        f1i[kb, :, slots:] = (-sin1[blk] * cf[blk]).T
    t2 = np.arange(n2)[None, None, :]
    f2 = np.arange(n2)[None, :, None]
    ff1 = np.arange(nf)[:, None, None]
    ang2 = 2.0 * np.pi * ((t2 * (ff1 + n1 * f2)) % n) / n
    gr, gim = np.cos(ang2), -np.sin(ang2)
    g = np.concatenate([np.concatenate([gr, -gim], axis=2),
                        np.concatenate([gim, gr], axis=2)], axis=1)
    as32 = lambda a: np.ascontiguousarray(a, dtype=np.float32)
    return as32(f1h), as32(g), as32(f1i), nt1, nf, slots


def _spec_rows(slot0, t2, slots):
    return pl.ds(slot0 * SPEC_PITCH + t2, slots, stride=SPEC_PITCH)


def _dft_stage1(load_slab, f1h_ref, kb, spec_scr, slot0, slots):
    def body(t2, carry):
        a = _bdot(f1h_ref[kb], load_slab(t2))
        for lt in range(2):
            ls = slice(lt * LANES, (lt + 1) * LANES)
            spec_scr[lt, _spec_rows(slot0, t2, slots), :] = a[:slots, ls]
            spec_scr[lt, _spec_rows(slot0, t2 + DFT_N2, slots), :] = a[slots:, ls]
        return carry

    lax.fori_loop(0, DFT_N2, body, 0, unroll=SLAB_UNROLL)


def _spec_slot_load(spec_scr, slot):
    base = pl.multiple_of(slot * SPEC_PITCH, SUBLANES)
    rows = pl.ds(base, 2 * DFT_N2)
    return jnp.concatenate([spec_scr[0, rows, :], spec_scr[1, rows, :]], axis=1), rows


def _filtfft_kernel(hf_ref, hb_ref, nrm_ref, f1h_ref, g_ref, k_ref, spec_scr, *, nf, slots):
    inv = 1.0 / nrm_ref[...]

    def slab(t2):
        return jnp.concatenate([hf_ref[t2], hb_ref[t2]], axis=1)

    for kb in range(DFT_BLOCKS):
        _dft_stage1(slab, f1h_ref, kb, spec_scr, kb * slots, slots)

    def body(f1, carry):
        a, _ = _spec_slot_load(spec_scr, f1)
        xs = _bdot(g_ref[f1], a.astype(BF16))
        fwd, bwd = xs[:, :LANES], xs[:, LANES:]
        k_ref[0, f1] = jnp.concatenate(
            [fwd[:DFT_N2] + bwd[:DFT_N2], fwd[DFT_N2:] - bwd[DFT_N2:]], axis=0) * inv
        return carry

    lax.fori_loop(0, nf, body, 0, unroll=FREQ_UNROLL)


def _filtfft_call(hraw, nrm, seq_len):
    f1h, g, _, nt1, nf, slots = _dft_constants(seq_len)
    _, rows, n4 = hraw.shape
    d = n4 // 4
    nct = d // LANES
    return pl.pallas_call(
        functools.partial(_filtfft_kernel, nf=nf, slots=slots),
        grid=(2, nct),
        in_specs=[
            pl.BlockSpec((GRID_W, rows, LANES), lambda o, c: (0, 0, o * nct + c)),
            pl.BlockSpec((GRID_W, rows, LANES), lambda o, c: (0, 0, 2 * nct + o * nct + c)),
            pl.BlockSpec((1, LANES), lambda o, c: (0, o * nct + c)),
            _const_spec(f1h.shape),
            _const_spec(g.shape),
        ],
        out_specs=pl.BlockSpec((1, nf, 2 * DFT_N2, LANES), lambda o, c: (o, 0, 0, c)),
        out_shape=jax.ShapeDtypeStruct((2, nf, 2 * DFT_N2, d), F32),
        scratch_shapes=[pltpu.VMEM((2, DFT_BLOCKS * slots * SPEC_PITCH, LANES), F32)],
        compiler_params=_cparams(("parallel", "parallel")),
    )(hraw, hraw, nrm, jnp.asarray(f1h).astype(BF16), jnp.asarray(g).astype(BF16))


def _longconv_kernel(v_ref, m_ref, k_ref, skip_ref, f1h_ref, g_ref, f1i_ref, o_ref,
                     spec_scr, acc_scr, *, nf, slots):
    def slab(t2):
        return jnp.concatenate([v_ref[0, t2], v_ref[1, t2]], axis=1)

    skip = skip_ref[...]
    for kb in range(DFT_BLOCKS):
        _dft_stage1(slab, f1h_ref, kb, spec_scr, 0, slots)
        f1_lo = kb * slots

        def mid(s, carry):
            a, rows = _spec_slot_load(spec_scr, s)
            xs = _bdot(g_ref[f1_lo + s], a.astype(BF16))
            kf = k_ref[0, f1_lo + s]
            kr = jnp.concatenate([kf[:DFT_N2]] * 2, axis=1)
            ki = jnp.concatenate([kf[DFT_N2:]] * 2, axis=1)
            xr, xi = xs[:DFT_N2], xs[DFT_N2:]
            ys = jnp.concatenate([xr * kr - xi * ki, xr * ki + xi * kr], axis=0).astype(BF16)
            bs = lax.dot_general(g_ref[f1_lo + s], ys, (((0,), (0,)), ((), ())),
                                 preferred_element_type=F32)
            spec_scr[0, rows, :] = bs[:, :LANES]
            spec_scr[1, rows, :] = bs[:, LANES:]
            return carry

        lax.fori_loop(0, min(slots, nf - f1_lo), mid, 0, unroll=FREQ_UNROLL)

        def last(t2, carry):
            halves = []
            for lt in range(2):
                re = spec_scr[lt, _spec_rows(0, t2, slots), :]
                im = spec_scr[lt, _spec_rows(0, t2 + DFT_N2, slots), :]
                halves.append(jnp.concatenate([re, im], axis=0))
            y = _bdot(f1i_ref[kb], jnp.concatenate(halves, axis=1).astype(BF16))
            if kb == 0:
                acc_scr[t2] = y
            elif kb < DFT_BLOCKS - 1:
                acc_scr[t2] = acc_scr[t2] + y
            else:
                tot = acc_scr[t2] + y
                for b in range(2):
                    conv = tot[:, b * LANES:(b + 1) * LANES]
                    vs = v_ref[b, t2].astype(F32)
                    o_ref[b, t2] = (m_ref[b, t2].astype(F32) * (conv + vs * skip)).astype(BF16)
            return carry

        lax.fori_loop(0, DFT_N2, last, 0, unroll=SLAB_UNROLL)


def _longconv_call(v, m, kf, order, skip, seq_len):
    f1h, g, f1i, nt1, nf, slots = _dft_constants(seq_len)
    b, _, rows, d = v.shape
    nct = d // LANES
    seq_block = (2, GRID_W, rows, LANES)
    scale = 1.0 / (2 * seq_len)
    return pl.pallas_call(
        functools.partial(_longconv_kernel, nf=nf, slots=slots),
        grid=(nct, b // 2),
        in_specs=[
            pl.BlockSpec(seq_block, lambda c, i: (i, 0, 0, c)),
            pl.BlockSpec(seq_block, lambda c, i: (i, 0, 0, c), pipeline_mode=pl.Buffered(1)),
            pl.BlockSpec((1, nf, 2 * DFT_N2, LANES), lambda c, i: (order, 0, 0, c),
                         pipeline_mode=pl.Buffered(1)),
            pl.BlockSpec((1, LANES), lambda c, i: (0, c)),
            _const_spec(f1h.shape),
            _const_spec(g.shape),
            _const_spec(f1i.shape),
        ],
        out_specs=pl.BlockSpec(seq_block, lambda c, i: (i, 0, 0, c)),
        out_shape=jax.ShapeDtypeStruct(v.shape, BF16),
        scratch_shapes=[
            pltpu.VMEM((2, slots * SPEC_PITCH, LANES), F32),
            pltpu.VMEM((GRID_W, rows, 2 * LANES), F32),
        ],
        compiler_params=_cparams(("parallel", "parallel")),
    )(v, m, kf, skip, jnp.asarray(f1h).astype(BF16), jnp.asarray(g).astype(BF16),
      jnp.asarray(f1i * scale).astype(BF16))


def _mod_rows(mod_layer, nb, d):
    m = mod_layer.reshape(SUBLANES, 6, d)
    m = jnp.concatenate([m, jnp.zeros((SUBLANES, SUBLANES - 6, d), F32)], axis=1)
    return m[:nb], jnp.broadcast_to(m[nb:nb + 1], (nb, SUBLANES, d))


def _filter_positions(seq_len):
    t = jnp.linspace(0.0, 1.0, seq_len, dtype=F32)[:, None]
    w = (2.0 * math.pi / seq_len) * jnp.arange(seq_len, dtype=F32)[:, None]
    bands = jnp.linspace(1e-4, FILTER_BANDS - 1, FILTER_BANDS, dtype=F32)
    return jnp.concatenate([t, jnp.cos(bands * w), -jnp.sin(bands * w)], axis=-1)


def kernel(x, c, ctx, c_ctx, ada_w, ada_b, norm_g, mlp_w1, mlp_w2, lru_w_in, lru_b_in, lru_conv_w, lru_conv_b, lru_w_a, lru_b_a, lru_w_i, lru_b_i, lru_lambda, lru_w_out, lru_b_out, hy_w_in, hy_b_in, hy_conv_w, hy_conv_b, hy_fw1, hy_fb1, hy_fw2, hy_fb2, hy_fw3, hy_fb3, hy_fw4, hy_freq, hy_skip, hy_w_out, hy_b_out, final_g):
    nb, seq, d = x.shape
    ctx_len = ctx.shape[1]
    w = lru_w_out.shape[1]
    lru_tile = NSEG * GRID_W
    assert nb + 1 <= SUBLANES and nb % 2 == 0
    assert seq % (BF16_ROWS * GRID_W) == 0 and ctx_len % (NSEG * SUBLANES) == 0

    cvec = jnp.concatenate([c, c_ctx[None, :], jnp.zeros((SUBLANES - nb - 1, d), F32)], axis=0)
    mod = _ada_call(cvec, ada_w, ada_b)

    mv_l, mv_c = _mod_rows(mod[0], nb, d)
    ng = norm_g[0, 0][None, :]
    wgate = lru_w_in[0, :, :w].astype(BF16)
    wrec = lru_w_in[0, :, w:].astype(BF16)
    bgate = lru_b_in[0, :w][None, :]
    brec = lru_b_in[0, w:][None, :]
    cw = lru_conv_w[0]
    cb = lru_conv_b[0][None, :]
    wg = (0.5 * jnp.concatenate([lru_w_a[0, 0], lru_w_i[0, 0], lru_w_a[0, 1], lru_w_i[0, 1]], axis=-1)).astype(BF16)
    bg = 0.5 * jnp.concatenate([lru_b_a[0, 0], lru_b_i[0, 0], lru_b_a[0, 1], lru_b_i[0, 1]], axis=-1)[:, None, :]
    lam = lru_lambda[0]
    lru_w = (ng, wrec, brec, cw, cb, wg, bg, lam)
    agg_c = _lru_pass1_call(ctx, mv_c, *lru_w, tile=ctx_len, wrap=True)
    agg_l, pq = _lru_mix_call(x, mv_l, *lru_w, wgate, bgate, tile=lru_tile)
    hin = _segscan_call(agg_l, agg_c)
    x2 = _mlp_call(x, mv_l, norm_g[0, 1][None, :], mlp_w1[0].astype(BF16), mlp_w2[0].astype(BF16), groups=NSEG,
                   lru=(pq, hin, lru_w_out[0].astype(BF16), lru_b_out[0][None, :]))

    mv1, _ = _mod_rows(mod[1], nb, d)
    v, xa, xb = _hyproj_call(x2, mv1, norm_g[1, 0][None, :], hy_w_in[0].astype(BF16), hy_b_in[0][None, :],
                             hy_conv_w[0], hy_conv_b[0][None, :], groups=BF16_ROWS)
    pos = _filter_positions(seq)
    pe = _round_up(pos.shape[1], LANES)
    pos = jnp.pad(pos, ((0, 0), (0, pe - pos.shape[1])))
    pos = pos.reshape(seq // GRID_W, GRID_W, pe).transpose(1, 0, 2)
    fw1 = jnp.pad(hy_fw1[0], ((0, pe - hy_fw1.shape[1]), (0, 0)))
    deltas = jnp.abs(jnp.linspace(math.log(FILTER_TARGET) / SLOW_DECAY_PCT,
                                  math.log(FILTER_TARGET) / FAST_DECAY_PCT, d, dtype=F32))[None, :]
    hraw, nrm = _filter_call(pos, fw1, hy_fb1[0][None, :], hy_fw2[0], hy_fb2[0][None, :], hy_fw3[0],
                             hy_fb3[0][None, :], hy_fw4[0].astype(BF16), hy_freq[0][None, :], deltas,
                             groups=BF16_ROWS)
    kf = _filtfft_call(hraw, nrm, seq)
    v1 = _longconv_call(v, xa, kf, 0, hy_skip[0, 0][None, :], seq)
    v2 = _longconv_call(v1, xb, kf, 1, hy_skip[0, 1][None, :], seq)
    return _mlp_call(x2, mv1, norm_g[1, 1][None, :], mlp_w1[1].astype(BF16), mlp_w2[1].astype(BF16),
                     groups=BF16_ROWS, hyena=(v2, hy_w_out[0].astype(BF16), hy_b_out[0][None, :]),
                     final_g=final_g[None, :])
```

```python
import functools
import math

import numpy as np
import jax
import jax.numpy as jnp
from jax import lax
from jax.experimental import pallas as pl
from jax.experimental.pallas import tpu as pltpu

F32 = jnp.float32
BF16 = jnp.bfloat16
HIGHEST = lax.Precision.HIGHEST

NORM_EPS = 1e-6
GRID_W = 64
LRU_HEADS = 4
LRU_C = 8.0
LRU_CONV_LEFT = 2
HYENA_CONV_LEFT = 1
FILTER_BANDS = 16
FILTER_TARGET = 1e-2
FAST_DECAY_PCT = 0.3
SLOW_DECAY_PCT = 1.5

SUBLANES = 8
LANES = 128
NSEG = SUBLANES
BF16_ROWS = 16
VMEM_LIMIT = 58 * 1024 * 1024

DFT_N2 = GRID_W
SPEC_PAD = 8
SPEC_PITCH = 2 * DFT_N2 + SPEC_PAD
SLAB_UNROLL = 32
FREQ_UNROLL = 43


def _cparams(sem):
    return pltpu.CompilerParams(dimension_semantics=sem, vmem_limit_bytes=VMEM_LIMIT)


def _const_spec(shape):
    nd = len(shape)
    return pl.BlockSpec(shape, lambda *_: (0,) * nd, pipeline_mode=pl.Buffered(1))


def _round_up(a, m):
    return (a + m - 1) // m * m


def _rms_norm(x, g):
    ms = jnp.mean(x * x, axis=-1, keepdims=True)
    return (x * lax.rsqrt(ms + NORM_EPS)) * g


def _modulate(x, g, shift, scale):
    return _rms_norm(x, g) * (1.0 + scale) + shift


def _gelu_tanh(x):
    c = math.sqrt(2.0 / math.pi)
    return x * (0.5 * (1.0 + jnp.tanh(c * (x + 0.044715 * (x * x * x)))))


def _softplus(x):
    return jnp.maximum(x, 0.0) + jnp.log1p(jnp.exp(-jnp.abs(x)))


def _bdot(a, b):
    return jnp.dot(a, b, preferred_element_type=F32)


def _to_tile_order(x, groups):
    n, d = x.shape
    return jnp.swapaxes(x.reshape(groups, n // groups, d), 0, 1).reshape(n, d)


def _from_tile_order(x, groups):
    n, d = x.shape
    return jnp.swapaxes(x.reshape(n // groups, groups, d), 0, 1).reshape(n, d)


def _wrapped_edge(edge, step):
    pieces = []
    for p in range(edge.shape[0] // SUBLANES):
        piece = edge[p * SUBLANES:(p + 1) * SUBLANES]
        sub = lax.broadcasted_iota(jnp.int32, piece.shape, 0)
        if step > 0:
            pieces.append(jnp.where(sub == 0, 0.0, pltpu.roll(piece, 1, 0)))
        else:
            pieces.append(jnp.where(sub == SUBLANES - 1, 0.0, pltpu.roll(piece, SUBLANES - 1, 0)))
    return jnp.concatenate(pieces, axis=0) if len(pieces) > 1 else pieces[0]


def _shift_tokens(z, o, groups, wrap):
    n = abs(o) * groups
    rows = z.shape[0]
    if o < 0:
        edge = _wrapped_edge(z[rows - n:], 1) if wrap else jnp.zeros((n, z.shape[1]), z.dtype)
        return jnp.concatenate([edge, z[:rows - n]], axis=0)
    edge = _wrapped_edge(z[:n], -1) if wrap else jnp.zeros((n, z.shape[1]), z.dtype)
    return jnp.concatenate([z[n:], edge], axis=0)


def _row_conv(z, w, b, left, groups, wrap=False):
    acc = b + w[left:left + 1] * z
    for k in range(w.shape[0]):
        if k != left:
            acc = acc + w[k:k + 1] * _shift_tokens(z, k - left, groups, wrap)
    return acc


def _ada_kernel(c_ref, w_ref, b_ref, o_ref):
    c = c_ref[...]
    cond = c * jax.nn.sigmoid(c)
    o_ref[0] = jnp.dot(cond, w_ref[0], preferred_element_type=F32, precision=HIGHEST) + b_ref[0]


def _ada_call(cvec, ada_w, ada_b):
    depth, d, n = ada_w.shape
    tn = 1536
    return pl.pallas_call(
        _ada_kernel,
        grid=(depth, n // tn),
        in_specs=[
            pl.BlockSpec((SUBLANES, d), lambda i, j: (0, 0)),
            pl.BlockSpec((1, d, tn), lambda i, j: (i, 0, j)),
            pl.BlockSpec((1, 1, tn), lambda i, j: (i, 0, j)),
        ],
        out_specs=pl.BlockSpec((1, SUBLANES, tn), lambda i, j: (i, 0, j)),
        out_shape=jax.ShapeDtypeStruct((depth, SUBLANES, n), F32),
        compiler_params=_cparams(("parallel", "parallel")),
    )(cvec, ada_w, ada_b.reshape(depth, 1, n))


def _lru_head_coeffs(xh, wg_h, bg_h, half_c_sp_h, ab_scr, cs):
    hb = xh.shape[1]
    gates = _bdot(xh.astype(BF16), wg_h) + bg_h
    xh_half = 0.5 * xh
    for e in range(2):
        tr = jnp.tanh(gates[:, (2 * e) * hb:(2 * e + 1) * hb])
        ti = jnp.tanh(gates[:, (2 * e + 1) * hb:(2 * e + 2) * hb])
        c = half_c_sp_h[e:e + 1]
        log_a = c * tr + c
        th = jnp.tanh(log_a)
        q = (-2.0 * th) / (1.0 - th)
        wgt = jnp.where(q > 0.0, q * lax.rsqrt(q), 0.0) * xh_half
        ab_scr[e, 0, :, cs] = jnp.exp(log_a)
        ab_scr[e, 1, :, cs] = wgt * ti + wgt


def _lru_head_scan(ab_scr, agg_ref, cs, seg_len, keep):
    hb = cs.stop - cs.start
    pf = pb = jnp.ones((NSEG, hb), F32)
    hf = hbk = jnp.zeros((NSEG, hb), F32)
    for i in range(seg_len):
        rf = slice(i * NSEG, (i + 1) * NSEG)
        rb = slice((seg_len - 1 - i) * NSEG, (seg_len - i) * NSEG)
        af = ab_scr[0, 0, rf, cs]
        ab = ab_scr[1, 0, rb, cs]
        pf, hf = pf * af, af * hf + ab_scr[0, 1, rf, cs]
        pb, hbk = pb * ab, ab * hbk + ab_scr[1, 1, rb, cs]
        if keep:
            ab_scr[0, 0, rf, cs] = hf
            ab_scr[0, 1, rf, cs] = pf
            ab_scr[1, 0, rb, cs] = hbk
            ab_scr[1, 1, rb, cs] = pb
    agg_ref[0, 0, 0, :, cs] = pf
    agg_ref[0, 0, 1, :, cs] = hf
    agg_ref[0, 1, 0, :, cs] = pb
    agg_ref[0, 1, 1, :, cs] = hbk


def _lru_local_scan(u, wrec_ref, brec_ref, cw_ref, cb_ref, wg_ref, bg_ref, lam_ref, agg_ref, ab_scr,
                    *, wrap, keep, after_head=None):
    seg_len = u.shape[0] // NSEG
    w = ab_scr.shape[-1]
    hb = w // LRU_HEADS
    zr = _bdot(u, wrec_ref[...]) + brec_ref[...]
    xl = _row_conv(zr, cw_ref[...], cb_ref[...], LRU_CONV_LEFT, NSEG, wrap)
    half_c_sp = (-0.5 * LRU_C) * _softplus(-lam_ref[...])
    for h in range(LRU_HEADS):
        cs = slice(h * hb, (h + 1) * hb)
        _lru_head_coeffs(xl[:, cs], wg_ref[h], bg_ref[h], half_c_sp[:, cs], ab_scr, cs)
        _lru_head_scan(ab_scr, agg_ref, cs, seg_len, keep)
        if after_head is not None:
            after_head(cs)


def _lru_pass1_kernel(x_ref, mv_ref, ng_ref, wrec_ref, brec_ref, cw_ref, cb_ref, wg_ref, bg_ref,
                      lam_ref, agg_ref, ab_scr, *, wrap):
    mv = mv_ref[0]
    xp = _to_tile_order(x_ref[0], NSEG)
    u = _modulate(xp, ng_ref[...], mv[0:1], mv[1:2]).astype(BF16)
    _lru_local_scan(u, wrec_ref, brec_ref, cw_ref, cb_ref, wg_ref, bg_ref, lam_ref, agg_ref, ab_scr,
                    wrap=wrap, keep=False)


def _lru_mix_kernel(x_ref, mv_ref, ng_ref, wrec_ref, brec_ref, cw_ref, cb_ref, wg_ref, bg_ref,
                    lam_ref, wgate_ref, bgate_ref, agg_ref, pq_ref, ab_scr):
    mv = mv_ref[0]
    xp = _to_tile_order(x_ref[0], NSEG)
    u = _modulate(xp, ng_ref[...], mv[0:1], mv[1:2]).astype(BF16)

    def emit(cs):
        gate = _gelu_tanh(_bdot(u, wgate_ref[:, cs]) + bgate_ref[:, cs])
        pq_ref[0, 0, 0, :, cs] = ((ab_scr[0, 0, :, cs] + ab_scr[1, 0, :, cs]) * gate).astype(BF16)
        pq_ref[0, 0, 1, :, cs] = (ab_scr[0, 1, :, cs] * gate).astype(BF16)
        pq_ref[0, 0, 2, :, cs] = (ab_scr[1, 1, :, cs] * gate).astype(BF16)

    _lru_local_scan(u, wrec_ref, brec_ref, cw_ref, cb_ref, wg_ref, bg_ref, lam_ref, agg_ref, ab_scr,
                    wrap=False, keep=True, after_head=emit)


def _lru_weight_specs(d, w, cw, wg, bg, lam):
    return [
        _const_spec((1, d)),
        _const_spec((d, w)),
        _const_spec((1, w)),
        _const_spec(cw.shape),
        _const_spec((1, w)),
        _const_spec(wg.shape),
        _const_spec(bg.shape),
        _const_spec(lam.shape),
    ]


def _lru_pass1_call(x, mv, ng, wrec, brec, cw, cb, wg, bg, lam, *, tile, wrap):
    b, s, d = x.shape
    w = wrec.shape[1]
    nt = s // tile
    return pl.pallas_call(
        functools.partial(_lru_pass1_kernel, wrap=wrap),
        grid=(b, nt),
        in_specs=[
            pl.BlockSpec((1, tile, d), lambda i, j: (i, j, 0)),
            pl.BlockSpec((1, SUBLANES, d), lambda i, j: (i, 0, 0)),
        ] + _lru_weight_specs(d, w, cw, wg, bg, lam),
        out_specs=pl.BlockSpec((1, 2, 2, NSEG, w), lambda i, j: (i, 0, 0, j, 0)),
        out_shape=jax.ShapeDtypeStruct((b, 2, 2, nt * NSEG, w), F32),
        scratch_shapes=[pltpu.VMEM((2, 2, tile, w), F32)],
        compiler_params=_cparams(("parallel", "parallel")),
    )(x, mv, ng, wrec, brec, cw, cb, wg, bg, lam)


def _segscan_kernel(aggl_ref, aggc_ref, hin_ref):
    nsl = aggl_ref.shape[3]
    nsc = aggc_ref.shape[3]
    w = aggl_ref.shape[-1]
    for e in range(2):
        order_c = range(nsc) if e == 0 else range(nsc - 1, -1, -1)
        order_l = range(nsl) if e == 0 else range(nsl - 1, -1, -1)
        st = jnp.zeros((1, w), F32)
        for s in order_c:
            st = aggc_ref[0, e, 0, s:s + 1, :] * st + aggc_ref[0, e, 1, s:s + 1, :]
        for s in order_l:
            hin_ref[0, e, s:s + 1, :] = st
            st = aggl_ref[0, e, 0, s:s + 1, :] * st + aggl_ref[0, e, 1, s:s + 1, :]


def _segscan_call(agg_l, agg_c):
    b, _, _, nsl, w = agg_l.shape
    nsc = agg_c.shape[3]
    return pl.pallas_call(
        _segscan_kernel,
        grid=(b,),
        in_specs=[
            pl.BlockSpec((1, 2, 2, nsl, w), lambda i: (i, 0, 0, 0, 0)),
            pl.BlockSpec((1, 2, 2, nsc, w), lambda i: (i, 0, 0, 0, 0)),
        ],
        out_specs=pl.BlockSpec((1, 2, nsl, w), lambda i: (i, 0, 0, 0)),
        out_shape=jax.ShapeDtypeStruct((b, 2, nsl, w), F32),
        compiler_params=_cparams(("parallel",)),
    )(agg_l, agg_c)


def _lru_mix_call(x, mv, ng, wrec, brec, cw, cb, wg, bg, lam, wgate, bgate, *, tile):
    b, s, d = x.shape
    w = wrec.shape[1]
    nt = s // tile
    return pl.pallas_call(
        _lru_mix_kernel,
        grid=(b, nt),
        in_specs=[
            pl.BlockSpec((1, tile, d), lambda i, j: (i, j, 0)),
            pl.BlockSpec((1, SUBLANES, d), lambda i, j: (i, 0, 0)),
        ] + _lru_weight_specs(d, w, cw, wg, bg, lam) + [
            _const_spec((d, w)),
            _const_spec((1, w)),
        ],
        out_specs=[
            pl.BlockSpec((1, 2, 2, NSEG, w), lambda i, j: (i, 0, 0, j, 0)),
            pl.BlockSpec((1, 1, 3, tile, w), lambda i, j: (i, j, 0, 0, 0)),
        ],
        out_shape=[
            jax.ShapeDtypeStruct((b, 2, 2, nt * NSEG, w), F32),
            jax.ShapeDtypeStruct((b, nt, 3, tile, w), BF16),
        ],
        scratch_shapes=[pltpu.VMEM((2, 2, tile, w), F32)],
        compiler_params=_cparams(("parallel", "parallel")),
    )(x, mv, ng, wrec, brec, cw, cb, wg, bg, lam, wgate, bgate)


FF_CHUNK = 1024


def _mlp_kernel(*refs, pre, final):
    refs = list(refs)
    x_ref = refs.pop(0)
    mv_ref = refs.pop(0)
    ng_ref = refs.pop(0)
    w1_ref = refs.pop(0)
    w2_ref = refs.pop(0)
    if pre == "hyena":
        v_ref = refs.pop(0)
    if pre == "lru":
        pq_ref = refs.pop(0)
        hin_ref = refs.pop(0)
    if pre:
        wout_ref = refs.pop(0)
        bout_ref = refs.pop(0)
    if final:
        fg_ref = refs.pop(0)
    o_ref = refs.pop(0)
    mv = mv_ref[0]
    if pre == "lru":
        groups = NSEG
        x = _to_tile_order(x_ref[0], groups)
        rows, w = pq_ref.shape[-2:]

        def times_entering(plane, e):
            running = pq_ref[0, 0, plane].astype(F32).reshape(rows // NSEG, NSEG, w)
            return (running * hin_ref[0, e][None]).reshape(rows, w)

        mixed = pq_ref[0, 0, 0].astype(F32) + times_entering(1, 0) + times_entering(2, 1)
        x = x + mv[2:3] * (_bdot(mixed.astype(BF16), wout_ref[...]) + bout_ref[...])
    else:
        groups = x_ref.shape[2]
        x = x_ref[0].reshape(GRID_W * groups, x_ref.shape[3])
    if pre == "hyena":
        x = x + mv[2:3] * (_bdot(_load_lane_tiles(v_ref, 0), wout_ref[...]) + bout_ref[...])
    u = _modulate(x, ng_ref[...], mv[3:4], mv[4:5]).astype(BF16)
    acc = jnp.zeros(x.shape, F32)
    for c in range(w1_ref.shape[1] // FF_CHUNK):
        cs = slice(c * FF_CHUNK, (c + 1) * FF_CHUNK)
        h = jnp.maximum(_bdot(u, w1_ref[:, cs]), 0.0)
        acc = acc + _bdot((h * h).astype(BF16), w2_ref[cs, :])
    out = x + mv[5:6] * acc
    if final:
        o_ref[0] = _from_tile_order(_rms_norm(out, fg_ref[...]), groups)
    else:
        o_ref[0] = out.reshape(o_ref.shape[1:])


def _mlp_call(x, mv, ng, w1, w2, *, groups, hyena=None, lru=None, final_g=None):
    f = w1.shape[1]
    if lru is not None:
        b, s, d = x.shape
        rows = s // GRID_W
        x_spec = pl.BlockSpec((1, GRID_W * groups, d), lambda i, j: (i, j, 0))
    else:
        b, _, rows, d = x.shape
    tile_spec = pl.BlockSpec((1, GRID_W, groups, d), lambda i, j: (i, 0, j, 0))
    args = [x, mv, ng, w1, w2]
    in_specs = [
        x_spec if lru is not None else tile_spec,
        pl.BlockSpec((1, SUBLANES, d), lambda i, j: (i, 0, 0)),
        _const_spec((1, d)),
        _const_spec((d, f)),
        _const_spec((f, d)),
    ]
    pre = None
    if hyena is not None:
        pre = "hyena"
        v, wout, bout = hyena
        args += [v, wout, bout]
        in_specs += [_lane_tiled_spec(d, groups), _const_spec(wout.shape), _const_spec((1, d))]
    if lru is not None:
        pre = "lru"
        pq, hin, wout, bout = lru
        args += [pq, hin, wout, bout]
        in_specs += [
            pl.BlockSpec((1, 1) + pq.shape[2:], lambda i, j: (i, j, 0, 0, 0)),
            pl.BlockSpec((1, 2, NSEG, hin.shape[-1]), lambda i, j: (i, 0, j, 0)),
            _const_spec(wout.shape),
            _const_spec((1, d)),
        ]
    if final_g is not None:
        args.append(final_g)
        in_specs.append(_const_spec((1, d)))
        out_spec = pl.BlockSpec((1, GRID_W * groups, d), lambda i, j: (i, j, 0))
        out_shape = jax.ShapeDtypeStruct((b, GRID_W * rows, d), F32)
    else:
        out_spec = tile_spec
        out_shape = jax.ShapeDtypeStruct((b, GRID_W, rows, d), F32)
    return pl.pallas_call(
        functools.partial(_mlp_kernel, pre=pre, final=final_g is not None),
        grid=(b, rows // groups),
        in_specs=in_specs,
        out_specs=out_spec,
        out_shape=out_shape,
        compiler_params=_cparams(("parallel", "parallel")),
    )(*args)


def _hyproj_kernel(x_ref, mv_ref, ng_ref, win_ref, bin_ref, cw_ref, cb_ref, v_ref, xa_ref, xb_ref):
    mv = mv_ref[0]
    groups = x_ref.shape[2]
    d = x_ref.shape[3]
    x = x_ref[0].reshape(GRID_W * groups, d)
    u = _modulate(x, ng_ref[...], mv[0:1], mv[1:2]).astype(BF16)
    for k, o_ref in enumerate((v_ref, xa_ref, xb_ref)):
        cs = slice(k * d, (k + 1) * d)
        z = _bdot(u, win_ref[:, cs]) + bin_ref[:, cs]
        z = _row_conv(z, cw_ref[:, cs], cb_ref[:, cs], HYENA_CONV_LEFT, groups).astype(BF16)
        _store_lane_tiles(o_ref, 0, z, groups)


def _store_lane_tiles(o_ref, lead, val, groups):
    for lt in range(val.shape[1] // LANES):
        o_ref[lead, lt] = val[:, lt * LANES:(lt + 1) * LANES].reshape(-1, groups, LANES)


def _load_lane_tiles(ref, lead):
    nlt, t2, groups, _ = ref.shape[1:]
    return jnp.concatenate([ref[lead, lt].reshape(t2 * groups, LANES) for lt in range(nlt)], axis=1)


def _lane_tiled_spec(d, groups):
    return pl.BlockSpec((1, d // LANES, GRID_W, groups, LANES), lambda i, j: (i, 0, 0, j, 0))


def _hyproj_call(x, mv, ng, win, bin_, cw, cb, *, groups):
    b, _, rows, d = x.shape
    tile_spec = pl.BlockSpec((1, GRID_W, groups, d), lambda i, j: (i, 0, j, 0))
    out_spec = _lane_tiled_spec(d, groups)
    out_sds = jax.ShapeDtypeStruct((b, d // LANES, GRID_W, rows, LANES), BF16)
    return pl.pallas_call(
        _hyproj_kernel,
        grid=(b, rows // groups),
        in_specs=[
            tile_spec,
            pl.BlockSpec((1, SUBLANES, d), lambda i, j: (i, 0, 0)),
            _const_spec((1, d)),
            _const_spec(win.shape),
            _const_spec(bin_.shape),
            _const_spec(cw.shape),
            _const_spec(cb.shape),
        ],
        out_specs=[out_spec, out_spec, out_spec],
        out_shape=[out_sds, out_sds, out_sds],
        compiler_params=_cparams(("parallel", "parallel")),
    )(x, mv, ng, win, bin_, cw, cb)


def _filter_kernel(pos_ref, fw1_ref, fb1_ref, fw2_ref, fb2_ref, fw3_ref, fb3_ref, fw4_ref, freq_ref,
                   deltas_ref, h_ref, nrm_ref):
    groups = pos_ref.shape[1]
    pe = pos_ref.shape[2]
    pos = pos_ref[...].reshape(GRID_W * groups, pe)
    half_rows = pos.shape[0] // 2
    half_t2 = GRID_W // 2
    pos2 = jnp.concatenate([pos[:half_rows], pos[half_rows:]], axis=1)
    freq = freq_ref[...]

    def hdot(a, b):
        return jnp.dot(a, b, preferred_element_type=F32, precision=HIGHEST)

    h = jnp.sin(freq * (hdot(pos2, fw1_ref[...]) + fb1_ref[...]))
    h = jnp.sin(freq * (hdot(h, fw2_ref[...]) + fb2_ref[...]))
    h = jnp.sin(freq * (hdot(h, fw3_ref[...]) + fb3_ref[...])).astype(BF16)
    d = deltas_ref.shape[1]
    n4 = fw4_ref.shape[1] // 2
    nparts = n4 // d
    sums = [jnp.zeros((1, d), F32)] * nparts
    for s in range(2):
        decay = jnp.exp(-pos2[:, s * pe:s * pe + 1] * deltas_ref[...])
        t2s = slice(s * half_t2, (s + 1) * half_t2)
        for p in range(nparts):
            cs = slice(p * d, (p + 1) * d)
            hp = _bdot(h, fw4_ref[:, s * n4 + p * d:s * n4 + (p + 1) * d]) * decay
            sums[p] = sums[p] + jnp.sum(jnp.abs(hp), axis=0, keepdims=True)
            hp16 = hp.astype(BF16)
            for lt in range(d // LANES):
                h_ref[p * (d // LANES) + lt, t2s] = (
                    hp16[:, lt * LANES:(lt + 1) * LANES].reshape(half_t2, groups, LANES))
    half = nparts // 2
    tot = jnp.concatenate([sums[p] + sums[p + half] for p in range(half)], axis=1)

    @pl.when(pl.program_id(0) == 0)
    def _():
        nrm_ref[...] = jnp.zeros_like(nrm_ref)

    nrm_ref[...] += tot


def _filter_call(pos, fw1, fb1, fw2, fb2, fw3, fb3, fw4, freq, deltas, *, groups):
    _, rows, pe = pos.shape
    fh = fw2.shape[0]
    n4 = fw4.shape[1] // 2
    d = deltas.shape[1]
    return pl.pallas_call(
        _filter_kernel,
        grid=(rows // groups,),
        in_specs=[
            pl.BlockSpec((GRID_W, groups, pe), lambda j: (0, j, 0)),
            _const_spec(fw1.shape), _const_spec((1, fh)),
            _const_spec((fh, fh)), _const_spec((1, fh)),
            _const_spec((fh, fh)), _const_spec((1, fh)),
            _const_spec(fw4.shape), _const_spec((1, fh)),
            _const_spec((1, d)),
        ],
        out_specs=[
            pl.BlockSpec((n4 // LANES, GRID_W, groups, LANES), lambda j: (0, 0, j, 0)),
            pl.BlockSpec((1, n4 // 2), lambda j: (0, 0)),
        ],
        out_shape=[
            jax.ShapeDtypeStruct((n4 // LANES, GRID_W, rows, LANES), BF16),
            jax.ShapeDtypeStruct((1, n4 // 2), F32),
        ],
        compiler_params=_cparams(("arbitrary",)),
    )(pos, fw1, fb1, fw2, fb2, fw3, fb3, fw4, freq, deltas)


@functools.lru_cache(maxsize=None)
def _dft_constants(seq_len):
    n = 2 * seq_len
    n2 = DFT_N2
    n1 = n // n2
    nt1 = n1 // 2
    nf = n1 // 2 + 1
    slots = _round_up(nf, SUBLANES)
    t1 = np.arange(nt1)[None, :]
    f1 = np.arange(slots)[:, None]
    live = (f1 < nf).astype(np.float64)
    ang1 = 2.0 * np.pi * (t1 * f1 % n1) / n1
    cos1, sin1 = np.cos(ang1) * live, np.sin(ang1) * live
    cf = np.full((slots, 1), 2.0)
    cf[0] = 1.0
    cf[nf - 1] = 1.0
    f1h = np.concatenate([cos1, -sin1], axis=0)
    f1i = np.concatenate([cos1 * cf, -sin1 * cf], axis=0).T
    t2 = np.arange(n2)[None, None, :]
    f2 = np.arange(n2)[None, :, None]
    ff1 = np.arange(nf)[:, None, None]
    ang2 = 2.0 * np.pi * ((t2 * (ff1 + n1 * f2)) % n) / n
    gr, gim = np.cos(ang2), -np.sin(ang2)
    g = np.concatenate([np.concatenate([gr, -gim], axis=2),
                        np.concatenate([gim, gr], axis=2)], axis=1)
    as32 = lambda a: np.ascontiguousarray(a, dtype=np.float32)
    return as32(f1h), as32(g), as32(f1i), nt1, nf, slots


def _spec_rows(t2, slots):
    return pl.ds(t2, slots, stride=SPEC_PITCH)


def _dft_stage1(load_slab, f1h_ref, spec_scr, slots):
    def body(t2, carry):
        a = _bdot(f1h_ref[...], load_slab(t2))
        for lt in range(2):
            ls = slice(lt * LANES, (lt + 1) * LANES)
            spec_scr[lt, _spec_rows(t2, slots), :] = a[:slots, ls]
            spec_scr[lt, _spec_rows(t2 + DFT_N2, slots), :] = a[slots:, ls]
        return carry

    lax.fori_loop(0, DFT_N2, body, 0, unroll=SLAB_UNROLL)


def _spec_slot_load(spec_scr, slot):
    base = pl.multiple_of(slot * SPEC_PITCH, SUBLANES)
    rows = pl.ds(base, 2 * DFT_N2)
    return jnp.concatenate([spec_scr[0, rows, :], spec_scr[1, rows, :]], axis=1), rows


def _filtfft_kernel(hf_ref, hb_ref, nrm_ref, f1h_ref, g_ref, k_ref, spec_scr, *, nf, slots):
    inv = 1.0 / nrm_ref[...]

    def slab(t2):
        return jnp.concatenate([hf_ref[0, t2], hb_ref[0, t2]], axis=1)

    _dft_stage1(slab, f1h_ref, spec_scr, slots)

    def body(f1, carry):
        a, _ = _spec_slot_load(spec_scr, f1)
        xs = _bdot(g_ref[f1], a.astype(BF16))
        fwd, bwd = xs[:, :LANES], xs[:, LANES:]
        k_ref[0, 0, f1] = (jnp.concatenate(
            [fwd[:DFT_N2] + bwd[:DFT_N2], fwd[DFT_N2:] - bwd[DFT_N2:]], axis=0) * inv).astype(BF16)
        return carry

    lax.fori_loop(0, nf, body, 0, unroll=FREQ_UNROLL)


def _filtfft_call(hraw, nrm, seq_len):
    f1h, g, _, nt1, nf, slots = _dft_constants(seq_len)
    nlt, _, rows, _ = hraw.shape
    nct = nlt // 4
    return pl.pallas_call(
        functools.partial(_filtfft_kernel, nf=nf, slots=slots),
        grid=(2, nct),
        in_specs=[
            pl.BlockSpec((1, GRID_W, rows, LANES), lambda o, c: (o * nct + c, 0, 0, 0)),
            pl.BlockSpec((1, GRID_W, rows, LANES), lambda o, c: (2 * nct + o * nct + c, 0, 0, 0)),
            pl.BlockSpec((1, LANES), lambda o, c: (0, o * nct + c)),
            _const_spec(f1h.shape),
            _const_spec(g.shape),
        ],
        out_specs=pl.BlockSpec((1, 1, nf, 2 * DFT_N2, LANES), lambda o, c: (o, c, 0, 0, 0)),
        out_shape=jax.ShapeDtypeStruct((2, nct, nf, 2 * DFT_N2, LANES), BF16),
        scratch_shapes=[pltpu.VMEM((2, slots * SPEC_PITCH, LANES), F32)],
        compiler_params=_cparams(("parallel", "parallel")),
    )(hraw, hraw, nrm, jnp.asarray(f1h).astype(BF16), jnp.asarray(g).astype(BF16))


def _longconv_kernel(v_ref, m_ref, k_ref, skip_ref, f1h_ref, g_ref, f1i_ref, o_ref,
                     spec_scr, *, nf, slots):
    def slab(t2):
        return jnp.concatenate([v_ref[0, 0, t2], v_ref[1, 0, t2]], axis=1)

    _dft_stage1(slab, f1h_ref, spec_scr, slots)

    def mid(f1, carry):
        a, rows = _spec_slot_load(spec_scr, f1)
        xs = _bdot(g_ref[f1], a.astype(BF16))
        kf = k_ref[0, 0, f1].astype(F32)
        kr = jnp.concatenate([kf[:DFT_N2]] * 2, axis=1)
        ki = jnp.concatenate([kf[DFT_N2:]] * 2, axis=1)
        xr, xi = xs[:DFT_N2], xs[DFT_N2:]
        ys = jnp.concatenate([xr * kr - xi * ki, xr * ki + xi * kr], axis=0).astype(BF16)
        bs = lax.dot_general(g_ref[f1], ys, (((0,), (0,)), ((), ())), preferred_element_type=F32)
        spec_scr[0, rows, :] = bs[:, :LANES]
        spec_scr[1, rows, :] = bs[:, LANES:]
        return carry

    lax.fori_loop(0, nf, mid, 0, unroll=FREQ_UNROLL)
    skip = skip_ref[...]

    def last(t2, carry):
        halves = []
        for lt in range(2):
            re = spec_scr[lt, _spec_rows(t2, slots), :]
            im = spec_scr[lt, _spec_rows(t2 + DFT_N2, slots), :]
            halves.append(jnp.concatenate([re, im], axis=0))
        y = _bdot(f1i_ref[...], jnp.concatenate(halves, axis=1).astype(BF16))
        for b in range(2):
            conv = y[:, b * LANES:(b + 1) * LANES]
            vs = v_ref[b, 0, t2].astype(F32)
            o_ref[b, 0, t2] = (m_ref[b, 0, t2].astype(F32) * (conv + vs * skip)).astype(BF16)
        return carry

    lax.fori_loop(0, DFT_N2, last, 0, unroll=SLAB_UNROLL)


def _longconv_call(v, m, kf, order, skip, seq_len):
    f1h, g, f1i, nt1, nf, slots = _dft_constants(seq_len)
    b, nct, _, rows, _ = v.shape
    seq_spec = pl.BlockSpec((2, 1, GRID_W, rows, LANES), lambda c, i: (i, c, 0, 0, 0))
    scale = 1.0 / (2 * seq_len)
    return pl.pallas_call(
        functools.partial(_longconv_kernel, nf=nf, slots=slots),
        grid=(nct, b // 2),
        in_specs=[
            seq_spec,
            seq_spec,
            pl.BlockSpec((1, 1, nf, 2 * DFT_N2, LANES), lambda c, i: (order, c, 0, 0, 0),
                         pipeline_mode=pl.Buffered(1)),
            pl.BlockSpec((1, LANES), lambda c, i: (0, c)),
            _const_spec(f1h.shape),
            _const_spec(g.shape),
            _const_spec(f1i.shape),
        ],
        out_specs=seq_spec,
        out_shape=jax.ShapeDtypeStruct(v.shape, BF16),
        scratch_shapes=[pltpu.VMEM((2, slots * SPEC_PITCH, LANES), F32)],
        compiler_params=_cparams(("parallel", "parallel")),
    )(v, m, kf, skip, jnp.asarray(f1h).astype(BF16), jnp.asarray(g).astype(BF16),
      jnp.asarray(f1i * scale).astype(BF16))


def _mod_rows(mod_layer, nb, d):
    m = mod_layer.reshape(SUBLANES, 6, d)
    m = jnp.concatenate([m, jnp.zeros((SUBLANES, SUBLANES - 6, d), F32)], axis=1)
    return m[:nb], jnp.broadcast_to(m[nb:nb + 1], (nb, SUBLANES, d))


def _filter_positions(seq_len):
    t = jnp.linspace(0.0, 1.0, seq_len, dtype=F32)[:, None]
    w = (2.0 * math.pi / seq_len) * jnp.arange(seq_len, dtype=F32)[:, None]
    bands = jnp.linspace(1e-4, FILTER_BANDS - 1, FILTER_BANDS, dtype=F32)
    return jnp.concatenate([t, jnp.cos(bands * w), -jnp.sin(bands * w)], axis=-1)


def kernel(x, c, ctx, c_ctx, ada_w, ada_b, norm_g, mlp_w1, mlp_w2, lru_w_in, lru_b_in, lru_conv_w, lru_conv_b, lru_w_a, lru_b_a, lru_w_i, lru_b_i, lru_lambda, lru_w_out, lru_b_out, hy_w_in, hy_b_in, hy_conv_w, hy_conv_b, hy_fw1, hy_fb1, hy_fw2, hy_fb2, hy_fw3, hy_fb3, hy_fw4, hy_freq, hy_skip, hy_w_out, hy_b_out, final_g):
    nb, seq, d = x.shape
    ctx_len = ctx.shape[1]
    w = lru_w_out.shape[1]
    lru_tile = NSEG * GRID_W
    assert nb + 1 <= SUBLANES and nb % 2 == 0
    assert seq % (BF16_ROWS * GRID_W) == 0 and ctx_len % (NSEG * SUBLANES) == 0

    cvec = jnp.concatenate([c, c_ctx[None, :], jnp.zeros((SUBLANES - nb - 1, d), F32)], axis=0)
    mod = _ada_call(cvec, ada_w, ada_b)

    mv_l, mv_c = _mod_rows(mod[0], nb, d)
    ng = norm_g[0, 0][None, :]
    wgate = lru_w_in[0, :, :w].astype(BF16)
    wrec = lru_w_in[0, :, w:].astype(BF16)
    bgate = lru_b_in[0, :w][None, :]
    brec = lru_b_in[0, w:][None, :]
    cw = lru_conv_w[0]
    cb = lru_conv_b[0][None, :]
    wg = (0.5 * jnp.concatenate([lru_w_a[0, 0], lru_w_i[0, 0], lru_w_a[0, 1], lru_w_i[0, 1]], axis=-1)).astype(BF16)
    bg = 0.5 * jnp.concatenate([lru_b_a[0, 0], lru_b_i[0, 0], lru_b_a[0, 1], lru_b_i[0, 1]], axis=-1)[:, None, :]
    lam = lru_lambda[0]
    lru_w = (ng, wrec, brec, cw, cb, wg, bg, lam)
    agg_c = _lru_pass1_call(ctx, mv_c, *lru_w, tile=ctx_len, wrap=True)
    agg_l, pq = _lru_mix_call(x, mv_l, *lru_w, wgate, bgate, tile=lru_tile)
    hin = _segscan_call(agg_l, agg_c)
    x2 = _mlp_call(x, mv_l, norm_g[0, 1][None, :], mlp_w1[0].astype(BF16), mlp_w2[0].astype(BF16), groups=NSEG,
                   lru=(pq, hin, lru_w_out[0].astype(BF16), lru_b_out[0][None, :]))

    mv1, _ = _mod_rows(mod[1], nb, d)
    v, xa, xb = _hyproj_call(x2, mv1, norm_g[1, 0][None, :], hy_w_in[0].astype(BF16), hy_b_in[0][None, :],
                             hy_conv_w[0], hy_conv_b[0][None, :], groups=BF16_ROWS)
    pos = _filter_positions(seq)
    pe = _round_up(pos.shape[1], LANES)
    pos = jnp.pad(pos, ((0, 0), (0, pe - pos.shape[1])))
    pos = pos.reshape(seq // GRID_W, GRID_W, pe).transpose(1, 0, 2)
    fw1 = jnp.pad(hy_fw1[0], ((0, pe - hy_fw1.shape[1]), (0, 0)))
    deltas = jnp.abs(jnp.linspace(math.log(FILTER_TARGET) / SLOW_DECAY_PCT,
                                  math.log(FILTER_TARGET) / FAST_DECAY_PCT, d, dtype=F32))[None, :]
    twin = lambda wmat: jnp.kron(jnp.eye(2, dtype=F32), wmat)
    twice = lambda vec: jnp.tile(vec[None, :], (1, 2))
    hraw, nrm = _filter_call(pos, twin(fw1), twice(hy_fb1[0]), twin(hy_fw2[0]), twice(hy_fb2[0]),
                             twin(hy_fw3[0]), twice(hy_fb3[0]), twin(hy_fw4[0]).astype(BF16),
                             twice(hy_freq[0]), deltas, groups=BF16_ROWS)
    kf = _filtfft_call(hraw, nrm, seq)
    v1 = _longconv_call(v, xa, kf, 0, hy_skip[0, 0][None, :], seq)
    v2 = _longconv_call(v1, xb, kf, 1, hy_skip[0, 1][None, :], seq)
    return _mlp_call(x2, mv1, norm_g[1, 1][None, :], mlp_w1[1].astype(BF16), mlp_w2[1].astype(BF16),
                     groups=BF16_ROWS, hyena=(v2, hy_w_out[0].astype(BF16), hy_b_out[0][None, :]),
                     final_g=final_g[None, :])
```

```python
import functools
import math

import numpy as np
import jax
import jax.numpy as jnp
from jax import lax
from jax.experimental import pallas as pl
from jax.experimental.pallas import tpu as pltpu

F32 = jnp.float32
BF16 = jnp.bfloat16
HIGHEST = lax.Precision.HIGHEST

NORM_EPS = 1e-6
GRID_W = 64
LRU_HEADS = 4
LRU_C = 8.0
LRU_CONV_LEFT = 2
HYENA_CONV_LEFT = 1
FILTER_BANDS = 16
FILTER_TARGET = 1e-2
FAST_DECAY_PCT = 0.3
SLOW_DECAY_PCT = 1.5

SUBLANES = 8
LANES = 128
NSEG = SUBLANES
BF16_ROWS = 16
V7X_VMEM_BYTES = 64 * 1024 * 1024
VMEM_LIMIT = V7X_VMEM_BYTES - 6 * 1024 * 1024

DFT_N2 = GRID_W
SPEC_PAD = 8
SPEC_PITCH = 2 * DFT_N2 + SPEC_PAD
SLAB_UNROLL = 32
FREQ_UNROLL = 43


def _cparams(sem):
    return pltpu.CompilerParams(dimension_semantics=sem, vmem_limit_bytes=VMEM_LIMIT)


def _const_spec(shape):
    nd = len(shape)
    return pl.BlockSpec(shape, lambda *_: (0,) * nd, pipeline_mode=pl.Buffered(1))


def _round_up(a, m):
    return (a + m - 1) // m * m


def _rms_norm(x, g):
    ms = jnp.mean(x * x, axis=-1, keepdims=True)
    return (x * lax.rsqrt(ms + NORM_EPS)) * g


def _modulate(x, g, shift, scale):
    return _rms_norm(x, g) * (1.0 + scale) + shift


def _gelu_tanh(x):
    c = math.sqrt(2.0 / math.pi)
    return x * (0.5 * (1.0 + jnp.tanh(c * (x + 0.044715 * (x * x * x)))))


def _softplus(x):
    return jnp.maximum(x, 0.0) + jnp.log1p(jnp.exp(-jnp.abs(x)))


def _bdot(a, b):
    return jnp.dot(a, b, preferred_element_type=F32)


def _to_tile_order(x, groups):
    n, d = x.shape
    return jnp.swapaxes(x.reshape(groups, n // groups, d), 0, 1).reshape(n, d)


def _from_tile_order(x, groups):
    n, d = x.shape
    return jnp.swapaxes(x.reshape(n // groups, groups, d), 0, 1).reshape(n, d)


def _wrapped_edge(edge, step):
    pieces = []
    for p in range(edge.shape[0] // SUBLANES):
        piece = edge[p * SUBLANES:(p + 1) * SUBLANES]
        sub = lax.broadcasted_iota(jnp.int32, piece.shape, 0)
        if step > 0:
            pieces.append(jnp.where(sub == 0, 0.0, pltpu.roll(piece, 1, 0)))
        else:
            pieces.append(jnp.where(sub == SUBLANES - 1, 0.0, pltpu.roll(piece, SUBLANES - 1, 0)))
    return jnp.concatenate(pieces, axis=0) if len(pieces) > 1 else pieces[0]


def _shift_tokens(z, o, groups, wrap):
    n = abs(o) * groups
    rows = z.shape[0]
    if o < 0:
        edge = _wrapped_edge(z[rows - n:], 1) if wrap else jnp.zeros((n, z.shape[1]), z.dtype)
        return jnp.concatenate([edge, z[:rows - n]], axis=0)
    edge = _wrapped_edge(z[:n], -1) if wrap else jnp.zeros((n, z.shape[1]), z.dtype)
    return jnp.concatenate([z[n:], edge], axis=0)


def _row_conv(z, w, b, left, groups, wrap=False):
    acc = b + w[left:left + 1] * z
    for k in range(w.shape[0]):
        if k != left:
            acc = acc + w[k:k + 1] * _shift_tokens(z, k - left, groups, wrap)
    return acc


def _ada_kernel(c_ref, w_ref, b_ref, o_ref):
    c = c_ref[...]
    cond = c * jax.nn.sigmoid(c)
    o_ref[0] = jnp.dot(cond, w_ref[0], preferred_element_type=F32, precision=HIGHEST) + b_ref[0]


def _ada_call(cvec, ada_w, ada_b):
    depth, d, n = ada_w.shape
    tn = 1536
    return pl.pallas_call(
        _ada_kernel,
        grid=(depth, n // tn),
        in_specs=[
            pl.BlockSpec((SUBLANES, d), lambda i, j: (0, 0)),
            pl.BlockSpec((1, d, tn), lambda i, j: (i, 0, j)),
            pl.BlockSpec((1, 1, tn), lambda i, j: (i, 0, j)),
        ],
        out_specs=pl.BlockSpec((1, SUBLANES, tn), lambda i, j: (i, 0, j)),
        out_shape=jax.ShapeDtypeStruct((depth, SUBLANES, n), F32),
        compiler_params=_cparams(("parallel", "parallel")),
    )(cvec, ada_w, ada_b.reshape(depth, 1, n))


def _lru_head_coeffs(xh, wg_h, bg_h, half_c_sp_h, ab_scr, cs):
    hb = xh.shape[1]
    gates = _bdot(xh.astype(BF16), wg_h) + bg_h
    xh_scaled = math.sqrt(0.5) * xh
    for e in range(2):
        tr = jnp.tanh(gates[:, (2 * e) * hb:(2 * e + 1) * hb])
        ti = jnp.tanh(gates[:, (2 * e + 1) * hb:(2 * e + 2) * hb])
        c = half_c_sp_h[e:e + 1]
        log_a = c * tr + c
        th = jnp.tanh(log_a)
        s = th / (th - 1.0)
        wgt = jnp.where(s > 0.0, s * lax.rsqrt(s), 0.0) * xh_scaled
        ab_scr[e, 0, :, cs] = jnp.exp(log_a)
        ab_scr[e, 1, :, cs] = wgt * ti + wgt


def _lru_head_scan(ab_scr, agg_ref, cs, seg_len, keep):
    hb = cs.stop - cs.start
    pf = pb = jnp.ones((NSEG, hb), F32)
    hf = hbk = jnp.zeros((NSEG, hb), F32)
    for i in range(seg_len):
        rf = slice(i * NSEG, (i + 1) * NSEG)
        rb = slice((seg_len - 1 - i) * NSEG, (seg_len - i) * NSEG)
        af = ab_scr[0, 0, rf, cs]
        ab = ab_scr[1, 0, rb, cs]
        pf, hf = pf * af, af * hf + ab_scr[0, 1, rf, cs]
        pb, hbk = pb * ab, ab * hbk + ab_scr[1, 1, rb, cs]
        if keep:
            ab_scr[0, 0, rf, cs] = hf
            ab_scr[0, 1, rf, cs] = pf
            ab_scr[1, 0, rb, cs] = hbk
            ab_scr[1, 1, rb, cs] = pb
    agg_ref[0, 0, 0, :, cs] = pf
    agg_ref[0, 0, 1, :, cs] = hf
    agg_ref[0, 1, 0, :, cs] = pb
    agg_ref[0, 1, 1, :, cs] = hbk


def _lru_local_scan(u, wrec_ref, brec_ref, cw_ref, cb_ref, wg_ref, bg_ref, lam_ref, agg_ref, ab_scr,
                    *, wrap, keep, after_head=None):
    seg_len = u.shape[0] // NSEG
    w = ab_scr.shape[-1]
    hb = w // LRU_HEADS
    zr = _bdot(u, wrec_ref[...]) + brec_ref[...]
    xl = _row_conv(zr, cw_ref[...], cb_ref[...], LRU_CONV_LEFT, NSEG, wrap)
    half_c_sp = (-0.5 * LRU_C) * _softplus(-lam_ref[...])
    for h in range(LRU_HEADS):
        cs = slice(h * hb, (h + 1) * hb)
        _lru_head_coeffs(xl[:, cs], wg_ref[h], bg_ref[h], half_c_sp[:, cs], ab_scr, cs)
        _lru_head_scan(ab_scr, agg_ref, cs, seg_len, keep)
        if after_head is not None:
            after_head(cs)


def _lru_pass1_kernel(x_ref, mv_ref, ng_ref, wrec_ref, brec_ref, cw_ref, cb_ref, wg_ref, bg_ref,
                      lam_ref, agg_ref, ab_scr, *, wrap):
    mv = mv_ref[0]
    xp = _to_tile_order(x_ref[0], NSEG)
    u = _modulate(xp, ng_ref[...], mv[0:1], mv[1:2]).astype(BF16)
    _lru_local_scan(u, wrec_ref, brec_ref, cw_ref, cb_ref, wg_ref, bg_ref, lam_ref, agg_ref, ab_scr,
                    wrap=wrap, keep=False)


def _lru_mix_kernel(x_ref, mv_ref, ng_ref, wrec_ref, brec_ref, cw_ref, cb_ref, wg_ref, bg_ref,
                    lam_ref, wgate_ref, bgate_ref, agg_ref, pq_ref, xt_ref, ab_scr):
    mv = mv_ref[0]
    xp = _to_tile_order(x_ref[0], NSEG)
    xt_ref[0] = xp.reshape(xt_ref.shape[1:])
    u = _modulate(xp, ng_ref[...], mv[0:1], mv[1:2]).astype(BF16)

    def emit(cs):
        gate = _gelu_tanh(_bdot(u, wgate_ref[:, cs]) + bgate_ref[:, cs])
        pq_ref[0, 0, 0, :, cs] = ((ab_scr[0, 0, :, cs] + ab_scr[1, 0, :, cs]) * gate).astype(BF16)
        pq_ref[0, 0, 1, :, cs] = (ab_scr[0, 1, :, cs] * gate).astype(BF16)
        pq_ref[0, 0, 2, :, cs] = (ab_scr[1, 1, :, cs] * gate).astype(BF16)

    _lru_local_scan(u, wrec_ref, brec_ref, cw_ref, cb_ref, wg_ref, bg_ref, lam_ref, agg_ref, ab_scr,
                    wrap=False, keep=True, after_head=emit)


def _lru_weight_specs(d, w, cw, wg, bg, lam):
    return [
        _const_spec((1, d)),
        _const_spec((d, w)),
        _const_spec((1, w)),
        _const_spec(cw.shape),
        _const_spec((1, w)),
        _const_spec(wg.shape),
        _const_spec(bg.shape),
        _const_spec(lam.shape),
    ]


def _lru_pass1_call(x, mv, ng, wrec, brec, cw, cb, wg, bg, lam, *, tile, wrap):
    b, s, d = x.shape
    w = wrec.shape[1]
    nt = s // tile
    return pl.pallas_call(
        functools.partial(_lru_pass1_kernel, wrap=wrap),
        grid=(b, nt),
        in_specs=[
            pl.BlockSpec((1, tile, d), lambda i, j: (i, j, 0)),
            pl.BlockSpec((1, SUBLANES, d), lambda i, j: (i, 0, 0)),
        ] + _lru_weight_specs(d, w, cw, wg, bg, lam),
        out_specs=pl.BlockSpec((1, 2, 2, NSEG, w), lambda i, j: (i, 0, 0, j, 0)),
        out_shape=jax.ShapeDtypeStruct((b, 2, 2, nt * NSEG, w), F32),
        scratch_shapes=[pltpu.VMEM((2, 2, tile, w), F32)],
        compiler_params=_cparams(("parallel", "parallel")),
    )(x, mv, ng, wrec, brec, cw, cb, wg, bg, lam)


def _segscan_kernel(aggl_ref, aggc_ref, hin_ref):
    nsl = aggl_ref.shape[3]
    nsc = aggc_ref.shape[3]
    w = aggl_ref.shape[-1]
    for e in range(2):
        order_c = range(nsc) if e == 0 else range(nsc - 1, -1, -1)
        order_l = range(nsl) if e == 0 else range(nsl - 1, -1, -1)
        st = jnp.zeros((1, w), F32)
        for s in order_c:
            st = aggc_ref[0, e, 0, s:s + 1, :] * st + aggc_ref[0, e, 1, s:s + 1, :]
        for s in order_l:
            hin_ref[0, e, s:s + 1, :] = st
            st = aggl_ref[0, e, 0, s:s + 1, :] * st + aggl_ref[0, e, 1, s:s + 1, :]


def _segscan_call(agg_l, agg_c):
    b, _, _, nsl, w = agg_l.shape
    nsc = agg_c.shape[3]
    return pl.pallas_call(
        _segscan_kernel,
        grid=(b,),
        in_specs=[
            pl.BlockSpec((1, 2, 2, nsl, w), lambda i: (i, 0, 0, 0, 0)),
            pl.BlockSpec((1, 2, 2, nsc, w), lambda i: (i, 0, 0, 0, 0)),
        ],
        out_specs=pl.BlockSpec((1, 2, nsl, w), lambda i: (i, 0, 0, 0)),
        out_shape=jax.ShapeDtypeStruct((b, 2, nsl, w), F32),
        compiler_params=_cparams(("parallel",)),
    )(agg_l, agg_c)


def _lru_mix_call(x, mv, ng, wrec, brec, cw, cb, wg, bg, lam, wgate, bgate, *, tile):
    b, s, d = x.shape
    w = wrec.shape[1]
    nt = s // tile
    return pl.pallas_call(
        _lru_mix_kernel,
        grid=(b, nt),
        in_specs=[
            pl.BlockSpec((1, tile, d), lambda i, j: (i, j, 0)),
            pl.BlockSpec((1, SUBLANES, d), lambda i, j: (i, 0, 0)),
        ] + _lru_weight_specs(d, w, cw, wg, bg, lam) + [
            _const_spec((d, w)),
            _const_spec((1, w)),
        ],
        out_specs=[
            pl.BlockSpec((1, 2, 2, NSEG, w), lambda i, j: (i, 0, 0, j, 0)),
            pl.BlockSpec((1, 1, 3, tile, w), lambda i, j: (i, j, 0, 0, 0)),
            pl.BlockSpec((1, GRID_W, NSEG, d), lambda i, j: (i, 0, j, 0)),
        ],
        out_shape=[
            jax.ShapeDtypeStruct((b, 2, 2, nt * NSEG, w), F32),
            jax.ShapeDtypeStruct((b, nt, 3, tile, w), BF16),
            jax.ShapeDtypeStruct((b, GRID_W, s // GRID_W, d), F32),
        ],
        scratch_shapes=[pltpu.VMEM((2, 2, tile, w), F32)],
        compiler_params=_cparams(("parallel", "parallel")),
    )(x, mv, ng, wrec, brec, cw, cb, wg, bg, lam, wgate, bgate)


FF_CHUNK = 1024


def _mlp_kernel(*refs, pre, final):
    refs = list(refs)
    x_ref = refs.pop(0)
    mv_ref = refs.pop(0)
    ng_ref = refs.pop(0)
    w1_ref = refs.pop(0)
    w2_ref = refs.pop(0)
    if pre == "hyena":
        v_ref = refs.pop(0)
    if pre == "lru":
        pq_ref = refs.pop(0)
        hin_ref = refs.pop(0)
    if pre:
        wout_ref = refs.pop(0)
        bout_ref = refs.pop(0)
    if final:
        fg_ref = refs.pop(0)
    o_ref = refs.pop(0)
    mv = mv_ref[0]
    groups = x_ref.shape[2]
    x = x_ref[0].reshape(GRID_W * groups, x_ref.shape[3])
    if pre == "lru":
        rows, w = pq_ref.shape[-2:]

        def times_entering(plane, e):
            running = pq_ref[0, 0, plane].astype(F32).reshape(rows // NSEG, NSEG, w)
            return (running * hin_ref[0, e][None]).reshape(rows, w)

        mixed = pq_ref[0, 0, 0].astype(F32) + times_entering(1, 0) + times_entering(2, 1)
        x = x + mv[2:3] * (_bdot(mixed.astype(BF16), wout_ref[...]) + bout_ref[...])
    if pre == "hyena":
        x = x + mv[2:3] * (_bdot(_load_lane_tiles(v_ref, 0), wout_ref[...]) + bout_ref[...])
    u = _modulate(x, ng_ref[...], mv[3:4], mv[4:5]).astype(BF16)
    acc = jnp.zeros(x.shape, F32)
    for c in range(w1_ref.shape[1] // FF_CHUNK):
        cs = slice(c * FF_CHUNK, (c + 1) * FF_CHUNK)
        h = jnp.maximum(_bdot(u, w1_ref[:, cs]), 0.0)
        acc = acc + _bdot((h * h).astype(BF16), w2_ref[cs, :])
    out = x + mv[5:6] * acc
    if final:
        o_ref[0] = _from_tile_order(_rms_norm(out, fg_ref[...]), groups)
    else:
        o_ref[0] = out.reshape(o_ref.shape[1:])


def _mlp_call(x, mv, ng, w1, w2, *, groups, hyena=None, lru=None, final_g=None):
    f = w1.shape[1]
    b, _, rows, d = x.shape
    tile_spec = pl.BlockSpec((1, GRID_W, groups, d), lambda i, j: (i, 0, j, 0))
    args = [x, mv, ng, w1, w2]
    in_specs = [
        tile_spec,
        pl.BlockSpec((1, SUBLANES, d), lambda i, j: (i, 0, 0)),
        _const_spec((1, d)),
        _const_spec((d, f)),
        _const_spec((f, d)),
    ]
    pre = None
    if hyena is not None:
        pre = "hyena"
        v, wout, bout = hyena
        args += [v, wout, bout]
        in_specs += [_lane_tiled_spec(d, groups), _const_spec(wout.shape), _const_spec((1, d))]
    if lru is not None:
        pre = "lru"
        pq, hin, wout, bout = lru
        args += [pq, hin, wout, bout]
        in_specs += [
            pl.BlockSpec((1, 1) + pq.shape[2:], lambda i, j: (i, j, 0, 0, 0)),
            pl.BlockSpec((1, 2, NSEG, hin.shape[-1]), lambda i, j: (i, 0, j, 0)),
            _const_spec(wout.shape),
            _const_spec((1, d)),
        ]
    if final_g is not None:
        args.append(final_g)
        in_specs.append(_const_spec((1, d)))
        out_spec = pl.BlockSpec((1, GRID_W * groups, d), lambda i, j: (i, j, 0))
        out_shape = jax.ShapeDtypeStruct((b, GRID_W * rows, d), F32)
    else:
        out_spec = tile_spec
        out_shape = jax.ShapeDtypeStruct((b, GRID_W, rows, d), F32)
    return pl.pallas_call(
        functools.partial(_mlp_kernel, pre=pre, final=final_g is not None),
        grid=(b, rows // groups),
        in_specs=in_specs,
        out_specs=out_spec,
        out_shape=out_shape,
        compiler_params=_cparams(("parallel", "parallel")),
    )(*args)


def _hyproj_kernel(x_ref, mv_ref, ng_ref, win_ref, bin_ref, cw_ref, cb_ref, v_ref, xa_ref, xb_ref):
    mv = mv_ref[0]
    groups = x_ref.shape[2]
    d = x_ref.shape[3]
    x = x_ref[0].reshape(GRID_W * groups, d)
    u = _modulate(x, ng_ref[...], mv[0:1], mv[1:2]).astype(BF16)
    for k, o_ref in enumerate((v_ref, xa_ref, xb_ref)):
        cs = slice(k * d, (k + 1) * d)
        z = _bdot(u, win_ref[:, cs]) + bin_ref[:, cs]
        z = _row_conv(z, cw_ref[:, cs], cb_ref[:, cs], HYENA_CONV_LEFT, groups).astype(BF16)
        _store_lane_tiles(o_ref, 0, z, groups)


def _store_lane_tiles(o_ref, lead, val, groups):
    for lt in range(val.shape[1] // LANES):
        o_ref[lead, lt] = val[:, lt * LANES:(lt + 1) * LANES].reshape(-1, groups, LANES)


def _load_lane_tiles(ref, lead):
    nlt, t2, groups, _ = ref.shape[1:]
    return jnp.concatenate([ref[lead, lt].reshape(t2 * groups, LANES) for lt in range(nlt)], axis=1)


def _lane_tiled_spec(d, groups):
    return pl.BlockSpec((1, d // LANES, GRID_W, groups, LANES), lambda i, j: (i, 0, 0, j, 0))


def _hyproj_call(x, mv, ng, win, bin_, cw, cb, *, groups):
    b, _, rows, d = x.shape
    tile_spec = pl.BlockSpec((1, GRID_W, groups, d), lambda i, j: (i, 0, j, 0))
    out_spec = _lane_tiled_spec(d, groups)
    out_sds = jax.ShapeDtypeStruct((b, d // LANES, GRID_W, rows, LANES), BF16)
    return pl.pallas_call(
        _hyproj_kernel,
        grid=(b, rows // groups),
        in_specs=[
            tile_spec,
            pl.BlockSpec((1, SUBLANES, d), lambda i, j: (i, 0, 0)),
            _const_spec((1, d)),
            _const_spec(win.shape),
            _const_spec(bin_.shape),
            _const_spec(cw.shape),
            _const_spec(cb.shape),
        ],
        out_specs=[out_spec, out_spec, out_spec],
        out_shape=[out_sds, out_sds, out_sds],
        compiler_params=_cparams(("parallel", "parallel")),
    )(x, mv, ng, win, bin_, cw, cb)


def _filter_kernel(pos_ref, fw1_ref, fb1_ref, fw2_ref, fb2_ref, fw3_ref, fb3_ref, fw4_ref, freq_ref,
                   deltas_ref, h_ref, nrm_ref):
    groups = pos_ref.shape[1]
    pe = pos_ref.shape[2]
    pos = pos_ref[...].reshape(GRID_W * groups, pe)
    half_rows = pos.shape[0] // 2
    half_t2 = GRID_W // 2
    pos2 = jnp.concatenate([pos[:half_rows], pos[half_rows:]], axis=1)
    freq = freq_ref[...]

    def hdot(a, b):
        return jnp.dot(a, b, preferred_element_type=F32, precision=HIGHEST)

    h = jnp.sin(freq * (hdot(pos2, fw1_ref[...]) + fb1_ref[...]))
    h = jnp.sin(freq * (hdot(h, fw2_ref[...]) + fb2_ref[...]))
    h = jnp.sin(freq * (hdot(h, fw3_ref[...]) + fb3_ref[...])).astype(BF16)
    d = deltas_ref.shape[1]
    n4 = fw4_ref.shape[1] // 2
    nparts = n4 // d
    sums = [jnp.zeros((1, d), F32)] * nparts
    for s in range(2):
        decay = jnp.exp(-pos2[:, s * pe:s * pe + 1] * deltas_ref[...])
        t2s = slice(s * half_t2, (s + 1) * half_t2)
        for p in range(nparts):
            cs = slice(p * d, (p + 1) * d)
            hp = _bdot(h, fw4_ref[:, s * n4 + p * d:s * n4 + (p + 1) * d]) * decay
            sums[p] = sums[p] + jnp.sum(jnp.abs(hp), axis=0, keepdims=True)
            hp16 = hp.astype(BF16)
            for lt in range(d // LANES):
                h_ref[p * (d // LANES) + lt, t2s] = (
                    hp16[:, lt * LANES:(lt + 1) * LANES].reshape(half_t2, groups, LANES))
    half = nparts // 2
    tot = jnp.concatenate([sums[p] + sums[p + half] for p in range(half)], axis=1)

    @pl.when(pl.program_id(0) == 0)
    def _():
        nrm_ref[...] = jnp.zeros_like(nrm_ref)

    nrm_ref[...] += tot


def _filter_call(pos, fw1, fb1, fw2, fb2, fw3, fb3, fw4, freq, deltas, *, groups):
    _, rows, pe = pos.shape
    fh = fw2.shape[0]
    n4 = fw4.shape[1] // 2
    d = deltas.shape[1]
    return pl.pallas_call(
        _filter_kernel,
        grid=(rows // groups,),
        in_specs=[
            pl.BlockSpec((GRID_W, groups, pe), lambda j: (0, j, 0)),
            _const_spec(fw1.shape), _const_spec((1, fh)),
            _const_spec((fh, fh)), _const_spec((1, fh)),
            _const_spec((fh, fh)), _const_spec((1, fh)),
            _const_spec(fw4.shape), _const_spec((1, fh)),
            _const_spec((1, d)),
        ],
        out_specs=[
            pl.BlockSpec((n4 // LANES, GRID_W, groups, LANES), lambda j: (0, 0, j, 0)),
            pl.BlockSpec((1, n4 // 2), lambda j: (0, 0)),
        ],
        out_shape=[
            jax.ShapeDtypeStruct((n4 // LANES, GRID_W, rows, LANES), BF16),
            jax.ShapeDtypeStruct((1, n4 // 2), F32),
        ],
        compiler_params=_cparams(("arbitrary",)),
    )(pos, fw1, fb1, fw2, fb2, fw3, fb3, fw4, freq, deltas)


@functools.lru_cache(maxsize=None)
def _dft_constants(seq_len):
    n = 2 * seq_len
    n2 = DFT_N2
    n1 = n // n2
    nt1 = n1 // 2
    nf = n1 // 2 + 1
    slots = _round_up(nf, SUBLANES)
    t1 = np.arange(nt1)[None, :]
    f1 = np.arange(slots)[:, None]
    live = (f1 < nf).astype(np.float64)
    ang1 = 2.0 * np.pi * (t1 * f1 % n1) / n1
    cos1, sin1 = np.cos(ang1) * live, np.sin(ang1) * live
    cf = np.full((slots, 1), 2.0)
    cf[0] = 1.0
    cf[nf - 1] = 1.0
    f1h = np.concatenate([cos1, -sin1], axis=0)
    f1i = np.concatenate([cos1 * cf, -sin1 * cf], axis=0).T
    t2 = np.arange(n2)[None, None, :]
    f2 = np.arange(n2)[None, :, None]
    ff1 = np.arange(nf)[:, None, None]
    ang2 = 2.0 * np.pi * ((t2 * (ff1 + n1 * f2)) % n) / n
    gr, gim = np.cos(ang2), -np.sin(ang2)
    g = np.concatenate([np.concatenate([gr, -gim], axis=2),
                        np.concatenate([gim, gr], axis=2)], axis=1)
    as32 = lambda a: np.ascontiguousarray(a, dtype=np.float32)
    return as32(f1h), as32(g), as32(f1i), nt1, nf, slots


def _spec_rows(t2, slots):
    return pl.ds(t2, slots, stride=SPEC_PITCH)


def _dft_stage1(load_slab, f1h_ref, spec_scr, slots):
    def body(t2, carry):
        a = _bdot(f1h_ref[...], load_slab(t2))
        for lt in range(2):
            ls = slice(lt * LANES, (lt + 1) * LANES)
            spec_scr[lt, _spec_rows(t2, slots), :] = a[:slots, ls]
            spec_scr[lt, _spec_rows(t2 + DFT_N2, slots), :] = a[slots:, ls]
        return carry

    lax.fori_loop(0, DFT_N2, body, 0, unroll=SLAB_UNROLL)


def _spec_slot_load(spec_scr, slot):
    base = pl.multiple_of(slot * SPEC_PITCH, SUBLANES)
    rows = pl.ds(base, 2 * DFT_N2)
    return jnp.concatenate([spec_scr[0, rows, :], spec_scr[1, rows, :]], axis=1), rows


def _filtfft_kernel(hf_ref, hb_ref, nrm_ref, f1h_ref, g_ref, k_ref, spec_scr, *, nf, slots):
    inv = 1.0 / nrm_ref[...]

    def slab(t2):
        return jnp.concatenate([hf_ref[0, t2], hb_ref[0, t2]], axis=1)

    _dft_stage1(slab, f1h_ref, spec_scr, slots)

    def body(f1, carry):
        a, _ = _spec_slot_load(spec_scr, f1)
        xs = _bdot(g_ref[f1], a.astype(BF16))
        fwd, bwd = xs[:, :LANES], xs[:, LANES:]
        k_ref[0, 0, f1] = (jnp.concatenate(
            [fwd[:DFT_N2] + bwd[:DFT_N2], fwd[DFT_N2:] - bwd[DFT_N2:]], axis=0) * inv).astype(BF16)
        return carry

    lax.fori_loop(0, nf, body, 0, unroll=FREQ_UNROLL)


def _filtfft_call(hraw, nrm, seq_len):
    f1h, g, _, nt1, nf, slots = _dft_constants(seq_len)
    nlt, _, rows, _ = hraw.shape
    nct = nlt // 4
    return pl.pallas_call(
        functools.partial(_filtfft_kernel, nf=nf, slots=slots),
        grid=(2, nct),
        in_specs=[
            pl.BlockSpec((1, GRID_W, rows, LANES), lambda o, c: (o * nct + c, 0, 0, 0)),
            pl.BlockSpec((1, GRID_W, rows, LANES), lambda o, c: (2 * nct + o * nct + c, 0, 0, 0)),
            pl.BlockSpec((1, LANES), lambda o, c: (0, o * nct + c)),
            _const_spec(f1h.shape),
            _const_spec(g.shape),
        ],
        out_specs=pl.BlockSpec((1, 1, nf, 2 * DFT_N2, LANES), lambda o, c: (o, c, 0, 0, 0)),
        out_shape=jax.ShapeDtypeStruct((2, nct, nf, 2 * DFT_N2, LANES), BF16),
        scratch_shapes=[pltpu.VMEM((2, slots * SPEC_PITCH, LANES), F32)],
        compiler_params=_cparams(("parallel", "parallel")),
    )(hraw, hraw, nrm, jnp.asarray(f1h).astype(BF16), jnp.asarray(g).astype(BF16))


def _longconv_kernel(v_ref, m_ref, k_ref, skip_ref, f1h_ref, g_ref, f1i_ref, o_ref,
                     spec_scr, *, nf, slots):
    def slab(t2):
        return jnp.concatenate([v_ref[0, 0, t2], v_ref[1, 0, t2]], axis=1)

    _dft_stage1(slab, f1h_ref, spec_scr, slots)

    def mid(f1, carry):
        a, rows = _spec_slot_load(spec_scr, f1)
        xs = _bdot(g_ref[f1], a.astype(BF16))
        kf = k_ref[0, 0, f1].astype(F32)
        kr = jnp.concatenate([kf[:DFT_N2]] * 2, axis=1)
        ki = jnp.concatenate([kf[DFT_N2:]] * 2, axis=1)
        xr, xi = xs[:DFT_N2], xs[DFT_N2:]
        ys = jnp.concatenate([xr * kr - xi * ki, xr * ki + xi * kr], axis=0).astype(BF16)
        bs = lax.dot_general(g_ref[f1], ys, (((0,), (0,)), ((), ())), preferred_element_type=F32)
        spec_scr[0, rows, :] = bs[:, :LANES]
        spec_scr[1, rows, :] = bs[:, LANES:]
        return carry

    lax.fori_loop(0, nf, mid, 0, unroll=FREQ_UNROLL)
    skip = skip_ref[...]

    def last(t2, carry):
        halves = []
        for lt in range(2):
            re = spec_scr[lt, _spec_rows(t2, slots), :]
            im = spec_scr[lt, _spec_rows(t2 + DFT_N2, slots), :]
            halves.append(jnp.concatenate([re, im], axis=0))
        y = _bdot(f1i_ref[...], jnp.concatenate(halves, axis=1).astype(BF16))
        for b in range(2):
            conv = y[:, b * LANES:(b + 1) * LANES]
            vs = v_ref[b, 0, t2].astype(F32)
            o_ref[b, 0, t2] = (m_ref[b, 0, t2].astype(F32) * (conv + vs * skip)).astype(BF16)
        return carry

    lax.fori_loop(0, DFT_N2, last, 0, unroll=SLAB_UNROLL)


def _longconv_call(v, m, kf, order, skip, seq_len):
    f1h, g, f1i, nt1, nf, slots = _dft_constants(seq_len)
    b, nct, _, rows, _ = v.shape
    seq_spec = pl.BlockSpec((2, 1, GRID_W, rows, LANES), lambda c, i: (i, c, 0, 0, 0))
    scale = 1.0 / (2 * seq_len)
    return pl.pallas_call(
        functools.partial(_longconv_kernel, nf=nf, slots=slots),
        grid=(nct, b // 2),
        in_specs=[
            seq_spec,
            seq_spec,
            pl.BlockSpec((1, 1, nf, 2 * DFT_N2, LANES), lambda c, i: (order, c, 0, 0, 0),
                         pipeline_mode=pl.Buffered(1)),
            pl.BlockSpec((1, LANES), lambda c, i: (0, c)),
            _const_spec(f1h.shape),
            _const_spec(g.shape),
            _const_spec(f1i.shape),
        ],
        out_specs=seq_spec,
        out_shape=jax.ShapeDtypeStruct(v.shape, BF16),
        scratch_shapes=[pltpu.VMEM((2, slots * SPEC_PITCH, LANES), F32)],
        compiler_params=_cparams(("parallel", "parallel")),
    )(v, m, kf, skip, jnp.asarray(f1h).astype(BF16), jnp.asarray(g).astype(BF16),
      jnp.asarray(f1i * scale).astype(BF16))


def _mod_rows(mod_layer, nb, d):
    m = mod_layer.reshape(SUBLANES, 6, d)
    m = jnp.concatenate([m, jnp.zeros((SUBLANES, SUBLANES - 6, d), F32)], axis=1)
    return m[:nb], jnp.broadcast_to(m[nb:nb + 1], (nb, SUBLANES, d))


def _filter_positions(seq_len):
    t = jnp.linspace(0.0, 1.0, seq_len, dtype=F32)[:, None]
    w = (2.0 * math.pi / seq_len) * jnp.arange(seq_len, dtype=F32)[:, None]
    bands = jnp.linspace(1e-4, FILTER_BANDS - 1, FILTER_BANDS, dtype=F32)
    return jnp.concatenate([t, jnp.cos(bands * w), -jnp.sin(bands * w)], axis=-1)


def kernel(x, c, ctx, c_ctx, ada_w, ada_b, norm_g, mlp_w1, mlp_w2, lru_w_in, lru_b_in, lru_conv_w, lru_conv_b, lru_w_a, lru_b_a, lru_w_i, lru_b_i, lru_lambda, lru_w_out, lru_b_out, hy_w_in, hy_b_in, hy_conv_w, hy_conv_b, hy_fw1, hy_fb1, hy_fw2, hy_fb2, hy_fw3, hy_fb3, hy_fw4, hy_freq, hy_skip, hy_w_out, hy_b_out, final_g):
    nb, seq, d = x.shape
    ctx_len = ctx.shape[1]
    w = lru_w_out.shape[1]
    lru_tile = NSEG * GRID_W
    assert nb + 1 <= SUBLANES and nb % 2 == 0
    assert seq % (BF16_ROWS * GRID_W) == 0 and ctx_len % (NSEG * SUBLANES) == 0

    cvec = jnp.concatenate([c, c_ctx[None, :], jnp.zeros((SUBLANES - nb - 1, d), F32)], axis=0)
    mod = _ada_call(cvec, ada_w, ada_b)

    mv_l, mv_c = _mod_rows(mod[0], nb, d)
    ng = norm_g[0, 0][None, :]
    wgate = lru_w_in[0, :, :w].astype(BF16)
    wrec = lru_w_in[0, :, w:].astype(BF16)
    bgate = lru_b_in[0, :w][None, :]
    brec = lru_b_in[0, w:][None, :]
    cw = lru_conv_w[0]
    cb = lru_conv_b[0][None, :]
    wg = (0.5 * jnp.concatenate([lru_w_a[0, 0], lru_w_i[0, 0], lru_w_a[0, 1], lru_w_i[0, 1]], axis=-1)).astype(BF16)
    bg = 0.5 * jnp.concatenate([lru_b_a[0, 0], lru_b_i[0, 0], lru_b_a[0, 1], lru_b_i[0, 1]], axis=-1)[:, None, :]
    lam = lru_lambda[0]
    lru_w = (ng, wrec, brec, cw, cb, wg, bg, lam)
    agg_c = _lru_pass1_call(ctx, mv_c, *lru_w, tile=ctx_len, wrap=True)
    agg_l, pq, xt = _lru_mix_call(x, mv_l, *lru_w, wgate, bgate, tile=lru_tile)
    hin = _segscan_call(agg_l, agg_c)
    x2 = _mlp_call(xt, mv_l, norm_g[0, 1][None, :], mlp_w1[0].astype(BF16), mlp_w2[0].astype(BF16), groups=NSEG,
                   lru=(pq, hin, lru_w_out[0].astype(BF16), lru_b_out[0][None, :]))

    mv1, _ = _mod_rows(mod[1], nb, d)
    v, xa, xb = _hyproj_call(x2, mv1, norm_g[1, 0][None, :], hy_w_in[0].astype(BF16), hy_b_in[0][None, :],
                             hy_conv_w[0], hy_conv_b[0][None, :], groups=BF16_ROWS)
    pos = _filter_positions(seq)
    pe = _round_up(pos.shape[1], LANES)
    pos = jnp.pad(pos, ((0, 0), (0, pe - pos.shape[1])))
    pos = pos.reshape(seq // GRID_W, GRID_W, pe).transpose(1, 0, 2)
    fw1 = jnp.pad(hy_fw1[0], ((0, pe - hy_fw1.shape[1]), (0, 0)))
    deltas = jnp.abs(jnp.linspace(math.log(FILTER_TARGET) / SLOW_DECAY_PCT,
                                  math.log(FILTER_TARGET) / FAST_DECAY_PCT, d, dtype=F32))[None, :]
    twin = lambda wmat: jnp.kron(jnp.eye(2, dtype=F32), wmat)
    twice = lambda vec: jnp.tile(vec[None, :], (1, 2))
    hraw, nrm = _filter_call(pos, twin(fw1), twice(hy_fb1[0]), twin(hy_fw2[0]), twice(hy_fb2[0]),
                             twin(hy_fw3[0]), twice(hy_fb3[0]), twin(hy_fw4[0]).astype(BF16),
                             twice(hy_freq[0]), deltas, groups=BF16_ROWS)
    kf = _filtfft_call(hraw, nrm, seq)
    v1 = _longconv_call(v, xa, kf, 0, hy_skip[0, 0][None, :], seq)
    v2 = _longconv_call(v1, xb, kf, 1, hy_skip[0, 1][None, :], seq)
    return _mlp_call(x2, mv1, norm_g[1, 1][None, :], mlp_w1[1].astype(BF16), mlp_w2[1].astype(BF16),
                     groups=BF16_ROWS, hyena=(v2, hy_w_out[0].astype(BF16), hy_b_out[0][None, :]),
                     final_g=final_g[None, :])
```

```python
import functools
import math

import numpy as np
import jax
import jax.numpy as jnp
from jax import lax
from jax.experimental import pallas as pl
from jax.experimental.pallas import tpu as pltpu

F32 = jnp.float32
BF16 = jnp.bfloat16
HIGHEST = lax.Precision.HIGHEST

NORM_EPS = 1e-6
GRID_W = 64
LRU_HEADS = 4
LRU_C = 8.0
LRU_CONV_LEFT = 2
HYENA_CONV_LEFT = 1
FILTER_BANDS = 16
FILTER_TARGET = 1e-2
FAST_DECAY_PCT = 0.3
SLOW_DECAY_PCT = 1.5

SUBLANES = 8
LANES = 128
NSEG = SUBLANES
BF16_ROWS = 16
V7X_VMEM_BYTES = 64 * 1024 * 1024
VMEM_LIMIT = V7X_VMEM_BYTES - 6 * 1024 * 1024

DFT_N2 = GRID_W
SPEC_PAD = 8
SPEC_PITCH = 2 * DFT_N2 + SPEC_PAD
SLAB_UNROLL = 32
FREQ_UNROLL = 43


def _cparams(sem):
    return pltpu.CompilerParams(dimension_semantics=sem, vmem_limit_bytes=VMEM_LIMIT)


def _const_spec(shape):
    nd = len(shape)
    return pl.BlockSpec(shape, lambda *_: (0,) * nd, pipeline_mode=pl.Buffered(1))


def _round_up(a, m):
    return (a + m - 1) // m * m


def _rms_norm(x, g):
    ms = jnp.mean(x * x, axis=-1, keepdims=True)
    return (x * lax.rsqrt(ms + NORM_EPS)) * g


def _modulate(x, g, shift, scale):
    return _rms_norm(x, g) * (1.0 + scale) + shift


def _gelu_tanh(x):
    c = math.sqrt(2.0 / math.pi)
    return x * (0.5 * (1.0 + jnp.tanh(c * (x + 0.044715 * (x * x * x)))))


def _softplus(x):
    return jnp.maximum(x, 0.0) + jnp.log1p(jnp.exp(-jnp.abs(x)))


def _bdot(a, b):
    return jnp.dot(a, b, preferred_element_type=F32)


def _to_tile_order(x, groups):
    n, d = x.shape
    return jnp.swapaxes(x.reshape(groups, n // groups, d), 0, 1).reshape(n, d)


def _from_tile_order(x, groups):
    n, d = x.shape
    return jnp.swapaxes(x.reshape(n // groups, groups, d), 0, 1).reshape(n, d)


def _wrapped_edge(edge, step):
    pieces = []
    for p in range(edge.shape[0] // SUBLANES):
        piece = edge[p * SUBLANES:(p + 1) * SUBLANES]
        sub = lax.broadcasted_iota(jnp.int32, piece.shape, 0)
        if step > 0:
            pieces.append(jnp.where(sub == 0, 0.0, pltpu.roll(piece, 1, 0)))
        else:
            pieces.append(jnp.where(sub == SUBLANES - 1, 0.0, pltpu.roll(piece, SUBLANES - 1, 0)))
    return jnp.concatenate(pieces, axis=0) if len(pieces) > 1 else pieces[0]


def _shift_tokens(z, o, groups, wrap):
    n = abs(o) * groups
    rows = z.shape[0]
    if o < 0:
        edge = _wrapped_edge(z[rows - n:], 1) if wrap else jnp.zeros((n, z.shape[1]), z.dtype)
        return jnp.concatenate([edge, z[:rows - n]], axis=0)
    edge = _wrapped_edge(z[:n], -1) if wrap else jnp.zeros((n, z.shape[1]), z.dtype)
    return jnp.concatenate([z[n:], edge], axis=0)


def _row_conv(z, w, b, left, groups, wrap=False):
    acc = b + w[left:left + 1] * z
    for k in range(w.shape[0]):
        if k != left:
            acc = acc + w[k:k + 1] * _shift_tokens(z, k - left, groups, wrap)
    return acc


def _ada_kernel(c_ref, w_ref, b_ref, o_ref):
    c = c_ref[...]
    cond = c * jax.nn.sigmoid(c)
    o_ref[0] = jnp.dot(cond, w_ref[0], preferred_element_type=F32, precision=HIGHEST) + b_ref[0]


def _ada_call(cvec, ada_w, ada_b):
    depth, d, n = ada_w.shape
    tn = 1536
    return pl.pallas_call(
        _ada_kernel,
        grid=(depth, n // tn),
        in_specs=[
            pl.BlockSpec((SUBLANES, d), lambda i, j: (0, 0)),
            pl.BlockSpec((1, d, tn), lambda i, j: (i, 0, j)),
            pl.BlockSpec((1, 1, tn), lambda i, j: (i, 0, j)),
        ],
        out_specs=pl.BlockSpec((1, SUBLANES, tn), lambda i, j: (i, 0, j)),
        out_shape=jax.ShapeDtypeStruct((depth, SUBLANES, n), F32),
        compiler_params=_cparams(("parallel", "parallel")),
    )(cvec, ada_w, ada_b.reshape(depth, 1, n))


def _lru_head_coeffs(xh, wg_h, bg_h, half_c_sp_h, ab_scr, cs):
    hb = xh.shape[1]
    gates = _bdot(xh.astype(BF16), wg_h) + bg_h
    xh_scaled = math.sqrt(0.5) * xh
    for e in range(2):
        tr = jnp.tanh(gates[:, (2 * e) * hb:(2 * e + 1) * hb])
        ti = jnp.tanh(gates[:, (2 * e + 1) * hb:(2 * e + 2) * hb])
        c = half_c_sp_h[e:e + 1]
        log_a = c * tr + c
        th = jnp.tanh(log_a)
        s = th / (th - 1.0)
        wgt = jnp.where(s > 0.0, s * lax.rsqrt(s), 0.0) * xh_scaled
        ab_scr[e, 0, :, cs] = jnp.exp(log_a)
        ab_scr[e, 1, :, cs] = wgt * ti + wgt


def _lru_head_scan(ab_scr, agg_ref, cs, seg_len, keep):
    hb = cs.stop - cs.start
    pf = pb = jnp.ones((NSEG, hb), F32)
    hf = hbk = jnp.zeros((NSEG, hb), F32)
    for i in range(seg_len):
        rf = slice(i * NSEG, (i + 1) * NSEG)
        rb = slice((seg_len - 1 - i) * NSEG, (seg_len - i) * NSEG)
        af = ab_scr[0, 0, rf, cs]
        ab = ab_scr[1, 0, rb, cs]
        pf, hf = pf * af, af * hf + ab_scr[0, 1, rf, cs]
        pb, hbk = pb * ab, ab * hbk + ab_scr[1, 1, rb, cs]
        if keep:
            ab_scr[0, 0, rf, cs] = hf
            ab_scr[0, 1, rf, cs] = pf
            ab_scr[1, 0, rb, cs] = hbk
            ab_scr[1, 1, rb, cs] = pb
    agg_ref[0, 0, 0, :, cs] = pf
    agg_ref[0, 0, 1, :, cs] = hf
    agg_ref[0, 1, 0, :, cs] = pb
    agg_ref[0, 1, 1, :, cs] = hbk


def _lru_local_scan(u, wrec_ref, brec_ref, cw_ref, cb_ref, wg_ref, bg_ref, lam_ref, agg_ref, ab_scr,
                    *, wrap, keep, after_head=None):
    seg_len = u.shape[0] // NSEG
    w = ab_scr.shape[-1]
    hb = w // LRU_HEADS
    zr = _bdot(u, wrec_ref[...]) + brec_ref[...]
    xl = _row_conv(zr, cw_ref[...], cb_ref[...], LRU_CONV_LEFT, NSEG, wrap)
    half_c_sp = (-0.5 * LRU_C) * _softplus(-lam_ref[...])
    for h in range(LRU_HEADS):
        cs = slice(h * hb, (h + 1) * hb)
        _lru_head_coeffs(xl[:, cs], wg_ref[h], bg_ref[h], half_c_sp[:, cs], ab_scr, cs)
        _lru_head_scan(ab_scr, agg_ref, cs, seg_len, keep)
        if after_head is not None:
            after_head(cs)


def _lru_pass1_kernel(x_ref, mv_ref, ng_ref, wrec_ref, brec_ref, cw_ref, cb_ref, wg_ref, bg_ref,
                      lam_ref, agg_ref, ab_scr, *, wrap):
    mv = mv_ref[0]
    xp = _to_tile_order(x_ref[0], NSEG)
    u = _modulate(xp, ng_ref[...], mv[0:1], mv[1:2]).astype(BF16)
    _lru_local_scan(u, wrec_ref, brec_ref, cw_ref, cb_ref, wg_ref, bg_ref, lam_ref, agg_ref, ab_scr,
                    wrap=wrap, keep=False)


def _lru_mix_kernel(x_ref, mv_ref, ng_ref, wrec_ref, brec_ref, cw_ref, cb_ref, wg_ref, bg_ref,
                    lam_ref, wgate_ref, bgate_ref, agg_ref, pq_ref, xt_ref, ab_scr):
    mv = mv_ref[0]
    xp = _to_tile_order(x_ref[0], NSEG)
    xt_ref[0] = xp.reshape(xt_ref.shape[1:])
    u = _modulate(xp, ng_ref[...], mv[0:1], mv[1:2]).astype(BF16)

    def emit(cs):
        gate = _gelu_tanh(_bdot(u, wgate_ref[:, cs]) + bgate_ref[:, cs])
        pq_ref[0, 0, 0, :, cs] = ((ab_scr[0, 0, :, cs] + ab_scr[1, 0, :, cs]) * gate).astype(BF16)
        pq_ref[0, 0, 1, :, cs] = (ab_scr[0, 1, :, cs] * gate).astype(BF16)
        pq_ref[0, 0, 2, :, cs] = (ab_scr[1, 1, :, cs] * gate).astype(BF16)

    _lru_local_scan(u, wrec_ref, brec_ref, cw_ref, cb_ref, wg_ref, bg_ref, lam_ref, agg_ref, ab_scr,
                    wrap=False, keep=True, after_head=emit)


def _lru_weight_specs(d, w, cw, wg, bg, lam):
    return [
        _const_spec((1, d)),
        _const_spec((d, w)),
        _const_spec((1, w)),
        _const_spec(cw.shape),
        _const_spec((1, w)),
        _const_spec(wg.shape),
        _const_spec(bg.shape),
        _const_spec(lam.shape),
    ]


def _lru_pass1_call(x, mv, ng, wrec, brec, cw, cb, wg, bg, lam, *, tile, wrap):
    b, s, d = x.shape
    w = wrec.shape[1]
    nt = s // tile
    return pl.pallas_call(
        functools.partial(_lru_pass1_kernel, wrap=wrap),
        grid=(b, nt),
        in_specs=[
            pl.BlockSpec((1, tile, d), lambda i, j: (i, j, 0)),
            pl.BlockSpec((1, SUBLANES, d), lambda i, j: (i, 0, 0)),
        ] + _lru_weight_specs(d, w, cw, wg, bg, lam),
        out_specs=pl.BlockSpec((1, 2, 2, NSEG, w), lambda i, j: (i, 0, 0, j, 0)),
        out_shape=jax.ShapeDtypeStruct((b, 2, 2, nt * NSEG, w), F32),
        scratch_shapes=[pltpu.VMEM((2, 2, tile, w), F32)],
        compiler_params=_cparams(("parallel", "parallel")),
    )(x, mv, ng, wrec, brec, cw, cb, wg, bg, lam)


def _segscan_kernel(aggl_ref, aggc_ref, hin_ref):
    nsl = aggl_ref.shape[3]
    nsc = aggc_ref.shape[3]
    w = aggl_ref.shape[-1]
    for e in range(2):
        order_c = range(nsc) if e == 0 else range(nsc - 1, -1, -1)
        order_l = range(nsl) if e == 0 else range(nsl - 1, -1, -1)
        st = jnp.zeros((1, w), F32)
        for s in order_c:
            st = aggc_ref[0, e, 0, s:s + 1, :] * st + aggc_ref[0, e, 1, s:s + 1, :]
        for s in order_l:
            hin_ref[0, e, s:s + 1, :] = st
            st = aggl_ref[0, e, 0, s:s + 1, :] * st + aggl_ref[0, e, 1, s:s + 1, :]


def _segscan_call(agg_l, agg_c):
    b, _, _, nsl, w = agg_l.shape
    nsc = agg_c.shape[3]
    return pl.pallas_call(
        _segscan_kernel,
        grid=(b,),
        in_specs=[
            pl.BlockSpec((1, 2, 2, nsl, w), lambda i: (i, 0, 0, 0, 0)),
            pl.BlockSpec((1, 2, 2, nsc, w), lambda i: (i, 0, 0, 0, 0)),
        ],
        out_specs=pl.BlockSpec((1, 2, nsl, w), lambda i: (i, 0, 0, 0)),
        out_shape=jax.ShapeDtypeStruct((b, 2, nsl, w), F32),
        compiler_params=_cparams(("parallel",)),
    )(agg_l, agg_c)


def _lru_mix_call(x, mv, ng, wrec, brec, cw, cb, wg, bg, lam, wgate, bgate, *, tile):
    b, s, d = x.shape
    w = wrec.shape[1]
    nt = s // tile
    return pl.pallas_call(
        _lru_mix_kernel,
        grid=(b, nt),
        in_specs=[
            pl.BlockSpec((1, tile, d), lambda i, j: (i, j, 0)),
            pl.BlockSpec((1, SUBLANES, d), lambda i, j: (i, 0, 0)),
        ] + _lru_weight_specs(d, w, cw, wg, bg, lam) + [
            _const_spec((d, w)),
            _const_spec((1, w)),
        ],
        out_specs=[
            pl.BlockSpec((1, 2, 2, NSEG, w), lambda i, j: (i, 0, 0, j, 0)),
            pl.BlockSpec((1, 1, 3, tile, w), lambda i, j: (i, j, 0, 0, 0)),
            pl.BlockSpec((1, GRID_W, NSEG, d), lambda i, j: (i, 0, j, 0)),
        ],
        out_shape=[
            jax.ShapeDtypeStruct((b, 2, 2, nt * NSEG, w), F32),
            jax.ShapeDtypeStruct((b, nt, 3, tile, w), BF16),
            jax.ShapeDtypeStruct((b, GRID_W, s // GRID_W, d), F32),
        ],
        scratch_shapes=[pltpu.VMEM((2, 2, tile, w), F32)],
        compiler_params=_cparams(("parallel", "parallel")),
    )(x, mv, ng, wrec, brec, cw, cb, wg, bg, lam, wgate, bgate)


FF_CHUNK = 1024


def _mlp_kernel(*refs, pre, final):
    refs = list(refs)
    x_ref = refs.pop(0)
    mv_ref = refs.pop(0)
    ng_ref = refs.pop(0)
    w1_ref = refs.pop(0)
    w2_ref = refs.pop(0)
    if pre == "hyena":
        v_ref = refs.pop(0)
    if pre == "lru":
        pq_ref = refs.pop(0)
        hin_ref = refs.pop(0)
    if pre:
        wout_ref = refs.pop(0)
        bout_ref = refs.pop(0)
    if final:
        fg_ref = refs.pop(0)
    o_ref = refs.pop(0)
    mv = mv_ref[0]
    groups = x_ref.shape[2]
    x = x_ref[0].reshape(GRID_W * groups, x_ref.shape[3])
    if pre == "lru":
        rows, w = pq_ref.shape[-2:]

        def times_entering(plane, e):
            running = pq_ref[0, 0, plane].astype(F32).reshape(rows // NSEG, NSEG, w)
            return (running * hin_ref[0, e][None]).reshape(rows, w)

        mixed = pq_ref[0, 0, 0].astype(F32) + times_entering(1, 0) + times_entering(2, 1)
        x = x + mv[2:3] * (_bdot(mixed.astype(BF16), wout_ref[...]) + bout_ref[...])
    if pre == "hyena":
        x = x + mv[2:3] * (_bdot(_load_lane_tiles(v_ref, 0), wout_ref[...]) + bout_ref[...])
    u = _modulate(x, ng_ref[...], mv[3:4], mv[4:5]).astype(BF16)
    acc = jnp.zeros(x.shape, F32)
    for c in range(w1_ref.shape[1] // FF_CHUNK):
        cs = slice(c * FF_CHUNK, (c + 1) * FF_CHUNK)
        h = jnp.maximum(_bdot(u, w1_ref[:, cs]), 0.0)
        acc = acc + _bdot((h * h).astype(BF16), w2_ref[cs, :])
    out = x + mv[5:6] * acc
    if final:
        o_ref[0] = _from_tile_order(_rms_norm(out, fg_ref[...]), groups)
    else:
        o_ref[0] = out.reshape(o_ref.shape[1:])


def _mlp_call(x, mv, ng, w1, w2, *, groups, hyena=None, lru=None, final_g=None):
    f = w1.shape[1]
    b, _, rows, d = x.shape
    tile_spec = pl.BlockSpec((1, GRID_W, groups, d), lambda i, j: (i, 0, j, 0))
    args = [x, mv, ng, w1, w2]
    in_specs = [
        tile_spec,
        pl.BlockSpec((1, SUBLANES, d), lambda i, j: (i, 0, 0)),
        _const_spec((1, d)),
        _const_spec((d, f)),
        _const_spec((f, d)),
    ]
    pre = None
    if hyena is not None:
        pre = "hyena"
        v, wout, bout = hyena
        args += [v, wout, bout]
        in_specs += [_lane_tiled_spec(d, groups), _const_spec(wout.shape), _const_spec((1, d))]
    if lru is not None:
        pre = "lru"
        pq, hin, wout, bout = lru
        args += [pq, hin, wout, bout]
        in_specs += [
            pl.BlockSpec((1, 1) + pq.shape[2:], lambda i, j: (i, j, 0, 0, 0)),
            pl.BlockSpec((1, 2, NSEG, hin.shape[-1]), lambda i, j: (i, 0, j, 0)),
            _const_spec(wout.shape),
            _const_spec((1, d)),
        ]
    if final_g is not None:
        args.append(final_g)
        in_specs.append(_const_spec((1, d)))
        out_spec = pl.BlockSpec((1, GRID_W * groups, d), lambda i, j: (i, j, 0))
        out_shape = jax.ShapeDtypeStruct((b, GRID_W * rows, d), F32)
    else:
        out_spec = tile_spec
        out_shape = jax.ShapeDtypeStruct((b, GRID_W, rows, d), F32)
    return pl.pallas_call(
        functools.partial(_mlp_kernel, pre=pre, final=final_g is not None),
        grid=(b, rows // groups),
        in_specs=in_specs,
        out_specs=out_spec,
        out_shape=out_shape,
        compiler_params=_cparams(("parallel", "parallel")),
    )(*args)


def _hyproj_kernel(x_ref, mv_ref, ng_ref, win_ref, bin_ref, cw_ref, cb_ref, v_ref, xa_ref, xb_ref):
    mv = mv_ref[0]
    groups = x_ref.shape[2]
    d = x_ref.shape[3]
    x = x_ref[0].reshape(GRID_W * groups, d)
    u = _modulate(x, ng_ref[...], mv[0:1], mv[1:2]).astype(BF16)
    for k, o_ref in enumerate((v_ref, xa_ref, xb_ref)):
        cs = slice(k * d, (k + 1) * d)
        z = _bdot(u, win_ref[:, cs]) + bin_ref[:, cs]
        z = _row_conv(z, cw_ref[:, cs], cb_ref[:, cs], HYENA_CONV_LEFT, groups).astype(BF16)
        _store_lane_tiles(o_ref, 0, z, groups)


def _store_lane_tiles(o_ref, lead, val, groups):
    for lt in range(val.shape[1] // LANES):
        o_ref[lead, lt] = val[:, lt * LANES:(lt + 1) * LANES].reshape(-1, groups, LANES)


def _load_lane_tiles(ref, lead):
    nlt, t2, groups, _ = ref.shape[1:]
    return jnp.concatenate([ref[lead, lt].reshape(t2 * groups, LANES) for lt in range(nlt)], axis=1)


def _lane_tiled_spec(d, groups):
    return pl.BlockSpec((1, d // LANES, GRID_W, groups, LANES), lambda i, j: (i, 0, 0, j, 0))


def _hyproj_call(x, mv, ng, win, bin_, cw, cb, *, groups):
    b, _, rows, d = x.shape
    tile_spec = pl.BlockSpec((1, GRID_W, groups, d), lambda i, j: (i, 0, j, 0))
    out_spec = _lane_tiled_spec(d, groups)
    out_sds = jax.ShapeDtypeStruct((b, d // LANES, GRID_W, rows, LANES), BF16)
    return pl.pallas_call(
        _hyproj_kernel,
        grid=(b, rows // groups),
        in_specs=[
            tile_spec,
            pl.BlockSpec((1, SUBLANES, d), lambda i, j: (i, 0, 0)),
            _const_spec((1, d)),
            _const_spec(win.shape),
            _const_spec(bin_.shape),
            _const_spec(cw.shape),
            _const_spec(cb.shape),
        ],
        out_specs=[out_spec, out_spec, out_spec],
        out_shape=[out_sds, out_sds, out_sds],
        compiler_params=_cparams(("parallel", "parallel")),
    )(x, mv, ng, win, bin_, cw, cb)


def _filter_kernel(pos_ref, fw1_ref, fb1_ref, fw2_ref, fb2_ref, fw3_ref, fb3_ref, fw4_ref, freq_ref,
                   deltas_ref, h_ref, nrm_ref):
    groups = pos_ref.shape[1]
    pe = pos_ref.shape[2]
    pos = pos_ref[...].reshape(GRID_W * groups, pe)
    half_rows = pos.shape[0] // 2
    half_t2 = GRID_W // 2
    pos2 = jnp.concatenate([pos[:half_rows], pos[half_rows:]], axis=1)
    freq = freq_ref[...]

    def hdot(a, b):
        return jnp.dot(a, b, preferred_element_type=F32, precision=HIGHEST)

    h = jnp.sin(freq * (hdot(pos2, fw1_ref[...]) + fb1_ref[...]))
    h = jnp.sin(freq * (hdot(h, fw2_ref[...]) + fb2_ref[...]))
    h = jnp.sin(freq * (hdot(h, fw3_ref[...]) + fb3_ref[...])).astype(BF16)
    d = deltas_ref.shape[1]
    n4 = fw4_ref.shape[1] // 2
    nparts = n4 // d
    sums = [jnp.zeros((1, d), F32)] * nparts
    for s in range(2):
        decay = jnp.exp(-pos2[:, s * pe:s * pe + 1] * deltas_ref[...])
        t2s = slice(s * half_t2, (s + 1) * half_t2)
        for p in range(nparts):
            cs = slice(p * d, (p + 1) * d)
            hp = _bdot(h, fw4_ref[:, s * n4 + p * d:s * n4 + (p + 1) * d]) * decay
            sums[p] = sums[p] + jnp.sum(jnp.abs(hp), axis=0, keepdims=True)
            hp16 = hp.astype(BF16)
            for lt in range(d // LANES):
                h_ref[p * (d // LANES) + lt, t2s] = (
                    hp16[:, lt * LANES:(lt + 1) * LANES].reshape(half_t2, groups, LANES))
    half = nparts // 2
    tot = jnp.concatenate([sums[p] + sums[p + half] for p in range(half)], axis=1)

    @pl.when(pl.program_id(0) == 0)
    def _():
        nrm_ref[...] = jnp.zeros_like(nrm_ref)

    nrm_ref[...] += tot


def _filter_call(pos, fw1, fb1, fw2, fb2, fw3, fb3, fw4, freq, deltas, *, groups):
    _, rows, pe = pos.shape
    fh = fw2.shape[0]
    n4 = fw4.shape[1] // 2
    d = deltas.shape[1]
    return pl.pallas_call(
        _filter_kernel,
        grid=(rows // groups,),
        in_specs=[
            pl.BlockSpec((GRID_W, groups, pe), lambda j: (0, j, 0)),
            _const_spec(fw1.shape), _const_spec((1, fh)),
            _const_spec((fh, fh)), _const_spec((1, fh)),
            _const_spec((fh, fh)), _const_spec((1, fh)),
            _const_spec(fw4.shape), _const_spec((1, fh)),
            _const_spec((1, d)),
        ],
        out_specs=[
            pl.BlockSpec((n4 // LANES, GRID_W, groups, LANES), lambda j: (0, 0, j, 0)),
            pl.BlockSpec((1, n4 // 2), lambda j: (0, 0)),
        ],
        out_shape=[
            jax.ShapeDtypeStruct((n4 // LANES, GRID_W, rows, LANES), BF16),
            jax.ShapeDtypeStruct((1, n4 // 2), F32),
        ],
        compiler_params=_cparams(("arbitrary",)),
    )(pos, fw1, fb1, fw2, fb2, fw3, fb3, fw4, freq, deltas)


@functools.lru_cache(maxsize=None)
def _dft_constants(seq_len):
    n = 2 * seq_len
    n2 = DFT_N2
    n1 = n // n2
    nt1 = n1 // 2
    nf = n1 // 2 + 1
    slots = _round_up(nf, SUBLANES)
    t1 = np.arange(nt1)[None, :]
    f1 = np.arange(slots)[:, None]
    live = (f1 < nf).astype(np.float64)
    ang1 = 2.0 * np.pi * (t1 * f1 % n1) / n1
    cos1, sin1 = np.cos(ang1) * live, np.sin(ang1) * live
    cf = np.full((slots, 1), 2.0)
    cf[0] = 1.0
    cf[nf - 1] = 1.0
    f1h = np.concatenate([cos1, -sin1], axis=0)
    f1i = np.concatenate([cos1 * cf, -sin1 * cf], axis=0).T
    t2 = np.arange(n2)[None, None, :]
    f2 = np.arange(n2)[None, :, None]
    ff1 = np.arange(nf)[:, None, None]
    ang2 = 2.0 * np.pi * ((t2 * (ff1 + n1 * f2)) % n) / n
    gr, gim = np.cos(ang2), -np.sin(ang2)
    g = np.concatenate([np.concatenate([gr, -gim], axis=2),
                        np.concatenate([gim, gr], axis=2)], axis=1)
    as32 = lambda a: np.ascontiguousarray(a, dtype=np.float32)
    return as32(f1h), as32(g), as32(f1i), nt1, nf, slots


def _spec_rows(t2, slots):
    return pl.ds(t2, slots, stride=SPEC_PITCH)


def _dft_stage1(load_slab, f1h_ref, spec_scr, slots):
    def body(t2, carry):
        a = _bdot(f1h_ref[...], load_slab(t2))
        for lt in range(2):
            ls = slice(lt * LANES, (lt + 1) * LANES)
            spec_scr[lt, _spec_rows(t2, slots), :] = a[:slots, ls]
            spec_scr[lt, _spec_rows(t2 + DFT_N2, slots), :] = a[slots:, ls]
        return carry

    lax.fori_loop(0, DFT_N2, body, 0, unroll=SLAB_UNROLL)


def _spec_slot_load(spec_scr, slot):
    base = pl.multiple_of(slot * SPEC_PITCH, SUBLANES)
    rows = pl.ds(base, 2 * DFT_N2)
    return jnp.concatenate([spec_scr[0, rows, :], spec_scr[1, rows, :]], axis=1), rows


def _filtfft_kernel(hf_ref, hb_ref, nrm_ref, f1h_ref, g_ref, k_ref, spec_scr, *, nf, slots):
    inv = 1.0 / nrm_ref[...]

    def slab(t2):
        return jnp.concatenate([hf_ref[0, t2], hb_ref[0, t2]], axis=1)

    _dft_stage1(slab, f1h_ref, spec_scr, slots)

    def body(f1, carry):
        a, _ = _spec_slot_load(spec_scr, f1)
        xs = _bdot(g_ref[f1], a.astype(BF16))
        fwd, bwd = xs[:, :LANES], xs[:, LANES:]
        k_ref[0, 0, f1] = (jnp.concatenate(
            [fwd[:DFT_N2] + bwd[:DFT_N2], fwd[DFT_N2:] - bwd[DFT_N2:]], axis=0) * inv).astype(BF16)
        return carry

    lax.fori_loop(0, nf, body, 0, unroll=FREQ_UNROLL)


def _filtfft_call(hraw, nrm, seq_len):
    f1h, g, _, nt1, nf, slots = _dft_constants(seq_len)
    nlt, _, rows, _ = hraw.shape
    nct = nlt // 4
    return pl.pallas_call(
        functools.partial(_filtfft_kernel, nf=nf, slots=slots),
        grid=(2, nct),
        in_specs=[
            pl.BlockSpec((1, GRID_W, rows, LANES), lambda o, c: (o * nct + c, 0, 0, 0)),
            pl.BlockSpec((1, GRID_W, rows, LANES), lambda o, c: (2 * nct + o * nct + c, 0, 0, 0)),
            pl.BlockSpec((1, LANES), lambda o, c: (0, o * nct + c)),
            _const_spec(f1h.shape),
            _const_spec(g.shape),
        ],
        out_specs=pl.BlockSpec((1, 1, nf, 2 * DFT_N2, LANES), lambda o, c: (o, c, 0, 0, 0)),
        out_shape=jax.ShapeDtypeStruct((2, nct, nf, 2 * DFT_N2, LANES), BF16),
        scratch_shapes=[pltpu.VMEM((2, slots * SPEC_PITCH, LANES), F32)],
        compiler_params=_cparams(("parallel", "parallel")),
    )(hraw, hraw, nrm, jnp.asarray(f1h).astype(BF16), jnp.asarray(g).astype(BF16))


def _longconv_kernel(v_ref, m_ref, k_ref, skip_ref, f1h_ref, g_ref, f1i_ref, o_ref,
                     spec_scr, *, nf, slots):
    def slab(t2):
        return jnp.concatenate([v_ref[0, 0, t2], v_ref[1, 0, t2]], axis=1)

    _dft_stage1(slab, f1h_ref, spec_scr, slots)

    def mid(f1, carry):
        a, rows = _spec_slot_load(spec_scr, f1)
        xs = _bdot(g_ref[f1], a.astype(BF16))
        kf = k_ref[0, 0, f1].astype(F32)
        kr = jnp.concatenate([kf[:DFT_N2]] * 2, axis=1)
        ki = jnp.concatenate([kf[DFT_N2:]] * 2, axis=1)
        xr, xi = xs[:DFT_N2], xs[DFT_N2:]
        ys = jnp.concatenate([xr * kr - xi * ki, xr * ki + xi * kr], axis=0).astype(BF16)
        bs = lax.dot_general(g_ref[f1], ys, (((0,), (0,)), ((), ())), preferred_element_type=F32)
        spec_scr[0, rows, :] = bs[:, :LANES]
        spec_scr[1, rows, :] = bs[:, LANES:]
        return carry

    lax.fori_loop(0, nf, mid, 0, unroll=FREQ_UNROLL)
    skip = skip_ref[...]

    def last(t2, carry):
        halves = []
        for lt in range(2):
            re = spec_scr[lt, _spec_rows(t2, slots), :]
            im = spec_scr[lt, _spec_rows(t2 + DFT_N2, slots), :]
            halves.append(jnp.concatenate([re, im], axis=0))
        y = _bdot(f1i_ref[...], jnp.concatenate(halves, axis=1).astype(BF16))
        for b in range(2):
            conv = y[:, b * LANES:(b + 1) * LANES]
            vs = v_ref[b, 0, t2].astype(F32)
            o_ref[b, 0, t2] = (m_ref[b, 0, t2].astype(F32) * (conv + vs * skip)).astype(BF16)
        return carry

    lax.fori_loop(0, DFT_N2, last, 0, unroll=SLAB_UNROLL)


def _longconv_call(v, m, kf, order, skip, seq_len):
    f1h, g, f1i, nt1, nf, slots = _dft_constants(seq_len)
    b, nct, _, rows, _ = v.shape
    seq_spec = pl.BlockSpec((2, 1, GRID_W, rows, LANES), lambda c, i: (i, c, 0, 0, 0))
    scale = 1.0 / (2 * seq_len)
    return pl.pallas_call(
        functools.partial(_longconv_kernel, nf=nf, slots=slots),
        grid=(nct, b // 2),
        in_specs=[
            seq_spec,
            seq_spec,
            pl.BlockSpec((1, 1, nf, 2 * DFT_N2, LANES), lambda c, i: (order, c, 0, 0, 0)),
            pl.BlockSpec((1, LANES), lambda c, i: (0, c)),
            _const_spec(f1h.shape),
            _const_spec(g.shape),
            _const_spec(f1i.shape),
        ],
        out_specs=seq_spec,
        out_shape=jax.ShapeDtypeStruct(v.shape, BF16),
        scratch_shapes=[pltpu.VMEM((2, slots * SPEC_PITCH, LANES), F32)],
        compiler_params=_cparams(("parallel", "parallel")),
    )(v, m, kf, skip, jnp.asarray(f1h).astype(BF16), jnp.asarray(g).astype(BF16),
      jnp.asarray(f1i * scale).astype(BF16))


def _mod_rows(mod_layer, nb, d):
    m = mod_layer.reshape(SUBLANES, 6, d)
    m = jnp.concatenate([m, jnp.zeros((SUBLANES, SUBLANES - 6, d), F32)], axis=1)
    return m[:nb], jnp.broadcast_to(m[nb:nb + 1], (nb, SUBLANES, d))


def _filter_positions(seq_len):
    t = jnp.linspace(0.0, 1.0, seq_len, dtype=F32)[:, None]
    w = (2.0 * math.pi / seq_len) * jnp.arange(seq_len, dtype=F32)[:, None]
    bands = jnp.linspace(1e-4, FILTER_BANDS - 1, FILTER_BANDS, dtype=F32)
    return jnp.concatenate([t, jnp.cos(bands * w), -jnp.sin(bands * w)], axis=-1)


def kernel(x, c, ctx, c_ctx, ada_w, ada_b, norm_g, mlp_w1, mlp_w2, lru_w_in, lru_b_in, lru_conv_w, lru_conv_b, lru_w_a, lru_b_a, lru_w_i, lru_b_i, lru_lambda, lru_w_out, lru_b_out, hy_w_in, hy_b_in, hy_conv_w, hy_conv_b, hy_fw1, hy_fb1, hy_fw2, hy_fb2, hy_fw3, hy_fb3, hy_fw4, hy_freq, hy_skip, hy_w_out, hy_b_out, final_g):
    nb, seq, d = x.shape
    ctx_len = ctx.shape[1]
    w = lru_w_out.shape[1]
    lru_tile = NSEG * GRID_W
    assert nb + 1 <= SUBLANES and nb % 2 == 0
    assert seq % (BF16_ROWS * GRID_W) == 0 and ctx_len % (NSEG * SUBLANES) == 0

    cvec = jnp.concatenate([c, c_ctx[None, :], jnp.zeros((SUBLANES - nb - 1, d), F32)], axis=0)
    mod = _ada_call(cvec, ada_w, ada_b)

    mv_l, mv_c = _mod_rows(mod[0], nb, d)
    ng = norm_g[0, 0][None, :]
    wgate = lru_w_in[0, :, :w].astype(BF16)
    wrec = lru_w_in[0, :, w:].astype(BF16)
    bgate = lru_b_in[0, :w][None, :]
    brec = lru_b_in[0, w:][None, :]
    cw = lru_conv_w[0]
    cb = lru_conv_b[0][None, :]
    wg = (0.5 * jnp.concatenate([lru_w_a[0, 0], lru_w_i[0, 0], lru_w_a[0, 1], lru_w_i[0, 1]], axis=-1)).astype(BF16)
    bg = 0.5 * jnp.concatenate([lru_b_a[0, 0], lru_b_i[0, 0], lru_b_a[0, 1], lru_b_i[0, 1]], axis=-1)[:, None, :]
    lam = lru_lambda[0]
    lru_w = (ng, wrec, brec, cw, cb, wg, bg, lam)
    agg_c = _lru_pass1_call(ctx, mv_c, *lru_w, tile=ctx_len, wrap=True)
    agg_l, pq, xt = _lru_mix_call(x, mv_l, *lru_w, wgate, bgate, tile=lru_tile)
    hin = _segscan_call(agg_l, agg_c)
    x2 = _mlp_call(xt, mv_l, norm_g[0, 1][None, :], mlp_w1[0].astype(BF16), mlp_w2[0].astype(BF16), groups=NSEG,
                   lru=(pq, hin, lru_w_out[0].astype(BF16), lru_b_out[0][None, :]))

    mv1, _ = _mod_rows(mod[1], nb, d)
    v, xa, xb = _hyproj_call(x2, mv1, norm_g[1, 0][None, :], hy_w_in[0].astype(BF16), hy_b_in[0][None, :],
                             hy_conv_w[0], hy_conv_b[0][None, :], groups=BF16_ROWS)
    pos = _filter_positions(seq)
    pe = _round_up(pos.shape[1], LANES)
    pos = jnp.pad(pos, ((0, 0), (0, pe - pos.shape[1])))
    pos = pos.reshape(seq // GRID_W, GRID_W, pe).transpose(1, 0, 2)
    fw1 = jnp.pad(hy_fw1[0], ((0, pe - hy_fw1.shape[1]), (0, 0)))
    deltas = jnp.abs(jnp.linspace(math.log(FILTER_TARGET) / SLOW_DECAY_PCT,
                                  math.log(FILTER_TARGET) / FAST_DECAY_PCT, d, dtype=F32))[None, :]
    twin = lambda wmat: jnp.kron(jnp.eye(2, dtype=F32), wmat)
    twice = lambda vec: jnp.tile(vec[None, :], (1, 2))
    hraw, nrm = _filter_call(pos, twin(fw1), twice(hy_fb1[0]), twin(hy_fw2[0]), twice(hy_fb2[0]),
                             twin(hy_fw3[0]), twice(hy_fb3[0]), twin(hy_fw4[0]).astype(BF16),
                             twice(hy_freq[0]), deltas, groups=BF16_ROWS)
    kf = _filtfft_call(hraw, nrm, seq)
    v1 = _longconv_call(v, xa, kf, 0, hy_skip[0, 0][None, :], seq)
    v2 = _longconv_call(v1, xb, kf, 1, hy_skip[0, 1][None, :], seq)
    return _mlp_call(x2, mv1, norm_g[1, 1][None, :], mlp_w1[1].astype(BF16), mlp_w2[1].astype(BF16),
                     groups=BF16_ROWS, hyena=(v2, hy_w_out[0].astype(BF16), hy_b_out[0][None, :]),
                     final_g=final_g[None, :])
```

```python
import functools
import math

import numpy as np
import jax
import jax.numpy as jnp
from jax import lax
from jax.experimental import pallas as pl
from jax.experimental.pallas import tpu as pltpu

F32 = jnp.float32
BF16 = jnp.bfloat16
HIGHEST = lax.Precision.HIGHEST

NORM_EPS = 1e-6
GRID_W = 64
LRU_HEADS = 4
LRU_C = 8.0
LRU_CONV_LEFT = 2
HYENA_CONV_LEFT = 1
FILTER_BANDS = 16
FILTER_TARGET = 1e-2
FAST_DECAY_PCT = 0.3
SLOW_DECAY_PCT = 1.5

SUBLANES = 8
LANES = 128
NSEG = SUBLANES
BF16_ROWS = 16
V7X_VMEM_BYTES = 64 * 1024 * 1024
VMEM_LIMIT = V7X_VMEM_BYTES - 6 * 1024 * 1024

DFT_N2 = GRID_W
SPEC_PAD = 8
SPEC_PITCH = 2 * DFT_N2 + SPEC_PAD
SLAB_UNROLL = 32
FREQ_UNROLL = 43


def _cparams(sem):
    return pltpu.CompilerParams(dimension_semantics=sem, vmem_limit_bytes=VMEM_LIMIT)


def _const_spec(shape):
    nd = len(shape)
    return pl.BlockSpec(shape, lambda *_: (0,) * nd, pipeline_mode=pl.Buffered(1))


def _round_up(a, m):
    return (a + m - 1) // m * m


def _rms_norm(x, g):
    ms = jnp.mean(x * x, axis=-1, keepdims=True)
    return (x * lax.rsqrt(ms + NORM_EPS)) * g


def _modulate(x, g, shift, scale):
    return _rms_norm(x, g) * (1.0 + scale) + shift


def _gelu_tanh(x):
    c = math.sqrt(2.0 / math.pi)
    return x * (0.5 * (1.0 + jnp.tanh(c * (x + 0.044715 * (x * x * x)))))


def _softplus(x):
    return jnp.maximum(x, 0.0) + jnp.log1p(jnp.exp(-jnp.abs(x)))


def _bdot(a, b):
    return jnp.dot(a, b, preferred_element_type=F32)


def _to_tile_order(x, groups):
    n, d = x.shape
    return jnp.swapaxes(x.reshape(groups, n // groups, d), 0, 1).reshape(n, d)


def _from_tile_order(x, groups):
    n, d = x.shape
    return jnp.swapaxes(x.reshape(n // groups, groups, d), 0, 1).reshape(n, d)


def _wrapped_edge(edge, step):
    pieces = []
    for p in range(edge.shape[0] // SUBLANES):
        piece = edge[p * SUBLANES:(p + 1) * SUBLANES]
        sub = lax.broadcasted_iota(jnp.int32, piece.shape, 0)
        if step > 0:
            pieces.append(jnp.where(sub == 0, 0.0, pltpu.roll(piece, 1, 0)))
        else:
            pieces.append(jnp.where(sub == SUBLANES - 1, 0.0, pltpu.roll(piece, SUBLANES - 1, 0)))
    return jnp.concatenate(pieces, axis=0) if len(pieces) > 1 else pieces[0]


def _shift_tokens(z, o, groups, wrap):
    n = abs(o) * groups
    rows = z.shape[0]
    if o < 0:
        edge = _wrapped_edge(z[rows - n:], 1) if wrap else jnp.zeros((n, z.shape[1]), z.dtype)
        return jnp.concatenate([edge, z[:rows - n]], axis=0)
    edge = _wrapped_edge(z[:n], -1) if wrap else jnp.zeros((n, z.shape[1]), z.dtype)
    return jnp.concatenate([z[n:], edge], axis=0)


def _row_conv(z, w, b, left, groups, wrap=False):
    acc = b + w[left:left + 1] * z
    for k in range(w.shape[0]):
        if k != left:
            acc = acc + w[k:k + 1] * _shift_tokens(z, k - left, groups, wrap)
    return acc


def _ada_kernel(c_ref, w_ref, b_ref, o_ref):
    c = c_ref[...]
    cond = c * jax.nn.sigmoid(c)
    o_ref[0] = jnp.dot(cond, w_ref[0], preferred_element_type=F32, precision=HIGHEST) + b_ref[0]


def _ada_call(cvec, ada_w, ada_b):
    depth, d, n = ada_w.shape
    tn = 1536
    return pl.pallas_call(
        _ada_kernel,
        grid=(depth, n // tn),
        in_specs=[
            pl.BlockSpec((SUBLANES, d), lambda i, j: (0, 0)),
            pl.BlockSpec((1, d, tn), lambda i, j: (i, 0, j)),
            pl.BlockSpec((1, 1, tn), lambda i, j: (i, 0, j)),
        ],
        out_specs=pl.BlockSpec((1, SUBLANES, tn), lambda i, j: (i, 0, j)),
        out_shape=jax.ShapeDtypeStruct((depth, SUBLANES, n), F32),
        compiler_params=_cparams(("parallel", "parallel")),
    )(cvec, ada_w, ada_b.reshape(depth, 1, n))


def _lru_head_coeffs(xh, wg_h, bg_h, half_c_sp_h, ab_scr, cs):
    hb = xh.shape[1]
    gates = _bdot(xh.astype(BF16), wg_h) + bg_h
    xh_half = 0.5 * xh
    for e in range(2):
        tr = jnp.tanh(gates[:, (2 * e) * hb:(2 * e + 1) * hb])
        ti = jnp.tanh(gates[:, (2 * e + 1) * hb:(2 * e + 2) * hb])
        c = half_c_sp_h[e:e + 1]
        a = jnp.exp(c * tr + c)
        q = 1.0 - a * a
        wgt = jnp.where(q > 0.0, q * lax.rsqrt(q), 0.0) * xh_half
        ab_scr[e, 0, :, cs] = a
        ab_scr[e, 1, :, cs] = wgt * ti + wgt


def _lru_head_scan(ab_scr, agg_ref, cs, seg_len, keep):
    hb = cs.stop - cs.start
    pf = pb = jnp.ones((NSEG, hb), F32)
    hf = hbk = jnp.zeros((NSEG, hb), F32)
    for i in range(seg_len):
        rf = slice(i * NSEG, (i + 1) * NSEG)
        rb = slice((seg_len - 1 - i) * NSEG, (seg_len - i) * NSEG)
        af = ab_scr[0, 0, rf, cs]
        ab = ab_scr[1, 0, rb, cs]
        pf, hf = pf * af, af * hf + ab_scr[0, 1, rf, cs]
        pb, hbk = pb * ab, ab * hbk + ab_scr[1, 1, rb, cs]
        if keep:
            ab_scr[0, 0, rf, cs] = hf
            ab_scr[0, 1, rf, cs] = pf
            ab_scr[1, 0, rb, cs] = hbk
            ab_scr[1, 1, rb, cs] = pb
    agg_ref[0, 0, 0, :, cs] = pf
    agg_ref[0, 0, 1, :, cs] = hf
    agg_ref[0, 1, 0, :, cs] = pb
    agg_ref[0, 1, 1, :, cs] = hbk


def _lru_local_scan(u, wrec_ref, brec_ref, cw_ref, cb_ref, wg_ref, bg_ref, lam_ref, agg_ref, ab_scr,
                    *, wrap, keep, after_head=None):
    seg_len = u.shape[0] // NSEG
    w = ab_scr.shape[-1]
    hb = w // LRU_HEADS
    zr = _bdot(u, wrec_ref[...]) + brec_ref[...]
    xl = _row_conv(zr, cw_ref[...], cb_ref[...], LRU_CONV_LEFT, NSEG, wrap)
    half_c_sp = (-0.5 * LRU_C) * _softplus(-lam_ref[...])
    for h in range(LRU_HEADS):
        cs = slice(h * hb, (h + 1) * hb)
        _lru_head_coeffs(xl[:, cs], wg_ref[h], bg_ref[h], half_c_sp[:, cs], ab_scr, cs)
        _lru_head_scan(ab_scr, agg_ref, cs, seg_len, keep)
        if after_head is not None:
            after_head(cs)


def _lru_pass1_kernel(x_ref, mv_ref, ng_ref, wrec_ref, brec_ref, cw_ref, cb_ref, wg_ref, bg_ref,
                      lam_ref, agg_ref, ab_scr, *, wrap):
    mv = mv_ref[0]
    xp = _to_tile_order(x_ref[0], NSEG)
    u = _modulate(xp, ng_ref[...], mv[0:1], mv[1:2]).astype(BF16)
    _lru_local_scan(u, wrec_ref, brec_ref, cw_ref, cb_ref, wg_ref, bg_ref, lam_ref, agg_ref, ab_scr,
                    wrap=wrap, keep=False)


def _lru_mix_kernel(x_ref, mv_ref, ng_ref, wrec_ref, brec_ref, cw_ref, cb_ref, wg_ref, bg_ref,
                    lam_ref, wgate_ref, bgate_ref, agg_ref, pq_ref, xt_ref, ab_scr):
    mv = mv_ref[0]
    xp = _to_tile_order(x_ref[0], NSEG)
    xt_ref[0] = xp.reshape(xt_ref.shape[1:])
    u = _modulate(xp, ng_ref[...], mv[0:1], mv[1:2]).astype(BF16)

    def emit(cs):
        gate = _gelu_tanh(_bdot(u, wgate_ref[:, cs]) + bgate_ref[:, cs])
        pq_ref[0, 0, 0, :, cs] = ((ab_scr[0, 0, :, cs] + ab_scr[1, 0, :, cs]) * gate).astype(BF16)
        pq_ref[0, 0, 1, :, cs] = (ab_scr[0, 1, :, cs] * gate).astype(BF16)
        pq_ref[0, 0, 2, :, cs] = (ab_scr[1, 1, :, cs] * gate).astype(BF16)

    _lru_local_scan(u, wrec_ref, brec_ref, cw_ref, cb_ref, wg_ref, bg_ref, lam_ref, agg_ref, ab_scr,
                    wrap=False, keep=True, after_head=emit)


def _lru_weight_specs(d, w, cw, wg, bg, lam):
    return [
        _const_spec((1, d)),
        _const_spec((d, w)),
        _const_spec((1, w)),
        _const_spec(cw.shape),
        _const_spec((1, w)),
        _const_spec(wg.shape),
        _const_spec(bg.shape),
        _const_spec(lam.shape),
    ]


def _lru_pass1_call(x, mv, ng, wrec, brec, cw, cb, wg, bg, lam, *, tile, wrap):
    b, s, d = x.shape
    w = wrec.shape[1]
    nt = s // tile
    return pl.pallas_call(
        functools.partial(_lru_pass1_kernel, wrap=wrap),
        grid=(b, nt),
        in_specs=[
            pl.BlockSpec((1, tile, d), lambda i, j: (i, j, 0)),
            pl.BlockSpec((1, SUBLANES, d), lambda i, j: (i, 0, 0)),
        ] + _lru_weight_specs(d, w, cw, wg, bg, lam),
        out_specs=pl.BlockSpec((1, 2, 2, NSEG, w), lambda i, j: (i, 0, 0, j, 0)),
        out_shape=jax.ShapeDtypeStruct((b, 2, 2, nt * NSEG, w), F32),
        scratch_shapes=[pltpu.VMEM((2, 2, tile, w), F32)],
        compiler_params=_cparams(("parallel", "parallel")),
    )(x, mv, ng, wrec, brec, cw, cb, wg, bg, lam)


def _segscan_kernel(aggl_ref, aggc_ref, hin_ref):
    nsl = aggl_ref.shape[3]
    nsc = aggc_ref.shape[3]
    w = aggl_ref.shape[-1]
    for e in range(2):
        order_c = range(nsc) if e == 0 else range(nsc - 1, -1, -1)
        order_l = range(nsl) if e == 0 else range(nsl - 1, -1, -1)
        st = jnp.zeros((1, w), F32)
        for s in order_c:
            st = aggc_ref[0, e, 0, s:s + 1, :] * st + aggc_ref[0, e, 1, s:s + 1, :]
        for s in order_l:
            hin_ref[0, e, s:s + 1, :] = st
            st = aggl_ref[0, e, 0, s:s + 1, :] * st + aggl_ref[0, e, 1, s:s + 1, :]


def _segscan_call(agg_l, agg_c):
    b, _, _, nsl, w = agg_l.shape
    nsc = agg_c.shape[3]
    return pl.pallas_call(
        _segscan_kernel,
        grid=(b,),
        in_specs=[
            pl.BlockSpec((1, 2, 2, nsl, w), lambda i: (i, 0, 0, 0, 0)),
            pl.BlockSpec((1, 2, 2, nsc, w), lambda i: (i, 0, 0, 0, 0)),
        ],
        out_specs=pl.BlockSpec((1, 2, nsl, w), lambda i: (i, 0, 0, 0)),
        out_shape=jax.ShapeDtypeStruct((b, 2, nsl, w), F32),
        compiler_params=_cparams(("parallel",)),
    )(agg_l, agg_c)


def _lru_mix_call(x, mv, ng, wrec, brec, cw, cb, wg, bg, lam, wgate, bgate, *, tile):
    b, s, d = x.shape
    w = wrec.shape[1]
    nt = s // tile
    return pl.pallas_call(
        _lru_mix_kernel,
        grid=(b, nt),
        in_specs=[
            pl.BlockSpec((1, tile, d), lambda i, j: (i, j, 0)),
            pl.BlockSpec((1, SUBLANES, d), lambda i, j: (i, 0, 0)),
        ] + _lru_weight_specs(d, w, cw, wg, bg, lam) + [
            _const_spec((d, w)),
            _const_spec((1, w)),
        ],
        out_specs=[
            pl.BlockSpec((1, 2, 2, NSEG, w), lambda i, j: (i, 0, 0, j, 0)),
            pl.BlockSpec((1, 1, 3, tile, w), lambda i, j: (i, j, 0, 0, 0)),
            pl.BlockSpec((1, GRID_W, NSEG, d), lambda i, j: (i, 0, j, 0)),
        ],
        out_shape=[
            jax.ShapeDtypeStruct((b, 2, 2, nt * NSEG, w), F32),
            jax.ShapeDtypeStruct((b, nt, 3, tile, w), BF16),
            jax.ShapeDtypeStruct((b, GRID_W, s // GRID_W, d), F32),
        ],
        scratch_shapes=[pltpu.VMEM((2, 2, tile, w), F32)],
        compiler_params=_cparams(("parallel", "parallel")),
    )(x, mv, ng, wrec, brec, cw, cb, wg, bg, lam, wgate, bgate)


FF_CHUNK = 1024


def _mlp_kernel(*refs, pre, final):
    refs = list(refs)
    x_ref = refs.pop(0)
    mv_ref = refs.pop(0)
    ng_ref = refs.pop(0)
    w1_ref = refs.pop(0)
    w2_ref = refs.pop(0)
    if pre == "hyena":
        v_ref = refs.pop(0)
    if pre == "lru":
        pq_ref = refs.pop(0)
        hin_ref = refs.pop(0)
    if pre:
        wout_ref = refs.pop(0)
        bout_ref = refs.pop(0)
    if final:
        fg_ref = refs.pop(0)
    o_ref = refs.pop(0)
    mv = mv_ref[0]
    groups = x_ref.shape[2]
    x = x_ref[0].reshape(GRID_W * groups, x_ref.shape[3])
    if pre == "lru":
        rows, w = pq_ref.shape[-2:]

        def times_entering(plane, e):
            running = pq_ref[0, 0, plane].astype(F32).reshape(rows // NSEG, NSEG, w)
            return (running * hin_ref[0, e][None]).reshape(rows, w)

        mixed = pq_ref[0, 0, 0].astype(F32) + times_entering(1, 0) + times_entering(2, 1)
        x = x + mv[2:3] * (_bdot(mixed.astype(BF16), wout_ref[...]) + bout_ref[...])
    if pre == "hyena":
        x = x + mv[2:3] * (_bdot(_load_lane_tiles(v_ref, 0), wout_ref[...]) + bout_ref[...])
    u = _modulate(x, ng_ref[...], mv[3:4], mv[4:5]).astype(BF16)
    acc = jnp.zeros(x.shape, F32)
    for c in range(w1_ref.shape[1] // FF_CHUNK):
        cs = slice(c * FF_CHUNK, (c + 1) * FF_CHUNK)
        h = jnp.maximum(_bdot(u, w1_ref[:, cs]), 0.0)
        acc = acc + _bdot((h * h).astype(BF16), w2_ref[cs, :])
    out = x + mv[5:6] * acc
    if final:
        o_ref[0] = _from_tile_order(_rms_norm(out, fg_ref[...]), groups)
    else:
        o_ref[0] = out.reshape(o_ref.shape[1:])


def _mlp_call(x, mv, ng, w1, w2, *, groups, hyena=None, lru=None, final_g=None):
    f = w1.shape[1]
    b, _, rows, d = x.shape
    tile_spec = pl.BlockSpec((1, GRID_W, groups, d), lambda i, j: (i, 0, j, 0))
    args = [x, mv, ng, w1, w2]
    in_specs = [
        tile_spec,
        pl.BlockSpec((1, SUBLANES, d), lambda i, j: (i, 0, 0)),
        _const_spec((1, d)),
        _const_spec((d, f)),
        _const_spec((f, d)),
    ]
    pre = None
    if hyena is not None:
        pre = "hyena"
        v, wout, bout = hyena
        args += [v, wout, bout]
        in_specs += [_lane_tiled_spec(d, groups), _const_spec(wout.shape), _const_spec((1, d))]
    if lru is not None:
        pre = "lru"
        pq, hin, wout, bout = lru
        args += [pq, hin, wout, bout]
        in_specs += [
            pl.BlockSpec((1, 1) + pq.shape[2:], lambda i, j: (i, j, 0, 0, 0)),
            pl.BlockSpec((1, 2, NSEG, hin.shape[-1]), lambda i, j: (i, 0, j, 0)),
            _const_spec(wout.shape),
            _const_spec((1, d)),
        ]
    if final_g is not None:
        args.append(final_g)
        in_specs.append(_const_spec((1, d)))
        out_spec = pl.BlockSpec((1, GRID_W * groups, d), lambda i, j: (i, j, 0))
        out_shape = jax.ShapeDtypeStruct((b, GRID_W * rows, d), F32)
    else:
        out_spec = tile_spec
        out_shape = jax.ShapeDtypeStruct((b, GRID_W, rows, d), F32)
    return pl.pallas_call(
        functools.partial(_mlp_kernel, pre=pre, final=final_g is not None),
        grid=(b, rows // groups),
        in_specs=in_specs,
        out_specs=out_spec,
        out_shape=out_shape,
        compiler_params=_cparams(("parallel", "parallel")),
    )(*args)


def _hyproj_kernel(x_ref, mv_ref, ng_ref, win_ref, bin_ref, cw_ref, cb_ref, v_ref, xa_ref, xb_ref):
    mv = mv_ref[0]
    groups = x_ref.shape[2]
    d = x_ref.shape[3]
    x = x_ref[0].reshape(GRID_W * groups, d)
    u = _modulate(x, ng_ref[...], mv[0:1], mv[1:2]).astype(BF16)
    for k, o_ref in enumerate((v_ref, xa_ref, xb_ref)):
        cs = slice(k * d, (k + 1) * d)
        z = _bdot(u, win_ref[:, cs]) + bin_ref[:, cs]
        z = _row_conv(z, cw_ref[:, cs], cb_ref[:, cs], HYENA_CONV_LEFT, groups).astype(BF16)
        _store_lane_tiles(o_ref, 0, z, groups)


def _store_lane_tiles(o_ref, lead, val, groups):
    for lt in range(val.shape[1] // LANES):
        o_ref[lead, lt] = val[:, lt * LANES:(lt + 1) * LANES].reshape(-1, groups, LANES)


def _load_lane_tiles(ref, lead):
    nlt, t2, groups, _ = ref.shape[1:]
    return jnp.concatenate([ref[lead, lt].reshape(t2 * groups, LANES) for lt in range(nlt)], axis=1)


def _lane_tiled_spec(d, groups):
    return pl.BlockSpec((1, d // LANES, GRID_W, groups, LANES), lambda i, j: (i, 0, 0, j, 0))


def _hyproj_call(x, mv, ng, win, bin_, cw, cb, *, groups):
    b, _, rows, d = x.shape
    tile_spec = pl.BlockSpec((1, GRID_W, groups, d), lambda i, j: (i, 0, j, 0))
    out_spec = _lane_tiled_spec(d, groups)
    out_sds = jax.ShapeDtypeStruct((b, d // LANES, GRID_W, rows, LANES), BF16)
    return pl.pallas_call(
        _hyproj_kernel,
        grid=(b, rows // groups),
        in_specs=[
            tile_spec,
            pl.BlockSpec((1, SUBLANES, d), lambda i, j: (i, 0, 0)),
            _const_spec((1, d)),
            _const_spec(win.shape),
            _const_spec(bin_.shape),
            _const_spec(cw.shape),
            _const_spec(cb.shape),
        ],
        out_specs=[out_spec, out_spec, out_spec],
        out_shape=[out_sds, out_sds, out_sds],
        compiler_params=_cparams(("parallel", "parallel")),
    )(x, mv, ng, win, bin_, cw, cb)


def _filter_kernel(pos_ref, fw1_ref, fb1_ref, fw2_ref, fb2_ref, fw3_ref, fb3_ref, fw4_ref, freq_ref,
                   deltas_ref, h_ref, nrm_ref):
    groups = pos_ref.shape[1]
    pe = pos_ref.shape[2]
    pos = pos_ref[...].reshape(GRID_W * groups, pe)
    half_rows = pos.shape[0] // 2
    half_t2 = GRID_W // 2
    pos2 = jnp.concatenate([pos[:half_rows], pos[half_rows:]], axis=1)
    freq = freq_ref[...]

    def hdot(a, b):
        return jnp.dot(a, b, preferred_element_type=F32, precision=HIGHEST)

    h = jnp.sin(freq * (hdot(pos2, fw1_ref[...]) + fb1_ref[...]))
    h = jnp.sin(freq * (hdot(h, fw2_ref[...]) + fb2_ref[...]))
    h = jnp.sin(freq * (hdot(h, fw3_ref[...]) + fb3_ref[...])).astype(BF16)
    d = deltas_ref.shape[1]
    n4 = fw4_ref.shape[1] // 2
    nparts = n4 // d
    sums = [jnp.zeros((1, d), F32)] * nparts
    for s in range(2):
        decay = jnp.exp(-pos2[:, s * pe:s * pe + 1] * deltas_ref[...])
        t2s = slice(s * half_t2, (s + 1) * half_t2)
        for p in range(nparts):
            cs = slice(p * d, (p + 1) * d)
            hp = _bdot(h, fw4_ref[:, s * n4 + p * d:s * n4 + (p + 1) * d]) * decay
            sums[p] = sums[p] + jnp.sum(jnp.abs(hp), axis=0, keepdims=True)
            hp16 = hp.astype(BF16)
            for lt in range(d // LANES):
                h_ref[p * (d // LANES) + lt, t2s] = (
                    hp16[:, lt * LANES:(lt + 1) * LANES].reshape(half_t2, groups, LANES))
    half = nparts // 2
    tot = jnp.concatenate([sums[p] + sums[p + half] for p in range(half)], axis=1)

    @pl.when(pl.program_id(0) == 0)
    def _():
        nrm_ref[...] = jnp.zeros_like(nrm_ref)

    nrm_ref[...] += tot


def _filter_call(pos, fw1, fb1, fw2, fb2, fw3, fb3, fw4, freq, deltas, *, groups):
    _, rows, pe = pos.shape
    fh = fw2.shape[0]
    n4 = fw4.shape[1] // 2
    d = deltas.shape[1]
    return pl.pallas_call(
        _filter_kernel,
        grid=(rows // groups,),
        in_specs=[
            pl.BlockSpec((GRID_W, groups, pe), lambda j: (0, j, 0)),
            _const_spec(fw1.shape), _const_spec((1, fh)),
            _const_spec((fh, fh)), _const_spec((1, fh)),
            _const_spec((fh, fh)), _const_spec((1, fh)),
            _const_spec(fw4.shape), _const_spec((1, fh)),
            _const_spec((1, d)),
        ],
        out_specs=[
            pl.BlockSpec((n4 // LANES, GRID_W, groups, LANES), lambda j: (0, 0, j, 0)),
            pl.BlockSpec((1, n4 // 2), lambda j: (0, 0)),
        ],
        out_shape=[
            jax.ShapeDtypeStruct((n4 // LANES, GRID_W, rows, LANES), BF16),
            jax.ShapeDtypeStruct((1, n4 // 2), F32),
        ],
        compiler_params=_cparams(("arbitrary",)),
    )(pos, fw1, fb1, fw2, fb2, fw3, fb3, fw4, freq, deltas)


@functools.lru_cache(maxsize=None)
def _dft_constants(seq_len):
    n = 2 * seq_len
    n2 = DFT_N2
    n1 = n // n2
    nt1 = n1 // 2
    nf = n1 // 2 + 1
    slots = _round_up(nf, SUBLANES)
    t1 = np.arange(nt1)[None, :]
    f1 = np.arange(slots)[:, None]
    live = (f1 < nf).astype(np.float64)
    ang1 = 2.0 * np.pi * (t1 * f1 % n1) / n1
    cos1, sin1 = np.cos(ang1) * live, np.sin(ang1) * live
    cf = np.full((slots, 1), 2.0)
    cf[0] = 1.0
    cf[nf - 1] = 1.0
    f1h = np.concatenate([cos1, -sin1], axis=0)
    f1i = np.concatenate([cos1 * cf, -sin1 * cf], axis=0).T
    t2 = np.arange(n2)[None, None, :]
    f2 = np.arange(n2)[None, :, None]
    ff1 = np.arange(nf)[:, None, None]
    ang2 = 2.0 * np.pi * ((t2 * (ff1 + n1 * f2)) % n) / n
    gr, gim = np.cos(ang2), -np.sin(ang2)
    g = np.concatenate([np.concatenate([gr, -gim], axis=2),
                        np.concatenate([gim, gr], axis=2)], axis=1)
    as32 = lambda a: np.ascontiguousarray(a, dtype=np.float32)
    return as32(f1h), as32(g), as32(f1i), nt1, nf, slots


def _spec_rows(t2, slots):
    return pl.ds(t2, slots, stride=SPEC_PITCH)


def _dft_stage1(load_slab, f1h_ref, spec_scr, slots):
    def body(t2, carry):
        a = _bdot(f1h_ref[...], load_slab(t2))
        for lt in range(2):
            ls = slice(lt * LANES, (lt + 1) * LANES)
            spec_scr[lt, _spec_rows(t2, slots), :] = a[:slots, ls]
            spec_scr[lt, _spec_rows(t2 + DFT_N2, slots), :] = a[slots:, ls]
        return carry

    lax.fori_loop(0, DFT_N2, body, 0, unroll=SLAB_UNROLL)


def _spec_slot_load(spec_scr, slot):
    base = pl.multiple_of(slot * SPEC_PITCH, SUBLANES)
    rows = pl.ds(base, 2 * DFT_N2)
    return jnp.concatenate([spec_scr[0, rows, :], spec_scr[1, rows, :]], axis=1), rows


def _filtfft_kernel(hf_ref, hb_ref, nrm_ref, f1h_ref, g_ref, k_ref, spec_scr, *, nf, slots):
    inv = 1.0 / nrm_ref[...]

    def slab(t2):
        return jnp.concatenate([hf_ref[0, t2], hb_ref[0, t2]], axis=1)

    _dft_stage1(slab, f1h_ref, spec_scr, slots)

    def body(f1, carry):
        a, _ = _spec_slot_load(spec_scr, f1)
        xs = _bdot(g_ref[f1], a.astype(BF16))
        fwd, bwd = xs[:, :LANES], xs[:, LANES:]
        k_ref[0, 0, f1] = (jnp.concatenate(
            [fwd[:DFT_N2] + bwd[:DFT_N2], fwd[DFT_N2:] - bwd[DFT_N2:]], axis=0) * inv).astype(BF16)
        return carry

    lax.fori_loop(0, nf, body, 0, unroll=FREQ_UNROLL)


def _filtfft_call(hraw, nrm, seq_len):
    f1h, g, _, nt1, nf, slots = _dft_constants(seq_len)
    nlt, _, rows, _ = hraw.shape
    nct = nlt // 4
    return pl.pallas_call(
        functools.partial(_filtfft_kernel, nf=nf, slots=slots),
        grid=(2, nct),
        in_specs=[
            pl.BlockSpec((1, GRID_W, rows, LANES), lambda o, c: (o * nct + c, 0, 0, 0)),
            pl.BlockSpec((1, GRID_W, rows, LANES), lambda o, c: (2 * nct + o * nct + c, 0, 0, 0)),
            pl.BlockSpec((1, LANES), lambda o, c: (0, o * nct + c)),
            _const_spec(f1h.shape),
            _const_spec(g.shape),
        ],
        out_specs=pl.BlockSpec((1, 1, nf, 2 * DFT_N2, LANES), lambda o, c: (o, c, 0, 0, 0)),
        out_shape=jax.ShapeDtypeStruct((2, nct, nf, 2 * DFT_N2, LANES), BF16),
        scratch_shapes=[pltpu.VMEM((2, slots * SPEC_PITCH, LANES), F32)],
        compiler_params=_cparams(("parallel", "parallel")),
    )(hraw, hraw, nrm, jnp.asarray(f1h).astype(BF16), jnp.asarray(g).astype(BF16))


def _longconv_kernel(v_ref, m_ref, k_ref, skip_ref, f1h_ref, g_ref, f1i_ref, o_ref,
                     spec_scr, *, nf, slots):
    def slab(t2):
        return jnp.concatenate([v_ref[0, 0, t2], v_ref[1, 0, t2]], axis=1)

    _dft_stage1(slab, f1h_ref, spec_scr, slots)

    def mid(f1, carry):
        a, rows = _spec_slot_load(spec_scr, f1)
        xs = _bdot(g_ref[f1], a.astype(BF16))
        kf = k_ref[0, 0, f1].astype(F32)
        kr = jnp.concatenate([kf[:DFT_N2]] * 2, axis=1)
        ki = jnp.concatenate([kf[DFT_N2:]] * 2, axis=1)
        xr, xi = xs[:DFT_N2], xs[DFT_N2:]
        ys = jnp.concatenate([xr * kr - xi * ki, xr * ki + xi * kr], axis=0).astype(BF16)
        bs = lax.dot_general(g_ref[f1], ys, (((0,), (0,)), ((), ())), preferred_element_type=F32)
        spec_scr[0, rows, :] = bs[:, :LANES]
        spec_scr[1, rows, :] = bs[:, LANES:]
        return carry

    lax.fori_loop(0, nf, mid, 0, unroll=FREQ_UNROLL)
    skip = skip_ref[...]

    def last(t2, carry):
        halves = []
        for lt in range(2):
            re = spec_scr[lt, _spec_rows(t2, slots), :]
            im = spec_scr[lt, _spec_rows(t2 + DFT_N2, slots), :]
            halves.append(jnp.concatenate([re, im], axis=0))
        y = _bdot(f1i_ref[...], jnp.concatenate(halves, axis=1).astype(BF16))
        for b in range(2):
            conv = y[:, b * LANES:(b + 1) * LANES]
            vs = v_ref[b, 0, t2].astype(F32)
            o_ref[b, 0, t2] = (m_ref[b, 0, t2].astype(F32) * (conv + vs * skip)).astype(BF16)
        return carry

    lax.fori_loop(0, DFT_N2, last, 0, unroll=SLAB_UNROLL)


def _longconv_call(v, m, kf, order, skip, seq_len):
    f1h, g, f1i, nt1, nf, slots = _dft_constants(seq_len)
    b, nct, _, rows, _ = v.shape
    seq_spec = pl.BlockSpec((2, 1, GRID_W, rows, LANES), lambda c, i: (i, c, 0, 0, 0))
    scale = 1.0 / (2 * seq_len)
    return pl.pallas_call(
        functools.partial(_longconv_kernel, nf=nf, slots=slots),
        grid=(nct, b // 2),
        in_specs=[
            seq_spec,
            seq_spec,
            pl.BlockSpec((1, 1, nf, 2 * DFT_N2, LANES), lambda c, i: (order, c, 0, 0, 0)),
            pl.BlockSpec((1, LANES), lambda c, i: (0, c)),
            _const_spec(f1h.shape),
            _const_spec(g.shape),
            _const_spec(f1i.shape),
        ],
        out_specs=seq_spec,
        out_shape=jax.ShapeDtypeStruct(v.shape, BF16),
        scratch_shapes=[pltpu.VMEM((2, slots * SPEC_PITCH, LANES), F32)],
        compiler_params=_cparams(("parallel", "parallel")),
    )(v, m, kf, skip, jnp.asarray(f1h).astype(BF16), jnp.asarray(g).astype(BF16),
      jnp.asarray(f1i * scale).astype(BF16))


def _mod_rows(mod_layer, nb, d):
    m = mod_layer.reshape(SUBLANES, 6, d)
    m = jnp.concatenate([m, jnp.zeros((SUBLANES, SUBLANES - 6, d), F32)], axis=1)
    return m[:nb], jnp.broadcast_to(m[nb:nb + 1], (nb, SUBLANES, d))


@functools.lru_cache(maxsize=None)
def _filter_constants(seq_len, d):
    t = np.linspace(0.0, 1.0, seq_len)[:, None]
    w = (2.0 * np.pi / seq_len) * np.arange(seq_len)[:, None]
    bands = np.linspace(1e-4, FILTER_BANDS - 1, FILTER_BANDS)
    pos = np.concatenate([t, np.cos(bands * w), -np.sin(bands * w)], axis=-1)
    pe = _round_up(pos.shape[1], LANES)
    pos = np.pad(pos, ((0, 0), (0, pe - pos.shape[1])))
    pos = pos.reshape(seq_len // GRID_W, GRID_W, pe).transpose(1, 0, 2)
    deltas = np.abs(np.linspace(math.log(FILTER_TARGET) / SLOW_DECAY_PCT,
                                math.log(FILTER_TARGET) / FAST_DECAY_PCT, d))[None, :]
    return np.ascontiguousarray(pos, dtype=np.float32), np.ascontiguousarray(deltas, dtype=np.float32)


def kernel(x, c, ctx, c_ctx, ada_w, ada_b, norm_g, mlp_w1, mlp_w2, lru_w_in, lru_b_in, lru_conv_w, lru_conv_b, lru_w_a, lru_b_a, lru_w_i, lru_b_i, lru_lambda, lru_w_out, lru_b_out, hy_w_in, hy_b_in, hy_conv_w, hy_conv_b, hy_fw1, hy_fb1, hy_fw2, hy_fb2, hy_fw3, hy_fb3, hy_fw4, hy_freq, hy_skip, hy_w_out, hy_b_out, final_g):
    nb, seq, d = x.shape
    ctx_len = ctx.shape[1]
    w = lru_w_out.shape[1]
    lru_tile = NSEG * GRID_W
    assert nb + 1 <= SUBLANES and nb % 2 == 0
    assert seq % (BF16_ROWS * GRID_W) == 0 and ctx_len % (NSEG * SUBLANES) == 0

    cvec = jnp.concatenate([c, c_ctx[None, :], jnp.zeros((SUBLANES - nb - 1, d), F32)], axis=0)
    mod = _ada_call(cvec, ada_w, ada_b)

    mv_l, mv_c = _mod_rows(mod[0], nb, d)
    ng = norm_g[0, 0][None, :]
    wgate = lru_w_in[0, :, :w].astype(BF16)
    wrec = lru_w_in[0, :, w:].astype(BF16)
    bgate = lru_b_in[0, :w][None, :]
    brec = lru_b_in[0, w:][None, :]
    cw = lru_conv_w[0]
    cb = lru_conv_b[0][None, :]
    wg = (0.5 * jnp.concatenate([lru_w_a[0, 0], lru_w_i[0, 0], lru_w_a[0, 1], lru_w_i[0, 1]], axis=-1)).astype(BF16)
    bg = 0.5 * jnp.concatenate([lru_b_a[0, 0], lru_b_i[0, 0], lru_b_a[0, 1], lru_b_i[0, 1]], axis=-1)[:, None, :]
    lam = lru_lambda[0]
    lru_w = (ng, wrec, brec, cw, cb, wg, bg, lam)
    agg_c = _lru_pass1_call(ctx, mv_c, *lru_w, tile=ctx_len, wrap=True)
    agg_l, pq, xt = _lru_mix_call(x, mv_l, *lru_w, wgate, bgate, tile=lru_tile)
    hin = _segscan_call(agg_l, agg_c)
    x2 = _mlp_call(xt, mv_l, norm_g[0, 1][None, :], mlp_w1[0].astype(BF16), mlp_w2[0].astype(BF16), groups=NSEG,
                   lru=(pq, hin, lru_w_out[0].astype(BF16), lru_b_out[0][None, :]))

    mv1, _ = _mod_rows(mod[1], nb, d)
    v, xa, xb = _hyproj_call(x2, mv1, norm_g[1, 0][None, :], hy_w_in[0].astype(BF16), hy_b_in[0][None, :],
                             hy_conv_w[0], hy_conv_b[0][None, :], groups=BF16_ROWS)
    pos, deltas = (jnp.asarray(a) for a in _filter_constants(seq, d))
    fw1 = jnp.pad(hy_fw1[0], ((0, pos.shape[2] - hy_fw1.shape[1]), (0, 0)))
    twin = lambda wmat: jnp.kron(jnp.eye(2, dtype=F32), wmat)
    twice = lambda vec: jnp.tile(vec[None, :], (1, 2))
    hraw, nrm = _filter_call(pos, twin(fw1), twice(hy_fb1[0]), twin(hy_fw2[0]), twice(hy_fb2[0]),
                             twin(hy_fw3[0]), twice(hy_fb3[0]), twin(hy_fw4[0]).astype(BF16),
                             twice(hy_freq[0]), deltas, groups=BF16_ROWS)
    kf = _filtfft_call(hraw, nrm, seq)
    v1 = _longconv_call(v, xa, kf, 0, hy_skip[0, 0][None, :], seq)
    v2 = _longconv_call(v1, xb, kf, 1, hy_skip[0, 1][None, :], seq)
    return _mlp_call(x2, mv1, norm_g[1, 1][None, :], mlp_w1[1].astype(BF16), mlp_w2[1].astype(BF16),
                     groups=BF16_ROWS, hyena=(v2, hy_w_out[0].astype(BF16), hy_b_out[0][None, :]),
                     final_g=final_g[None, :])
```

```python
import functools
import math

import numpy as np
import jax
import jax.numpy as jnp
from jax import lax
from jax.experimental import pallas as pl
from jax.experimental.pallas import tpu as pltpu

F32 = jnp.float32
BF16 = jnp.bfloat16
HIGHEST = lax.Precision.HIGHEST

NORM_EPS = 1e-6
GRID_W = 64
LRU_HEADS = 4
LRU_C = 8.0
LRU_CONV_LEFT = 2
HYENA_CONV_LEFT = 1
FILTER_BANDS = 16
FILTER_TARGET = 1e-2
FAST_DECAY_PCT = 0.3
SLOW_DECAY_PCT = 1.5

SUBLANES = 8
LANES = 128
NSEG = SUBLANES
BF16_ROWS = 16
V7X_VMEM_BYTES = 64 * 1024 * 1024
VMEM_LIMIT = V7X_VMEM_BYTES - 6 * 1024 * 1024

DFT_N2 = GRID_W
SPEC_PAD = 4
SPEC_PITCH = 2 * DFT_N2 + SPEC_PAD
SLAB_UNROLL = 32
FREQ_UNROLL = 43


def _cparams(sem):
    return pltpu.CompilerParams(dimension_semantics=sem, vmem_limit_bytes=VMEM_LIMIT)


def _const_spec(shape):
    nd = len(shape)
    return pl.BlockSpec(shape, lambda *_: (0,) * nd, pipeline_mode=pl.Buffered(1))


def _round_up(a, m):
    return (a + m - 1) // m * m


def _rms_norm(x, g):
    ms = jnp.mean(x * x, axis=-1, keepdims=True)
    return (x * lax.rsqrt(ms + NORM_EPS)) * g


def _modulate(x, g, shift, scale):
    return _rms_norm(x, g) * (1.0 + scale) + shift


def _gelu_tanh(x):
    c = math.sqrt(2.0 / math.pi)
    return x * (0.5 * (1.0 + jnp.tanh(c * (x + 0.044715 * (x * x * x)))))


def _softplus(x):
    return jnp.maximum(x, 0.0) + jnp.log1p(jnp.exp(-jnp.abs(x)))


def _bdot(a, b):
    return jnp.dot(a, b, preferred_element_type=F32)


def _to_tile_order(x, groups):
    n, d = x.shape
    return jnp.swapaxes(x.reshape(groups, n // groups, d), 0, 1).reshape(n, d)


def _from_tile_order(x, groups):
    n, d = x.shape
    return jnp.swapaxes(x.reshape(n // groups, groups, d), 0, 1).reshape(n, d)


def _wrapped_edge(edge, step):
    pieces = []
    for p in range(edge.shape[0] // SUBLANES):
        piece = edge[p * SUBLANES:(p + 1) * SUBLANES]
        sub = lax.broadcasted_iota(jnp.int32, piece.shape, 0)
        if step > 0:
            pieces.append(jnp.where(sub == 0, 0.0, pltpu.roll(piece, 1, 0)))
        else:
            pieces.append(jnp.where(sub == SUBLANES - 1, 0.0, pltpu.roll(piece, SUBLANES - 1, 0)))
    return jnp.concatenate(pieces, axis=0) if len(pieces) > 1 else pieces[0]


def _shift_tokens(z, o, groups, wrap):
    n = abs(o) * groups
    rows = z.shape[0]
    if o < 0:
        edge = _wrapped_edge(z[rows - n:], 1) if wrap else jnp.zeros((n, z.shape[1]), z.dtype)
        return jnp.concatenate([edge, z[:rows - n]], axis=0)
    edge = _wrapped_edge(z[:n], -1) if wrap else jnp.zeros((n, z.shape[1]), z.dtype)
    return jnp.concatenate([z[n:], edge], axis=0)


def _row_conv(z, w, b, left, groups, wrap=False):
    acc = b + w[left:left + 1] * z
    for k in range(w.shape[0]):
        if k != left:
            acc = acc + w[k:k + 1] * _shift_tokens(z, k - left, groups, wrap)
    return acc


def _ada_kernel(c_ref, w_ref, b_ref, o_ref):
    c = c_ref[...]
    cond = c * jax.nn.sigmoid(c)
    o_ref[0] = jnp.dot(cond, w_ref[0], preferred_element_type=F32, precision=HIGHEST) + b_ref[0]


def _ada_call(cvec, ada_w, ada_b):
    depth, d, n = ada_w.shape
    tn = 1536
    return pl.pallas_call(
        _ada_kernel,
        grid=(depth, n // tn),
        in_specs=[
            pl.BlockSpec((SUBLANES, d), lambda i, j: (0, 0)),
            pl.BlockSpec((1, d, tn), lambda i, j: (i, 0, j)),
            pl.BlockSpec((1, 1, tn), lambda i, j: (i, 0, j)),
        ],
        out_specs=pl.BlockSpec((1, SUBLANES, tn), lambda i, j: (i, 0, j)),
        out_shape=jax.ShapeDtypeStruct((depth, SUBLANES, n), F32),
        compiler_params=_cparams(("parallel", "parallel")),
    )(cvec, ada_w, ada_b.reshape(depth, 1, n))


def _lru_head_coeffs(xh, wg_h, bg_h, half_c_sp_h, ab_scr, cs):
    hb = xh.shape[1]
    gates = _bdot(xh.astype(BF16), wg_h) + bg_h
    xh_half = 0.5 * xh
    for e in range(2):
        tr = jnp.tanh(gates[:, (2 * e) * hb:(2 * e + 1) * hb])
        ti = jnp.tanh(gates[:, (2 * e + 1) * hb:(2 * e + 2) * hb])
        c = half_c_sp_h[e:e + 1]
        a = jnp.exp(c * tr + c)
        q = 1.0 - a * a
        wgt = jnp.where(q > 0.0, q * lax.rsqrt(q), 0.0) * xh_half
        ab_scr[e, 0, :, cs] = a
        ab_scr[e, 1, :, cs] = wgt * ti + wgt


def _lru_head_scan(ab_scr, agg_ref, cs, seg_len, keep):
    hb = cs.stop - cs.start
    pf = pb = jnp.ones((NSEG, hb), F32)
    hf = hbk = jnp.zeros((NSEG, hb), F32)
    for i in range(seg_len):
        rf = slice(i * NSEG, (i + 1) * NSEG)
        rb = slice((seg_len - 1 - i) * NSEG, (seg_len - i) * NSEG)
        af = ab_scr[0, 0, rf, cs]
        ab = ab_scr[1, 0, rb, cs]
        pf, hf = pf * af, af * hf + ab_scr[0, 1, rf, cs]
        pb, hbk = pb * ab, ab * hbk + ab_scr[1, 1, rb, cs]
        if keep:
            ab_scr[0, 0, rf, cs] = hf
            ab_scr[0, 1, rf, cs] = pf
            ab_scr[1, 0, rb, cs] = hbk
            ab_scr[1, 1, rb, cs] = pb
    agg_ref[0, 0, 0, :, cs] = pf
    agg_ref[0, 0, 1, :, cs] = hf
    agg_ref[0, 1, 0, :, cs] = pb
    agg_ref[0, 1, 1, :, cs] = hbk


def _lru_local_scan(u, wrec_ref, brec_ref, cw_ref, cb_ref, wg_ref, bg_ref, lam_ref, agg_ref, ab_scr,
                    *, wrap, keep, after_head=None):
    seg_len = u.shape[0] // NSEG
    w = ab_scr.shape[-1]
    hb = w // LRU_HEADS
    zr = _bdot(u, wrec_ref[...]) + brec_ref[...]
    xl = _row_conv(zr, cw_ref[...], cb_ref[...], LRU_CONV_LEFT, NSEG, wrap)
    half_c_sp = (-0.5 * LRU_C) * _softplus(-lam_ref[...])
    for h in range(LRU_HEADS):
        cs = slice(h * hb, (h + 1) * hb)
        _lru_head_coeffs(xl[:, cs], wg_ref[h], bg_ref[h], half_c_sp[:, cs], ab_scr, cs)
        _lru_head_scan(ab_scr, agg_ref, cs, seg_len, keep)
        if after_head is not None:
            after_head(cs)


def _lru_pass1_kernel(x_ref, mv_ref, ng_ref, wrec_ref, brec_ref, cw_ref, cb_ref, wg_ref, bg_ref,
                      lam_ref, agg_ref, ab_scr, *, wrap):
    mv = mv_ref[0]
    xp = _to_tile_order(x_ref[0], NSEG)
    u = _modulate(xp, ng_ref[...], mv[0:1], mv[1:2]).astype(BF16)
    _lru_local_scan(u, wrec_ref, brec_ref, cw_ref, cb_ref, wg_ref, bg_ref, lam_ref, agg_ref, ab_scr,
                    wrap=wrap, keep=False)


def _lru_mix_kernel(x_ref, mv_ref, ng_ref, wrec_ref, brec_ref, cw_ref, cb_ref, wg_ref, bg_ref,
                    lam_ref, wgate_ref, bgate_ref, agg_ref, pq_ref, xt_ref, ab_scr):
    mv = mv_ref[0]
    xp = _to_tile_order(x_ref[0], NSEG)
    xt_ref[0] = xp.reshape(xt_ref.shape[1:])
    u = _modulate(xp, ng_ref[...], mv[0:1], mv[1:2]).astype(BF16)

    def emit(cs):
        gate = _gelu_tanh(_bdot(u, wgate_ref[:, cs]) + bgate_ref[:, cs])
        pq_ref[0, 0, 0, :, cs] = ((ab_scr[0, 0, :, cs] + ab_scr[1, 0, :, cs]) * gate).astype(BF16)
        pq_ref[0, 0, 1, :, cs] = (ab_scr[0, 1, :, cs] * gate).astype(BF16)
        pq_ref[0, 0, 2, :, cs] = (ab_scr[1, 1, :, cs] * gate).astype(BF16)

    _lru_local_scan(u, wrec_ref, brec_ref, cw_ref, cb_ref, wg_ref, bg_ref, lam_ref, agg_ref, ab_scr,
                    wrap=False, keep=True, after_head=emit)


def _lru_weight_specs(d, w, cw, wg, bg, lam):
    return [
        _const_spec((1, d)),
        _const_spec((d, w)),
        _const_spec((1, w)),
        _const_spec(cw.shape),
        _const_spec((1, w)),
        _const_spec(wg.shape),
        _const_spec(bg.shape),
        _const_spec(lam.shape),
    ]


def _lru_pass1_call(x, mv, ng, wrec, brec, cw, cb, wg, bg, lam, *, tile, wrap):
    b, s, d = x.shape
    w = wrec.shape[1]
    nt = s // tile
    return pl.pallas_call(
        functools.partial(_lru_pass1_kernel, wrap=wrap),
        grid=(b, nt),
        in_specs=[
            pl.BlockSpec((1, tile, d), lambda i, j: (i, j, 0)),
            pl.BlockSpec((1, SUBLANES, d), lambda i, j: (i, 0, 0)),
        ] + _lru_weight_specs(d, w, cw, wg, bg, lam),
        out_specs=pl.BlockSpec((1, 2, 2, NSEG, w), lambda i, j: (i, 0, 0, j, 0)),
        out_shape=jax.ShapeDtypeStruct((b, 2, 2, nt * NSEG, w), F32),
        scratch_shapes=[pltpu.VMEM((2, 2, tile, w), F32)],
        compiler_params=_cparams(("parallel", "parallel")),
    )(x, mv, ng, wrec, brec, cw, cb, wg, bg, lam)


def _segscan_kernel(aggl_ref, aggc_ref, hin_ref):
    nsl = aggl_ref.shape[3]
    nsc = aggc_ref.shape[3]
    w = aggl_ref.shape[-1]
    for e in range(2):
        order_c = range(nsc) if e == 0 else range(nsc - 1, -1, -1)
        order_l = range(nsl) if e == 0 else range(nsl - 1, -1, -1)
        st = jnp.zeros((1, w), F32)
        for s in order_c:
            st = aggc_ref[0, e, 0, s:s + 1, :] * st + aggc_ref[0, e, 1, s:s + 1, :]
        for s in order_l:
            hin_ref[0, e, s:s + 1, :] = st
            st = aggl_ref[0, e, 0, s:s + 1, :] * st + aggl_ref[0, e, 1, s:s + 1, :]


def _segscan_call(agg_l, agg_c):
    b, _, _, nsl, w = agg_l.shape
    nsc = agg_c.shape[3]
    return pl.pallas_call(
        _segscan_kernel,
        grid=(b,),
        in_specs=[
            pl.BlockSpec((1, 2, 2, nsl, w), lambda i: (i, 0, 0, 0, 0)),
            pl.BlockSpec((1, 2, 2, nsc, w), lambda i: (i, 0, 0, 0, 0)),
        ],
        out_specs=pl.BlockSpec((1, 2, nsl, w), lambda i: (i, 0, 0, 0)),
        out_shape=jax.ShapeDtypeStruct((b, 2, nsl, w), F32),
        compiler_params=_cparams(("parallel",)),
    )(agg_l, agg_c)


def _lru_mix_call(x, mv, ng, wrec, brec, cw, cb, wg, bg, lam, wgate, bgate, *, tile):
    b, s, d = x.shape
    w = wrec.shape[1]
    nt = s // tile
    return pl.pallas_call(
        _lru_mix_kernel,
        grid=(b, nt),
        in_specs=[
            pl.BlockSpec((1, tile, d), lambda i, j: (i, j, 0)),
            pl.BlockSpec((1, SUBLANES, d), lambda i, j: (i, 0, 0)),
        ] + _lru_weight_specs(d, w, cw, wg, bg, lam) + [
            _const_spec((d, w)),
            _const_spec((1, w)),
        ],
        out_specs=[
            pl.BlockSpec((1, 2, 2, NSEG, w), lambda i, j: (i, 0, 0, j, 0)),
            pl.BlockSpec((1, 1, 3, tile, w), lambda i, j: (i, j, 0, 0, 0)),
            pl.BlockSpec((1, GRID_W, NSEG, d), lambda i, j: (i, 0, j, 0)),
        ],
        out_shape=[
            jax.ShapeDtypeStruct((b, 2, 2, nt * NSEG, w), F32),
            jax.ShapeDtypeStruct((b, nt, 3, tile, w), BF16),
            jax.ShapeDtypeStruct((b, GRID_W, s // GRID_W, d), F32),
        ],
        scratch_shapes=[pltpu.VMEM((2, 2, tile, w), F32)],
        compiler_params=_cparams(("parallel", "parallel")),
    )(x, mv, ng, wrec, brec, cw, cb, wg, bg, lam, wgate, bgate)


FF_CHUNK = 1024


def _mlp_kernel(*refs, pre, final):
    refs = list(refs)
    x_ref = refs.pop(0)
    mv_ref = refs.pop(0)
    ng_ref = refs.pop(0)
    w1_ref = refs.pop(0)
    w2_ref = refs.pop(0)
    if pre == "hyena":
        v_ref = refs.pop(0)
    if pre == "lru":
        pq_ref = refs.pop(0)
        hin_ref = refs.pop(0)
    if pre:
        wout_ref = refs.pop(0)
        bout_ref = refs.pop(0)
    if final:
        fg_ref = refs.pop(0)
    o_ref = refs.pop(0)
    mv = mv_ref[0]
    groups = x_ref.shape[2]
    x = x_ref[0].reshape(GRID_W * groups, x_ref.shape[3])
    if pre == "lru":
        rows, w = pq_ref.shape[-2:]

        def times_entering(plane, e):
            running = pq_ref[0, 0, plane].astype(F32).reshape(rows // NSEG, NSEG, w)
            return (running * hin_ref[0, e][None]).reshape(rows, w)

        mixed = pq_ref[0, 0, 0].astype(F32) + times_entering(1, 0) + times_entering(2, 1)
        x = x + mv[2:3] * (_bdot(mixed.astype(BF16), wout_ref[...]) + bout_ref[...])
    if pre == "hyena":
        x = x + mv[2:3] * (_bdot(_load_lane_tiles(v_ref, 0), wout_ref[...]) + bout_ref[...])
    u = _modulate(x, ng_ref[...], mv[3:4], mv[4:5]).astype(BF16)
    acc = jnp.zeros(x.shape, F32)
    for c in range(w1_ref.shape[1] // FF_CHUNK):
        cs = slice(c * FF_CHUNK, (c + 1) * FF_CHUNK)
        h = jnp.maximum(_bdot(u, w1_ref[:, cs]), 0.0)
        acc = acc + _bdot((h * h).astype(BF16), w2_ref[cs, :])
    out = x + mv[5:6] * acc
    if final:
        o_ref[0] = _from_tile_order(_rms_norm(out, fg_ref[...]), groups)
    else:
        o_ref[0] = out.reshape(o_ref.shape[1:])


def _mlp_call(x, mv, ng, w1, w2, *, groups, hyena=None, lru=None, final_g=None):
    f = w1.shape[1]
    b, _, rows, d = x.shape
    tile_spec = pl.BlockSpec((1, GRID_W, groups, d), lambda i, j: (i, 0, j, 0))
    args = [x, mv, ng, w1, w2]
    in_specs = [
        tile_spec,
        pl.BlockSpec((1, SUBLANES, d), lambda i, j: (i, 0, 0)),
        _const_spec((1, d)),
        _const_spec((d, f)),
        _const_spec((f, d)),
    ]
    pre = None
    if hyena is not None:
        pre = "hyena"
        v, wout, bout = hyena
        args += [v, wout, bout]
        in_specs += [_lane_tiled_spec(d, groups), _const_spec(wout.shape), _const_spec((1, d))]
    if lru is not None:
        pre = "lru"
        pq, hin, wout, bout = lru
        args += [pq, hin, wout, bout]
        in_specs += [
            pl.BlockSpec((1, 1) + pq.shape[2:], lambda i, j: (i, j, 0, 0, 0)),
            pl.BlockSpec((1, 2, NSEG, hin.shape[-1]), lambda i, j: (i, 0, j, 0)),
            _const_spec(wout.shape),
            _const_spec((1, d)),
        ]
    if final_g is not None:
        args.append(final_g)
        in_specs.append(_const_spec((1, d)))
        out_spec = pl.BlockSpec((1, GRID_W * groups, d), lambda i, j: (i, j, 0))
        out_shape = jax.ShapeDtypeStruct((b, GRID_W * rows, d), F32)
    else:
        out_spec = tile_spec
        out_shape = jax.ShapeDtypeStruct((b, GRID_W, rows, d), F32)
    return pl.pallas_call(
        functools.partial(_mlp_kernel, pre=pre, final=final_g is not None),
        grid=(b, rows // groups),
        in_specs=in_specs,
        out_specs=out_spec,
        out_shape=out_shape,
        compiler_params=_cparams(("parallel", "parallel")),
    )(*args)


def _hyproj_kernel(x_ref, mv_ref, ng_ref, win_ref, bin_ref, cw_ref, cb_ref, v_ref, xa_ref, xb_ref):
    mv = mv_ref[0]
    groups = x_ref.shape[2]
    d = x_ref.shape[3]
    x = x_ref[0].reshape(GRID_W * groups, d)
    u = _modulate(x, ng_ref[...], mv[0:1], mv[1:2]).astype(BF16)
    for k, o_ref in enumerate((v_ref, xa_ref, xb_ref)):
        cs = slice(k * d, (k + 1) * d)
        z = _bdot(u, win_ref[:, cs]) + bin_ref[:, cs]
        z = _row_conv(z, cw_ref[:, cs], cb_ref[:, cs], HYENA_CONV_LEFT, groups).astype(BF16)
        _store_lane_tiles(o_ref, 0, z, groups)


def _store_lane_tiles(o_ref, lead, val, groups):
    for lt in range(val.shape[1] // LANES):
        o_ref[lead, lt] = val[:, lt * LANES:(lt + 1) * LANES].reshape(-1, groups, LANES)


def _load_lane_tiles(ref, lead):
    nlt, t2, groups, _ = ref.shape[1:]
    return jnp.concatenate([ref[lead, lt].reshape(t2 * groups, LANES) for lt in range(nlt)], axis=1)


def _lane_tiled_spec(d, groups):
    return pl.BlockSpec((1, d // LANES, GRID_W, groups, LANES), lambda i, j: (i, 0, 0, j, 0))


def _hyproj_call(x, mv, ng, win, bin_, cw, cb, *, groups):
    b, _, rows, d = x.shape
    tile_spec = pl.BlockSpec((1, GRID_W, groups, d), lambda i, j: (i, 0, j, 0))
    out_spec = _lane_tiled_spec(d, groups)
    out_sds = jax.ShapeDtypeStruct((b, d // LANES, GRID_W, rows, LANES), BF16)
    return pl.pallas_call(
        _hyproj_kernel,
        grid=(b, rows // groups),
        in_specs=[
            tile_spec,
            pl.BlockSpec((1, SUBLANES, d), lambda i, j: (i, 0, 0)),
            _const_spec((1, d)),
            _const_spec(win.shape),
            _const_spec(bin_.shape),
            _const_spec(cw.shape),
            _const_spec(cb.shape),
        ],
        out_specs=[out_spec, out_spec, out_spec],
        out_shape=[out_sds, out_sds, out_sds],
        compiler_params=_cparams(("parallel", "parallel")),
    )(x, mv, ng, win, bin_, cw, cb)


def _filter_kernel(pos_ref, fw1_ref, fb1_ref, fw2_ref, fb2_ref, fw3_ref, fb3_ref, fw4_ref, freq_ref,
                   deltas_ref, h_ref, nrm_ref):
    groups = pos_ref.shape[1]
    pe = pos_ref.shape[2]
    pos = pos_ref[...].reshape(GRID_W * groups, pe)
    half_rows = pos.shape[0] // 2
    half_t2 = GRID_W // 2
    pos2 = jnp.concatenate([pos[:half_rows], pos[half_rows:]], axis=1)
    freq = freq_ref[...]

    def hdot(a, b):
        return jnp.dot(a, b, preferred_element_type=F32, precision=HIGHEST)

    h = jnp.sin(freq * (hdot(pos2, fw1_ref[...]) + fb1_ref[...]))
    h = jnp.sin(freq * (hdot(h, fw2_ref[...]) + fb2_ref[...]))
    h = jnp.sin(freq * (hdot(h, fw3_ref[...]) + fb3_ref[...])).astype(BF16)
    d = deltas_ref.shape[1]
    n4 = fw4_ref.shape[1] // 2
    nparts = n4 // d
    sums = [jnp.zeros((1, d), F32)] * nparts
    for s in range(2):
        decay = jnp.exp(-pos2[:, s * pe:s * pe + 1] * deltas_ref[...])
        t2s = slice(s * half_t2, (s + 1) * half_t2)
        for p in range(nparts):
            cs = slice(p * d, (p + 1) * d)
            hp = _bdot(h, fw4_ref[:, s * n4 + p * d:s * n4 + (p + 1) * d]) * decay
            sums[p] = sums[p] + jnp.sum(jnp.abs(hp), axis=0, keepdims=True)
            hp16 = hp.astype(BF16)
            for lt in range(d // LANES):
                h_ref[p * (d // LANES) + lt, t2s] = (
                    hp16[:, lt * LANES:(lt + 1) * LANES].reshape(half_t2, groups, LANES))
    half = nparts // 2
    tot = jnp.concatenate([sums[p] + sums[p + half] for p in range(half)], axis=1)

    @pl.when(pl.program_id(0) == 0)
    def _():
        nrm_ref[...] = jnp.zeros_like(nrm_ref)

    nrm_ref[...] += tot


def _filter_call(pos, fw1, fb1, fw2, fb2, fw3, fb3, fw4, freq, deltas, *, groups):
    _, rows, pe = pos.shape
    fh = fw2.shape[0]
    n4 = fw4.shape[1] // 2
    d = deltas.shape[1]
    return pl.pallas_call(
        _filter_kernel,
        grid=(rows // groups,),
        in_specs=[
            pl.BlockSpec((GRID_W, groups, pe), lambda j: (0, j, 0)),
            _const_spec(fw1.shape), _const_spec((1, fh)),
            _const_spec((fh, fh)), _const_spec((1, fh)),
            _const_spec((fh, fh)), _const_spec((1, fh)),
            _const_spec(fw4.shape), _const_spec((1, fh)),
            _const_spec((1, d)),
        ],
        out_specs=[
            pl.BlockSpec((n4 // LANES, GRID_W, groups, LANES), lambda j: (0, 0, j, 0)),
            pl.BlockSpec((1, n4 // 2), lambda j: (0, 0)),
        ],
        out_shape=[
            jax.ShapeDtypeStruct((n4 // LANES, GRID_W, rows, LANES), BF16),
            jax.ShapeDtypeStruct((1, n4 // 2), F32),
        ],
        compiler_params=_cparams(("arbitrary",)),
    )(pos, fw1, fb1, fw2, fb2, fw3, fb3, fw4, freq, deltas)


@functools.lru_cache(maxsize=None)
def _dft_constants(seq_len):
    n = 2 * seq_len
    n2 = DFT_N2
    n1 = n // n2
    nt1 = n1 // 2
    nf = n1 // 2 + 1
    slots = _round_up(nf, SUBLANES)
    t1 = np.arange(nt1)[None, :]
    f1 = np.arange(slots)[:, None]
    live = (f1 < nf).astype(np.float64)
    ang1 = 2.0 * np.pi * (t1 * f1 % n1) / n1
    cos1, sin1 = np.cos(ang1) * live, np.sin(ang1) * live
    cf = np.full((slots, 1), 2.0)
    cf[0] = 1.0
    cf[nf - 1] = 1.0
    f1h = np.concatenate([cos1, -sin1], axis=0)
    f1i = np.concatenate([cos1 * cf, -sin1 * cf], axis=0).T
    t2 = np.arange(n2)[None, None, :]
    f2 = np.arange(n2)[None, :, None]
    ff1 = np.arange(nf)[:, None, None]
    ang2 = 2.0 * np.pi * ((t2 * (ff1 + n1 * f2)) % n) / n
    gr, gim = np.cos(ang2), -np.sin(ang2)
    g = np.concatenate([np.concatenate([gr, -gim], axis=2),
                        np.concatenate([gim, gr], axis=2)], axis=1)
    as32 = lambda a: np.ascontiguousarray(a, dtype=np.float32)
    return as32(f1h), as32(g), as32(f1i), nt1, nf, slots


def _spec_rows(t2, slots):
    return pl.ds(t2, slots, stride=SPEC_PITCH)


def _dft_stage1(load_slab, f1h_ref, spec_scr, slots):
    def body(t2, carry):
        a = _bdot(f1h_ref[...], load_slab(t2))
        for lt in range(2):
            ls = slice(lt * LANES, (lt + 1) * LANES)
            spec_scr[lt, _spec_rows(t2, slots), :] = a[:slots, ls]
            spec_scr[lt, _spec_rows(t2 + DFT_N2, slots), :] = a[slots:, ls]
        return carry

    lax.fori_loop(0, DFT_N2, body, 0, unroll=SLAB_UNROLL)


def _spec_slot_load(spec_scr, slot):
    rows = pl.ds(slot * SPEC_PITCH, 2 * DFT_N2)
    return jnp.concatenate([spec_scr[0, rows, :], spec_scr[1, rows, :]], axis=1), rows


def _filtfft_kernel(hf_ref, hb_ref, nrm_ref, f1h_ref, g_ref, k_ref, spec_scr, *, nf, slots):
    inv = 1.0 / nrm_ref[...]

    def slab(t2):
        return jnp.concatenate([hf_ref[0, t2], hb_ref[0, t2]], axis=1)

    _dft_stage1(slab, f1h_ref, spec_scr, slots)

    def body(f1, carry):
        a, _ = _spec_slot_load(spec_scr, f1)
        xs = _bdot(g_ref[f1], a.astype(BF16))
        fwd, bwd = xs[:, :LANES], xs[:, LANES:]
        k_ref[0, 0, f1] = (jnp.concatenate(
            [fwd[:DFT_N2] + bwd[:DFT_N2], fwd[DFT_N2:] - bwd[DFT_N2:]], axis=0) * inv).astype(BF16)
        return carry

    lax.fori_loop(0, nf, body, 0, unroll=FREQ_UNROLL)


def _filtfft_call(hraw, nrm, seq_len):
    f1h, g, _, nt1, nf, slots = _dft_constants(seq_len)
    nlt, _, rows, _ = hraw.shape
    nct = nlt // 4
    return pl.pallas_call(
        functools.partial(_filtfft_kernel, nf=nf, slots=slots),
        grid=(2, nct),
        in_specs=[
            pl.BlockSpec((1, GRID_W, rows, LANES), lambda o, c: (o * nct + c, 0, 0, 0)),
            pl.BlockSpec((1, GRID_W, rows, LANES), lambda o, c: (2 * nct + o * nct + c, 0, 0, 0)),
            pl.BlockSpec((1, LANES), lambda o, c: (0, o * nct + c)),
            _const_spec(f1h.shape),
            _const_spec(g.shape),
        ],
        out_specs=pl.BlockSpec((1, 1, nf, 2 * DFT_N2, LANES), lambda o, c: (o, c, 0, 0, 0)),
        out_shape=jax.ShapeDtypeStruct((2, nct, nf, 2 * DFT_N2, LANES), BF16),
        scratch_shapes=[pltpu.VMEM((2, slots * SPEC_PITCH, LANES), F32)],
        compiler_params=_cparams(("parallel", "parallel")),
    )(hraw, hraw, nrm, jnp.asarray(f1h).astype(BF16), jnp.asarray(g).astype(BF16))


def _longconv_kernel(v_ref, m_ref, k_ref, skip_ref, f1h_ref, g_ref, f1i_ref, o_ref,
                     spec_scr, *, nf, slots):
    def slab(t2):
        return jnp.concatenate([v_ref[0, 0, t2], v_ref[1, 0, t2]], axis=1)

    _dft_stage1(slab, f1h_ref, spec_scr, slots)

    def mid(f1, carry):
        a, rows = _spec_slot_load(spec_scr, f1)
        xs = _bdot(g_ref[f1], a.astype(BF16))
        kf = k_ref[0, 0, f1].astype(F32)
        kr = jnp.concatenate([kf[:DFT_N2]] * 2, axis=1)
        ki = jnp.concatenate([kf[DFT_N2:]] * 2, axis=1)
        xr, xi = xs[:DFT_N2], xs[DFT_N2:]
        ys = jnp.concatenate([xr * kr - xi * ki, xr * ki + xi * kr], axis=0).astype(BF16)
        bs = lax.dot_general(g_ref[f1], ys, (((0,), (0,)), ((), ())), preferred_element_type=F32)
        spec_scr[0, rows, :] = bs[:, :LANES]
        spec_scr[1, rows, :] = bs[:, LANES:]
        return carry

    lax.fori_loop(0, nf, mid, 0, unroll=FREQ_UNROLL)
    skip = skip_ref[...]

    def last(t2, carry):
        halves = []
        for lt in range(2):
            re = spec_scr[lt, _spec_rows(t2, slots), :]
            im = spec_scr[lt, _spec_rows(t2 + DFT_N2, slots), :]
            halves.append(jnp.concatenate([re, im], axis=0))
        y = _bdot(f1i_ref[...], jnp.concatenate(halves, axis=1).astype(BF16))
        for b in range(2):
            conv = y[:, b * LANES:(b + 1) * LANES]
            vs = v_ref[b, 0, t2].astype(F32)
            o_ref[b, 0, t2] = (m_ref[b, 0, t2].astype(F32) * (conv + vs * skip)).astype(BF16)
        return carry

    lax.fori_loop(0, DFT_N2, last, 0, unroll=SLAB_UNROLL)


def _longconv_call(v, m, kf, order, skip, seq_len):
    f1h, g, f1i, nt1, nf, slots = _dft_constants(seq_len)
    b, nct, _, rows, _ = v.shape
    seq_spec = pl.BlockSpec((2, 1, GRID_W, rows, LANES), lambda c, i: (i, c, 0, 0, 0))
    scale = 1.0 / (2 * seq_len)
    return pl.pallas_call(
        functools.partial(_longconv_kernel, nf=nf, slots=slots),
        grid=(nct, b // 2),
        in_specs=[
            seq_spec,
            seq_spec,
            pl.BlockSpec((1, 1, nf, 2 * DFT_N2, LANES), lambda c, i: (order, c, 0, 0, 0)),
            pl.BlockSpec((1, LANES), lambda c, i: (0, c)),
            _const_spec(f1h.shape),
            _const_spec(g.shape),
            _const_spec(f1i.shape),
        ],
        out_specs=seq_spec,
        out_shape=jax.ShapeDtypeStruct(v.shape, BF16),
        scratch_shapes=[pltpu.VMEM((2, slots * SPEC_PITCH, LANES), F32)],
        compiler_params=_cparams(("parallel", "parallel")),
    )(v, m, kf, skip, jnp.asarray(f1h).astype(BF16), jnp.asarray(g).astype(BF16),
      jnp.asarray(f1i * scale).astype(BF16))


def _mod_rows(mod_layer, nb, d):
    m = mod_layer.reshape(SUBLANES, 6, d)
    m = jnp.concatenate([m, jnp.zeros((SUBLANES, SUBLANES - 6, d), F32)], axis=1)
    return m[:nb], jnp.broadcast_to(m[nb:nb + 1], (nb, SUBLANES, d))


@functools.lru_cache(maxsize=None)
def _filter_constants(seq_len, d):
    t = np.linspace(0.0, 1.0, seq_len)[:, None]
    w = (2.0 * np.pi / seq_len) * np.arange(seq_len)[:, None]
    bands = np.linspace(1e-4, FILTER_BANDS - 1, FILTER_BANDS)
    pos = np.concatenate([t, np.cos(bands * w), -np.sin(bands * w)], axis=-1)
    pe = _round_up(pos.shape[1], LANES)
    pos = np.pad(pos, ((0, 0), (0, pe - pos.shape[1])))
    pos = pos.reshape(seq_len // GRID_W, GRID_W, pe).transpose(1, 0, 2)
    deltas = np.abs(np.linspace(math.log(FILTER_TARGET) / SLOW_DECAY_PCT,
                                math.log(FILTER_TARGET) / FAST_DECAY_PCT, d))[None, :]
    return np.ascontiguousarray(pos, dtype=np.float32), np.ascontiguousarray(deltas, dtype=np.float32)


def kernel(x, c, ctx, c_ctx, ada_w, ada_b, norm_g, mlp_w1, mlp_w2, lru_w_in, lru_b_in, lru_conv_w, lru_conv_b, lru_w_a, lru_b_a, lru_w_i, lru_b_i, lru_lambda, lru_w_out, lru_b_out, hy_w_in, hy_b_in, hy_conv_w, hy_conv_b, hy_fw1, hy_fb1, hy_fw2, hy_fb2, hy_fw3, hy_fb3, hy_fw4, hy_freq, hy_skip, hy_w_out, hy_b_out, final_g):
    nb, seq, d = x.shape
    ctx_len = ctx.shape[1]
    w = lru_w_out.shape[1]
    lru_tile = NSEG * GRID_W
    assert nb + 1 <= SUBLANES and nb % 2 == 0
    assert seq % (BF16_ROWS * GRID_W) == 0 and ctx_len % (NSEG * SUBLANES) == 0

    cvec = jnp.concatenate([c, c_ctx[None, :], jnp.zeros((SUBLANES - nb - 1, d), F32)], axis=0)
    mod = _ada_call(cvec, ada_w, ada_b)

    mv_l, mv_c = _mod_rows(mod[0], nb, d)
    ng = norm_g[0, 0][None, :]
    wgate = lru_w_in[0, :, :w].astype(BF16)
    wrec = lru_w_in[0, :, w:].astype(BF16)
    bgate = lru_b_in[0, :w][None, :]
    brec = lru_b_in[0, w:][None, :]
    cw = lru_conv_w[0]
    cb = lru_conv_b[0][None, :]
    wg = (0.5 * jnp.concatenate([lru_w_a[0, 0], lru_w_i[0, 0], lru_w_a[0, 1], lru_w_i[0, 1]], axis=-1)).astype(BF16)
    bg = 0.5 * jnp.concatenate([lru_b_a[0, 0], lru_b_i[0, 0], lru_b_a[0, 1], lru_b_i[0, 1]], axis=-1)[:, None, :]
    lam = lru_lambda[0]
    lru_w = (ng, wrec, brec, cw, cb, wg, bg, lam)
    agg_c = _lru_pass1_call(ctx, mv_c, *lru_w, tile=ctx_len, wrap=True)
    agg_l, pq, xt = _lru_mix_call(x, mv_l, *lru_w, wgate, bgate, tile=lru_tile)
    hin = _segscan_call(agg_l, agg_c)
    x2 = _mlp_call(xt, mv_l, norm_g[0, 1][None, :], mlp_w1[0].astype(BF16), mlp_w2[0].astype(BF16), groups=NSEG,
                   lru=(pq, hin, lru_w_out[0].astype(BF16), lru_b_out[0][None, :]))

    mv1, _ = _mod_rows(mod[1], nb, d)
    v, xa, xb = _hyproj_call(x2, mv1, norm_g[1, 0][None, :], hy_w_in[0].astype(BF16), hy_b_in[0][None, :],
                             hy_conv_w[0], hy_conv_b[0][None, :], groups=BF16_ROWS)
    pos, deltas = (jnp.asarray(a) for a in _filter_constants(seq, d))
    fw1 = jnp.pad(hy_fw1[0], ((0, pos.shape[2] - hy_fw1.shape[1]), (0, 0)))
    twin = lambda wmat: jnp.kron(jnp.eye(2, dtype=F32), wmat)
    twice = lambda vec: jnp.tile(vec[None, :], (1, 2))
    hraw, nrm = _filter_call(pos, twin(fw1), twice(hy_fb1[0]), twin(hy_fw2[0]), twice(hy_fb2[0]),
                             twin(hy_fw3[0]), twice(hy_fb3[0]), twin(hy_fw4[0]).astype(BF16),
                             twice(hy_freq[0]), deltas, groups=BF16_ROWS)
    kf = _filtfft_call(hraw, nrm, seq)
    v1 = _longconv_call(v, xa, kf, 0, hy_skip[0, 0][None, :], seq)
    v2 = _longconv_call(v1, xb, kf, 1, hy_skip[0, 1][None, :], seq)
    return _mlp_call(x2, mv1, norm_g[1, 1][None, :], mlp_w1[1].astype(BF16), mlp_w2[1].astype(BF16),
                     groups=BF16_ROWS, hyena=(v2, hy_w_out[0].astype(BF16), hy_b_out[0][None, :]),
                     final_g=final_g[None, :])
```

```python
import functools
import math

import numpy as np
import jax
import jax.numpy as jnp
from jax import lax
from jax.experimental import pallas as pl
from jax.experimental.pallas import tpu as pltpu

F32 = jnp.float32
BF16 = jnp.bfloat16
HIGHEST = lax.Precision.HIGHEST

NORM_EPS = 1e-6
GRID_W = 64
LRU_HEADS = 4
LRU_C = 8.0
LRU_CONV_LEFT = 2
HYENA_CONV_LEFT = 1
FILTER_BANDS = 16
FILTER_TARGET = 1e-2
FAST_DECAY_PCT = 0.3
SLOW_DECAY_PCT = 1.5

SUBLANES = 8
LANES = 128
NSEG = SUBLANES
BF16_ROWS = 16
V7X_VMEM_BYTES = 64 * 1024 * 1024
VMEM_LIMIT = V7X_VMEM_BYTES - 6 * 1024 * 1024

DFT_N2 = GRID_W
SPEC_PAD = 4
SPEC_PITCH = 2 * DFT_N2 + SPEC_PAD
SLAB_UNROLL = 32
FREQ_UNROLL = 43


def _cparams(sem):
    return pltpu.CompilerParams(dimension_semantics=sem, vmem_limit_bytes=VMEM_LIMIT)


def _const_spec(shape):
    nd = len(shape)
    return pl.BlockSpec(shape, lambda *_: (0,) * nd, pipeline_mode=pl.Buffered(1))


def _round_up(a, m):
    return (a + m - 1) // m * m


def _rms_norm(x, g):
    ms = jnp.mean(x * x, axis=-1, keepdims=True)
    return (x * lax.rsqrt(ms + NORM_EPS)) * g


def _modulate(x, g, shift, scale):
    return _rms_norm(x, g) * (1.0 + scale) + shift


def _gelu_tanh(x):
    c = math.sqrt(2.0 / math.pi)
    return x * (0.5 * (1.0 + jnp.tanh(c * (x + 0.044715 * (x * x * x)))))


def _softplus(x):
    return jnp.maximum(x, 0.0) + jnp.log1p(jnp.exp(-jnp.abs(x)))


def _bdot(a, b):
    return jnp.dot(a, b, preferred_element_type=F32)


def _order_pitch(length):
    return length + 4


def _order_scratch(groups, length, d):
    return pltpu.VMEM((d // LANES, groups * _order_pitch(length), LANES), F32)


def _to_tile_order(x, groups, scr):
    n, d = x.shape
    length = n // groups
    pitch = _order_pitch(length)
    for lt in range(d // LANES):
        for g in range(groups):
            scr[lt, g * pitch:g * pitch + length, :] = x[g * length:(g + 1) * length,
                                                         lt * LANES:(lt + 1) * LANES]
    rows = [jnp.concatenate([scr[lt, pl.ds(p, groups, stride=pitch), :] for lt in range(d // LANES)],
                            axis=1) for p in range(length)]
    return jnp.concatenate(rows, axis=0)


def _from_tile_order(x, groups, scr, o_ref):
    n, d = x.shape
    length = n // groups
    pitch = _order_pitch(length)
    for lt in range(d // LANES):
        ls = slice(lt * LANES, (lt + 1) * LANES)
        for p in range(length):
            scr[lt, pl.ds(p, groups, stride=pitch), :] = x[p * groups:(p + 1) * groups, ls]
        for g in range(groups):
            o_ref[0, g * length:(g + 1) * length, ls] = scr[lt, g * pitch:g * pitch + length, :]


def _wrapped_edge(edge, step):
    pieces = []
    for p in range(edge.shape[0] // SUBLANES):
        piece = edge[p * SUBLANES:(p + 1) * SUBLANES]
        sub = lax.broadcasted_iota(jnp.int32, piece.shape, 0)
        if step > 0:
            pieces.append(jnp.where(sub == 0, 0.0, pltpu.roll(piece, 1, 0)))
        else:
            pieces.append(jnp.where(sub == SUBLANES - 1, 0.0, pltpu.roll(piece, SUBLANES - 1, 0)))
    return jnp.concatenate(pieces, axis=0) if len(pieces) > 1 else pieces[0]


def _shift_tokens(z, o, groups, wrap):
    n = abs(o) * groups
    rows = z.shape[0]
    if o < 0:
        edge = _wrapped_edge(z[rows - n:], 1) if wrap else jnp.zeros((n, z.shape[1]), z.dtype)
        return jnp.concatenate([edge, z[:rows - n]], axis=0)
    edge = _wrapped_edge(z[:n], -1) if wrap else jnp.zeros((n, z.shape[1]), z.dtype)
    return jnp.concatenate([z[n:], edge], axis=0)


def _row_conv(z, w, b, left, groups, wrap=False):
    acc = b + w[left:left + 1] * z
    for k in range(w.shape[0]):
        if k != left:
            acc = acc + w[k:k + 1] * _shift_tokens(z, k - left, groups, wrap)
    return acc


def _ada_kernel(c_ref, w_ref, b_ref, o_ref):
    c = c_ref[...]
    cond = c * jax.nn.sigmoid(c)
    o_ref[0] = jnp.dot(cond, w_ref[0], preferred_element_type=F32, precision=HIGHEST) + b_ref[0]


def _ada_call(cvec, ada_w, ada_b):
    depth, d, n = ada_w.shape
    tn = 1536
    return pl.pallas_call(
        _ada_kernel,
        grid=(depth, n // tn),
        in_specs=[
            pl.BlockSpec((SUBLANES, d), lambda i, j: (0, 0)),
            pl.BlockSpec((1, d, tn), lambda i, j: (i, 0, j)),
            pl.BlockSpec((1, 1, tn), lambda i, j: (i, 0, j)),
        ],
        out_specs=pl.BlockSpec((1, SUBLANES, tn), lambda i, j: (i, 0, j)),
        out_shape=jax.ShapeDtypeStruct((depth, SUBLANES, n), F32),
        compiler_params=_cparams(("parallel", "parallel")),
    )(cvec, ada_w, ada_b.reshape(depth, 1, n))


def _lru_head_coeffs(xh, wg_h, bg_h, half_c_sp_h, ab_scr, cs):
    hb = xh.shape[1]
    gates = _bdot(xh.astype(BF16), wg_h) + bg_h
    xh_half = 0.5 * xh
    for e in range(2):
        tr = jnp.tanh(gates[:, (2 * e) * hb:(2 * e + 1) * hb])
        ti = jnp.tanh(gates[:, (2 * e + 1) * hb:(2 * e + 2) * hb])
        c = half_c_sp_h[e:e + 1]
        a = jnp.exp(c * tr + c)
        q = 1.0 - a * a
        wgt = jnp.where(q > 0.0, q * lax.rsqrt(q), 0.0) * xh_half
        ab_scr[e, 0, :, cs] = a
        ab_scr[e, 1, :, cs] = wgt * ti + wgt


def _lru_head_scan(ab_scr, agg_ref, cs, seg_len, keep):
    hb = cs.stop - cs.start
    pf = pb = jnp.ones((NSEG, hb), F32)
    hf = hbk = jnp.zeros((NSEG, hb), F32)
    for i in range(seg_len):
        rf = slice(i * NSEG, (i + 1) * NSEG)
        rb = slice((seg_len - 1 - i) * NSEG, (seg_len - i) * NSEG)
        af = ab_scr[0, 0, rf, cs]
        ab = ab_scr[1, 0, rb, cs]
        pf, hf = pf * af, af * hf + ab_scr[0, 1, rf, cs]
        pb, hbk = pb * ab, ab * hbk + ab_scr[1, 1, rb, cs]
        if keep:
            ab_scr[0, 0, rf, cs] = hf
            ab_scr[0, 1, rf, cs] = pf
            ab_scr[1, 0, rb, cs] = hbk
            ab_scr[1, 1, rb, cs] = pb
    agg_ref[0, 0, 0, :, cs] = pf
    agg_ref[0, 0, 1, :, cs] = hf
    agg_ref[0, 1, 0, :, cs] = pb
    agg_ref[0, 1, 1, :, cs] = hbk


def _lru_local_scan(u, wrec_ref, brec_ref, cw_ref, cb_ref, wg_ref, bg_ref, lam_ref, agg_ref, ab_scr,
                    *, wrap, keep, after_head=None):
    seg_len = u.shape[0] // NSEG
    w = ab_scr.shape[-1]
    hb = w // LRU_HEADS
    zr = _bdot(u, wrec_ref[...]) + brec_ref[...]
    xl = _row_conv(zr, cw_ref[...], cb_ref[...], LRU_CONV_LEFT, NSEG, wrap)
    half_c_sp = (-0.5 * LRU_C) * _softplus(-lam_ref[...])
    for h in range(LRU_HEADS):
        cs = slice(h * hb, (h + 1) * hb)
        _lru_head_coeffs(xl[:, cs], wg_ref[h], bg_ref[h], half_c_sp[:, cs], ab_scr, cs)
        _lru_head_scan(ab_scr, agg_ref, cs, seg_len, keep)
        if after_head is not None:
            after_head(cs)


def _lru_pass1_kernel(x_ref, mv_ref, ng_ref, wrec_ref, brec_ref, cw_ref, cb_ref, wg_ref, bg_ref,
                      lam_ref, agg_ref, ab_scr, order_scr, *, wrap):
    mv = mv_ref[0]
    xp = _to_tile_order(x_ref[0], NSEG, order_scr)
    u = _modulate(xp, ng_ref[...], mv[0:1], mv[1:2]).astype(BF16)
    _lru_local_scan(u, wrec_ref, brec_ref, cw_ref, cb_ref, wg_ref, bg_ref, lam_ref, agg_ref, ab_scr,
                    wrap=wrap, keep=False)


def _lru_mix_kernel(x_ref, mv_ref, ng_ref, wrec_ref, brec_ref, cw_ref, cb_ref, wg_ref, bg_ref,
                    lam_ref, wgate_ref, bgate_ref, agg_ref, pq_ref, xt_ref, ab_scr, order_scr):
    mv = mv_ref[0]
    xp = _to_tile_order(x_ref[0], NSEG, order_scr)
    xt_ref[0] = xp.reshape(xt_ref.shape[1:])
    u = _modulate(xp, ng_ref[...], mv[0:1], mv[1:2]).astype(BF16)

    def emit(cs):
        gate = _gelu_tanh(_bdot(u, wgate_ref[:, cs]) + bgate_ref[:, cs])
        pq_ref[0, 0, 0, :, cs] = ((ab_scr[0, 0, :, cs] + ab_scr[1, 0, :, cs]) * gate).astype(BF16)
        pq_ref[0, 0, 1, :, cs] = (ab_scr[0, 1, :, cs] * gate).astype(BF16)
        pq_ref[0, 0, 2, :, cs] = (ab_scr[1, 1, :, cs] * gate).astype(BF16)

    _lru_local_scan(u, wrec_ref, brec_ref, cw_ref, cb_ref, wg_ref, bg_ref, lam_ref, agg_ref, ab_scr,
                    wrap=False, keep=True, after_head=emit)


def _lru_weight_specs(d, w, cw, wg, bg, lam):
    return [
        _const_spec((1, d)),
        _const_spec((d, w)),
        _const_spec((1, w)),
        _const_spec(cw.shape),
        _const_spec((1, w)),
        _const_spec(wg.shape),
        _const_spec(bg.shape),
        _const_spec(lam.shape),
    ]


def _lru_pass1_call(x, mv, ng, wrec, brec, cw, cb, wg, bg, lam, *, tile, wrap):
    b, s, d = x.shape
    w = wrec.shape[1]
    nt = s // tile
    return pl.pallas_call(
        functools.partial(_lru_pass1_kernel, wrap=wrap),
        grid=(b, nt),
        in_specs=[
            pl.BlockSpec((1, tile, d), lambda i, j: (i, j, 0)),
            pl.BlockSpec((1, SUBLANES, d), lambda i, j: (i, 0, 0)),
        ] + _lru_weight_specs(d, w, cw, wg, bg, lam),
        out_specs=pl.BlockSpec((1, 2, 2, NSEG, w), lambda i, j: (i, 0, 0, j, 0)),
        out_shape=jax.ShapeDtypeStruct((b, 2, 2, nt * NSEG, w), F32),
        scratch_shapes=[pltpu.VMEM((2, 2, tile, w), F32), _order_scratch(NSEG, tile // NSEG, d)],
        compiler_params=_cparams(("parallel", "parallel")),
    )(x, mv, ng, wrec, brec, cw, cb, wg, bg, lam)


def _segscan_kernel(aggl_ref, aggc_ref, hin_ref):
    nsl = aggl_ref.shape[3]
    nsc = aggc_ref.shape[3]
    w = aggl_ref.shape[-1]
    for e in range(2):
        order_c = range(nsc) if e == 0 else range(nsc - 1, -1, -1)
        order_l = range(nsl) if e == 0 else range(nsl - 1, -1, -1)
        st = jnp.zeros((1, w), F32)
        for s in order_c:
            st = aggc_ref[0, e, 0, s:s + 1, :] * st + aggc_ref[0, e, 1, s:s + 1, :]
        for s in order_l:
            hin_ref[0, e, s:s + 1, :] = st
            st = aggl_ref[0, e, 0, s:s + 1, :] * st + aggl_ref[0, e, 1, s:s + 1, :]


def _segscan_call(agg_l, agg_c):
    b, _, _, nsl, w = agg_l.shape
    nsc = agg_c.shape[3]
    return pl.pallas_call(
        _segscan_kernel,
        grid=(b,),
        in_specs=[
            pl.BlockSpec((1, 2, 2, nsl, w), lambda i: (i, 0, 0, 0, 0)),
            pl.BlockSpec((1, 2, 2, nsc, w), lambda i: (i, 0, 0, 0, 0)),
        ],
        out_specs=pl.BlockSpec((1, 2, nsl, w), lambda i: (i, 0, 0, 0)),
        out_shape=jax.ShapeDtypeStruct((b, 2, nsl, w), F32),
        compiler_params=_cparams(("parallel",)),
    )(agg_l, agg_c)


def _lru_mix_call(x, mv, ng, wrec, brec, cw, cb, wg, bg, lam, wgate, bgate, *, tile):
    b, s, d = x.shape
    w = wrec.shape[1]
    nt = s // tile
    return pl.pallas_call(
        _lru_mix_kernel,
        grid=(b, nt),
        in_specs=[
            pl.BlockSpec((1, tile, d), lambda i, j: (i, j, 0)),
            pl.BlockSpec((1, SUBLANES, d), lambda i, j: (i, 0, 0)),
        ] + _lru_weight_specs(d, w, cw, wg, bg, lam) + [
            _const_spec((d, w)),
            _const_spec((1, w)),
        ],
        out_specs=[
            pl.BlockSpec((1, 2, 2, NSEG, w), lambda i, j: (i, 0, 0, j, 0)),
            pl.BlockSpec((1, 1, 3, tile, w), lambda i, j: (i, j, 0, 0, 0)),
            pl.BlockSpec((1, GRID_W, NSEG, d), lambda i, j: (i, 0, j, 0)),
        ],
        out_shape=[
            jax.ShapeDtypeStruct((b, 2, 2, nt * NSEG, w), F32),
            jax.ShapeDtypeStruct((b, nt, 3, tile, w), BF16),
            jax.ShapeDtypeStruct((b, GRID_W, s // GRID_W, d), F32),
        ],
        scratch_shapes=[pltpu.VMEM((2, 2, tile, w), F32), _order_scratch(NSEG, tile // NSEG, d)],
        compiler_params=_cparams(("parallel", "parallel")),
    )(x, mv, ng, wrec, brec, cw, cb, wg, bg, lam, wgate, bgate)


FF_CHUNK = 1024


def _mlp_kernel(*refs, pre, final):
    refs = list(refs)
    x_ref = refs.pop(0)
    mv_ref = refs.pop(0)
    ng_ref = refs.pop(0)
    w1_ref = refs.pop(0)
    w2_ref = refs.pop(0)
    if pre == "hyena":
        v_ref = refs.pop(0)
    if pre == "lru":
        pq_ref = refs.pop(0)
        hin_ref = refs.pop(0)
    if pre:
        wout_ref = refs.pop(0)
        bout_ref = refs.pop(0)
    if final:
        fg_ref = refs.pop(0)
    o_ref = refs.pop(0)
    if final:
        order_scr = refs.pop(0)
    mv = mv_ref[0]
    groups = x_ref.shape[2]
    x = x_ref[0].reshape(GRID_W * groups, x_ref.shape[3])
    if pre == "lru":
        rows, w = pq_ref.shape[-2:]

        def times_entering(plane, e):
            running = pq_ref[0, 0, plane].astype(F32).reshape(rows // NSEG, NSEG, w)
            return (running * hin_ref[0, e][None]).reshape(rows, w)

        mixed = pq_ref[0, 0, 0].astype(F32) + times_entering(1, 0) + times_entering(2, 1)
        x = x + mv[2:3] * (_bdot(mixed.astype(BF16), wout_ref[...]) + bout_ref[...])
    if pre == "hyena":
        x = x + mv[2:3] * (_bdot(_load_lane_tiles(v_ref, 0), wout_ref[...]) + bout_ref[...])
    u = _modulate(x, ng_ref[...], mv[3:4], mv[4:5]).astype(BF16)
    acc = jnp.zeros(x.shape, F32)
    for c in range(w1_ref.shape[1] // FF_CHUNK):
        cs = slice(c * FF_CHUNK, (c + 1) * FF_CHUNK)
        h = jnp.maximum(_bdot(u, w1_ref[:, cs]), 0.0)
        acc = acc + _bdot((h * h).astype(BF16), w2_ref[cs, :])
    out = x + mv[5:6] * acc
    if final:
        _from_tile_order(_rms_norm(out, fg_ref[...]), groups, order_scr, o_ref)
    else:
        o_ref[0] = out.reshape(o_ref.shape[1:])


def _mlp_call(x, mv, ng, w1, w2, *, groups, hyena=None, lru=None, final_g=None):
    f = w1.shape[1]
    b, _, rows, d = x.shape
    tile_spec = pl.BlockSpec((1, GRID_W, groups, d), lambda i, j: (i, 0, j, 0))
    args = [x, mv, ng, w1, w2]
    in_specs = [
        tile_spec,
        pl.BlockSpec((1, SUBLANES, d), lambda i, j: (i, 0, 0)),
        _const_spec((1, d)),
        _const_spec((d, f)),
        _const_spec((f, d)),
    ]
    pre = None
    if hyena is not None:
        pre = "hyena"
        v, wout, bout = hyena
        args += [v, wout, bout]
        in_specs += [_lane_tiled_spec(d, groups), _const_spec(wout.shape), _const_spec((1, d))]
    if lru is not None:
        pre = "lru"
        pq, hin, wout, bout = lru
        args += [pq, hin, wout, bout]
        in_specs += [
            pl.BlockSpec((1, 1) + pq.shape[2:], lambda i, j: (i, j, 0, 0, 0)),
            pl.BlockSpec((1, 2, NSEG, hin.shape[-1]), lambda i, j: (i, 0, j, 0)),
            _const_spec(wout.shape),
            _const_spec((1, d)),
        ]
    if final_g is not None:
        args.append(final_g)
        in_specs.append(_const_spec((1, d)))
        out_spec = pl.BlockSpec((1, GRID_W * groups, d), lambda i, j: (i, j, 0))
        out_shape = jax.ShapeDtypeStruct((b, GRID_W * rows, d), F32)
    else:
        out_spec = tile_spec
        out_shape = jax.ShapeDtypeStruct((b, GRID_W, rows, d), F32)
    return pl.pallas_call(
        functools.partial(_mlp_kernel, pre=pre, final=final_g is not None),
        grid=(b, rows // groups),
        in_specs=in_specs,
        out_specs=out_spec,
        out_shape=out_shape,
        scratch_shapes=[_order_scratch(groups, GRID_W, d)] if final_g is not None else [],
        compiler_params=_cparams(("parallel", "parallel")),
    )(*args)


def _hyproj_kernel(x_ref, mv_ref, ng_ref, win_ref, bin_ref, cw_ref, cb_ref, v_ref, xa_ref, xb_ref):
    mv = mv_ref[0]
    groups = x_ref.shape[2]
    d = x_ref.shape[3]
    x = x_ref[0].reshape(GRID_W * groups, d)
    u = _modulate(x, ng_ref[...], mv[0:1], mv[1:2]).astype(BF16)
    for k, o_ref in enumerate((v_ref, xa_ref, xb_ref)):
        cs = slice(k * d, (k + 1) * d)
        z = _bdot(u, win_ref[:, cs]) + bin_ref[:, cs]
        z = _row_conv(z, cw_ref[:, cs], cb_ref[:, cs], HYENA_CONV_LEFT, groups).astype(BF16)
        _store_lane_tiles(o_ref, 0, z, groups)


def _store_lane_tiles(o_ref, lead, val, groups):
    for lt in range(val.shape[1] // LANES):
        o_ref[lead, lt] = val[:, lt * LANES:(lt + 1) * LANES].reshape(-1, groups, LANES)


def _load_lane_tiles(ref, lead):
    nlt, t2, groups, _ = ref.shape[1:]
    return jnp.concatenate([ref[lead, lt].reshape(t2 * groups, LANES) for lt in range(nlt)], axis=1)


def _lane_tiled_spec(d, groups):
    return pl.BlockSpec((1, d // LANES, GRID_W, groups, LANES), lambda i, j: (i, 0, 0, j, 0))


def _hyproj_call(x, mv, ng, win, bin_, cw, cb, *, groups):
    b, _, rows, d = x.shape
    tile_spec = pl.BlockSpec((1, GRID_W, groups, d), lambda i, j: (i, 0, j, 0))
    out_spec = _lane_tiled_spec(d, groups)
    out_sds = jax.ShapeDtypeStruct((b, d // LANES, GRID_W, rows, LANES), BF16)
    return pl.pallas_call(
        _hyproj_kernel,
        grid=(b, rows // groups),
        in_specs=[
            tile_spec,
            pl.BlockSpec((1, SUBLANES, d), lambda i, j: (i, 0, 0)),
            _const_spec((1, d)),
            _const_spec(win.shape),
            _const_spec(bin_.shape),
            _const_spec(cw.shape),
            _const_spec(cb.shape),
        ],
        out_specs=[out_spec, out_spec, out_spec],
        out_shape=[out_sds, out_sds, out_sds],
        compiler_params=_cparams(("parallel", "parallel")),
    )(x, mv, ng, win, bin_, cw, cb)


def _filter_kernel(pos_ref, fw1_ref, fb1_ref, fw2_ref, fb2_ref, fw3_ref, fb3_ref, fw4_ref, freq_ref,
                   deltas_ref, h_ref, nrm_ref):
    groups = pos_ref.shape[1]
    pe = pos_ref.shape[2]
    pos = pos_ref[...].reshape(GRID_W * groups, pe)
    half_rows = pos.shape[0] // 2
    half_t2 = GRID_W // 2
    pos2 = jnp.concatenate([pos[:half_rows], pos[half_rows:]], axis=1)
    freq = freq_ref[...]

    def hdot(a, b):
        return jnp.dot(a, b, preferred_element_type=F32, precision=HIGHEST)

    h = jnp.sin(freq * (hdot(pos2, fw1_ref[...]) + fb1_ref[...]))
    h = jnp.sin(freq * (hdot(h, fw2_ref[...]) + fb2_ref[...]))
    h = jnp.sin(freq * (hdot(h, fw3_ref[...]) + fb3_ref[...])).astype(BF16)
    d = deltas_ref.shape[1]
    n4 = fw4_ref.shape[1] // 2
    nparts = n4 // d
    sums = [jnp.zeros((1, d), F32)] * nparts
    for s in range(2):
        decay = jnp.exp(-pos2[:, s * pe:s * pe + 1] * deltas_ref[...])
        t2s = slice(s * half_t2, (s + 1) * half_t2)
        for p in range(nparts):
            cs = slice(p * d, (p + 1) * d)
            hp = _bdot(h, fw4_ref[:, s * n4 + p * d:s * n4 + (p + 1) * d]) * decay
            sums[p] = sums[p] + jnp.sum(jnp.abs(hp), axis=0, keepdims=True)
            hp16 = hp.astype(BF16)
            for lt in range(d // LANES):
                h_ref[p * (d // LANES) + lt, t2s] = (
                    hp16[:, lt * LANES:(lt + 1) * LANES].reshape(half_t2, groups, LANES))
    half = nparts // 2
    tot = jnp.concatenate([sums[p] + sums[p + half] for p in range(half)], axis=1)

    @pl.when(pl.program_id(0) == 0)
    def _():
        nrm_ref[...] = jnp.zeros_like(nrm_ref)

    nrm_ref[...] += tot


def _filter_call(pos, fw1, fb1, fw2, fb2, fw3, fb3, fw4, freq, deltas, *, groups):
    _, rows, pe = pos.shape
    fh = fw2.shape[0]
    n4 = fw4.shape[1] // 2
    d = deltas.shape[1]
    return pl.pallas_call(
        _filter_kernel,
        grid=(rows // groups,),
        in_specs=[
            pl.BlockSpec((GRID_W, groups, pe), lambda j: (0, j, 0)),
            _const_spec(fw1.shape), _const_spec((1, fh)),
            _const_spec((fh, fh)), _const_spec((1, fh)),
            _const_spec((fh, fh)), _const_spec((1, fh)),
            _const_spec(fw4.shape), _const_spec((1, fh)),
            _const_spec((1, d)),
        ],
        out_specs=[
            pl.BlockSpec((n4 // LANES, GRID_W, groups, LANES), lambda j: (0, 0, j, 0)),
            pl.BlockSpec((1, n4 // 2), lambda j: (0, 0)),
        ],
        out_shape=[
            jax.ShapeDtypeStruct((n4 // LANES, GRID_W, rows, LANES), BF16),
            jax.ShapeDtypeStruct((1, n4 // 2), F32),
        ],
        compiler_params=_cparams(("arbitrary",)),
    )(pos, fw1, fb1, fw2, fb2, fw3, fb3, fw4, freq, deltas)


@functools.lru_cache(maxsize=None)
def _dft_constants(seq_len):
    n = 2 * seq_len
    n2 = DFT_N2
    n1 = n // n2
    nt1 = n1 // 2
    nf = n1 // 2 + 1
    slots = _round_up(nf, SUBLANES)
    t1 = np.arange(nt1)[None, :]
    f1 = np.arange(slots)[:, None]
    live = (f1 < nf).astype(np.float64)
    ang1 = 2.0 * np.pi * (t1 * f1 % n1) / n1
    cos1, sin1 = np.cos(ang1) * live, np.sin(ang1) * live
    cf = np.full((slots, 1), 2.0)
    cf[0] = 1.0
    cf[nf - 1] = 1.0
    f1h = np.concatenate([cos1, -sin1], axis=0)
    f1i = np.concatenate([cos1 * cf, -sin1 * cf], axis=0).T
    t2 = np.arange(n2)[None, None, :]
    f2 = np.arange(n2)[None, :, None]
    ff1 = np.arange(nf)[:, None, None]
    ang2 = 2.0 * np.pi * ((t2 * (ff1 + n1 * f2)) % n) / n
    gr, gim = np.cos(ang2), -np.sin(ang2)
    g = np.concatenate([np.concatenate([gr, -gim], axis=2),
                        np.concatenate([gim, gr], axis=2)], axis=1)
    as32 = lambda a: np.ascontiguousarray(a, dtype=np.float32)
    return as32(f1h), as32(g), as32(f1i), nt1, nf, slots


def _spec_rows(t2, slots):
    return pl.ds(t2, slots, stride=SPEC_PITCH)


def _dft_stage1(load_slab, f1h_ref, spec_scr, slots):
    def body(t2, carry):
        a = _bdot(f1h_ref[...], load_slab(t2))
        for lt in range(2):
            ls = slice(lt * LANES, (lt + 1) * LANES)
            spec_scr[lt, _spec_rows(t2, slots), :] = a[:slots, ls]
            spec_scr[lt, _spec_rows(t2 + DFT_N2, slots), :] = a[slots:, ls]
        return carry

    lax.fori_loop(0, DFT_N2, body, 0, unroll=SLAB_UNROLL)


def _spec_slot_load(spec_scr, slot):
    rows = pl.ds(slot * SPEC_PITCH, 2 * DFT_N2)
    return jnp.concatenate([spec_scr[0, rows, :], spec_scr[1, rows, :]], axis=1), rows


def _filtfft_kernel(hf_ref, hb_ref, nrm_ref, f1h_ref, g_ref, k_ref, spec_scr, *, nf, slots):
    inv = 1.0 / nrm_ref[...]

    def slab(t2):
        return jnp.concatenate([hf_ref[0, t2], hb_ref[0, t2]], axis=1)

    _dft_stage1(slab, f1h_ref, spec_scr, slots)

    def body(f1, carry):
        a, _ = _spec_slot_load(spec_scr, f1)
        xs = _bdot(g_ref[f1], a.astype(BF16))
        fwd, bwd = xs[:, :LANES], xs[:, LANES:]
        k_ref[0, 0, f1] = (jnp.concatenate(
            [fwd[:DFT_N2] + bwd[:DFT_N2], fwd[DFT_N2:] - bwd[DFT_N2:]], axis=0) * inv).astype(BF16)
        return carry

    lax.fori_loop(0, nf, body, 0, unroll=FREQ_UNROLL)


def _filtfft_call(hraw, nrm, seq_len):
    f1h, g, _, nt1, nf, slots = _dft_constants(seq_len)
    nlt, _, rows, _ = hraw.shape
    nct = nlt // 4
    return pl.pallas_call(
        functools.partial(_filtfft_kernel, nf=nf, slots=slots),
        grid=(2, nct),
        in_specs=[
            pl.BlockSpec((1, GRID_W, rows, LANES), lambda o, c: (o * nct + c, 0, 0, 0)),
            pl.BlockSpec((1, GRID_W, rows, LANES), lambda o, c: (2 * nct + o * nct + c, 0, 0, 0)),
            pl.BlockSpec((1, LANES), lambda o, c: (0, o * nct + c)),
            _const_spec(f1h.shape),
            _const_spec(g.shape),
        ],
        out_specs=pl.BlockSpec((1, 1, nf, 2 * DFT_N2, LANES), lambda o, c: (o, c, 0, 0, 0)),
        out_shape=jax.ShapeDtypeStruct((2, nct, nf, 2 * DFT_N2, LANES), BF16),
        scratch_shapes=[pltpu.VMEM((2, slots * SPEC_PITCH, LANES), F32)],
        compiler_params=_cparams(("parallel", "parallel")),
    )(hraw, hraw, nrm, jnp.asarray(f1h).astype(BF16), jnp.asarray(g).astype(BF16))


def _longconv_kernel(v_ref, m_ref, k_ref, skip_ref, f1h_ref, g_ref, f1i_ref, o_ref,
                     spec_scr, *, nf, slots):
    def slab(t2):
        return jnp.concatenate([v_ref[0, 0, t2], v_ref[1, 0, t2]], axis=1)

    _dft_stage1(slab, f1h_ref, spec_scr, slots)

    def mid(f1, carry):
        a, rows = _spec_slot_load(spec_scr, f1)
        xs = _bdot(g_ref[f1], a.astype(BF16))
        kf = k_ref[0, 0, f1].astype(F32)
        kr = jnp.concatenate([kf[:DFT_N2]] * 2, axis=1)
        ki = jnp.concatenate([kf[DFT_N2:]] * 2, axis=1)
        xr, xi = xs[:DFT_N2], xs[DFT_N2:]
        ys = jnp.concatenate([xr * kr - xi * ki, xr * ki + xi * kr], axis=0).astype(BF16)
        bs = lax.dot_general(g_ref[f1], ys, (((0,), (0,)), ((), ())), preferred_element_type=F32)
        spec_scr[0, rows, :] = bs[:, :LANES]
        spec_scr[1, rows, :] = bs[:, LANES:]
        return carry

    lax.fori_loop(0, nf, mid, 0, unroll=FREQ_UNROLL)
    skip = skip_ref[...]

    def last(t2, carry):
        halves = []
        for lt in range(2):
            re = spec_scr[lt, _spec_rows(t2, slots), :]
            im = spec_scr[lt, _spec_rows(t2 + DFT_N2, slots), :]
            halves.append(jnp.concatenate([re, im], axis=0))
        y = _bdot(f1i_ref[...], jnp.concatenate(halves, axis=1).astype(BF16))
        for b in range(2):
            conv = y[:, b * LANES:(b + 1) * LANES]
            vs = v_ref[b, 0, t2].astype(F32)
            o_ref[b, 0, t2] = (m_ref[b, 0, t2].astype(F32) * (conv + vs * skip)).astype(BF16)
        return carry

    lax.fori_loop(0, DFT_N2, last, 0, unroll=SLAB_UNROLL)


def _longconv_call(v, m, kf, order, skip, seq_len):
    f1h, g, f1i, nt1, nf, slots = _dft_constants(seq_len)
    b, nct, _, rows, _ = v.shape
    seq_spec = pl.BlockSpec((2, 1, GRID_W, rows, LANES), lambda c, i: (i, c, 0, 0, 0))
    scale = 1.0 / (2 * seq_len)
    return pl.pallas_call(
        functools.partial(_longconv_kernel, nf=nf, slots=slots),
        grid=(nct, b // 2),
        in_specs=[
            seq_spec,
            seq_spec,
            pl.BlockSpec((1, 1, nf, 2 * DFT_N2, LANES), lambda c, i: (order, c, 0, 0, 0)),
            pl.BlockSpec((1, LANES), lambda c, i: (0, c)),
            _const_spec(f1h.shape),
            _const_spec(g.shape),
            _const_spec(f1i.shape),
        ],
        out_specs=seq_spec,
        out_shape=jax.ShapeDtypeStruct(v.shape, BF16),
        scratch_shapes=[pltpu.VMEM((2, slots * SPEC_PITCH, LANES), F32)],
        compiler_params=_cparams(("parallel", "parallel")),
    )(v, m, kf, skip, jnp.asarray(f1h).astype(BF16), jnp.asarray(g).astype(BF16),
      jnp.asarray(f1i * scale).astype(BF16))


def _mod_rows(mod_layer, nb, d):
    m = mod_layer.reshape(SUBLANES, 6, d)
    m = jnp.concatenate([m, jnp.zeros((SUBLANES, SUBLANES - 6, d), F32)], axis=1)
    return m[:nb], jnp.broadcast_to(m[nb:nb + 1], (nb, SUBLANES, d))


@functools.lru_cache(maxsize=None)
def _filter_constants(seq_len, d):
    t = np.linspace(0.0, 1.0, seq_len)[:, None]
    w = (2.0 * np.pi / seq_len) * np.arange(seq_len)[:, None]
    bands = np.linspace(1e-4, FILTER_BANDS - 1, FILTER_BANDS)
    pos = np.concatenate([t, np.cos(bands * w), -np.sin(bands * w)], axis=-1)
    pe = _round_up(pos.shape[1], LANES)
    pos = np.pad(pos, ((0, 0), (0, pe - pos.shape[1])))
    pos = pos.reshape(seq_len // GRID_W, GRID_W, pe).transpose(1, 0, 2)
    deltas = np.abs(np.linspace(math.log(FILTER_TARGET) / SLOW_DECAY_PCT,
                                math.log(FILTER_TARGET) / FAST_DECAY_PCT, d))[None, :]
    return np.ascontiguousarray(pos, dtype=np.float32), np.ascontiguousarray(deltas, dtype=np.float32)


def kernel(x, c, ctx, c_ctx, ada_w, ada_b, norm_g, mlp_w1, mlp_w2, lru_w_in, lru_b_in, lru_conv_w, lru_conv_b, lru_w_a, lru_b_a, lru_w_i, lru_b_i, lru_lambda, lru_w_out, lru_b_out, hy_w_in, hy_b_in, hy_conv_w, hy_conv_b, hy_fw1, hy_fb1, hy_fw2, hy_fb2, hy_fw3, hy_fb3, hy_fw4, hy_freq, hy_skip, hy_w_out, hy_b_out, final_g):
    nb, seq, d = x.shape
    ctx_len = ctx.shape[1]
    w = lru_w_out.shape[1]
    lru_tile = NSEG * GRID_W
    assert nb + 1 <= SUBLANES and nb % 2 == 0
    assert seq % (BF16_ROWS * GRID_W) == 0 and ctx_len % (NSEG * SUBLANES) == 0

    cvec = jnp.concatenate([c, c_ctx[None, :], jnp.zeros((SUBLANES - nb - 1, d), F32)], axis=0)
    mod = _ada_call(cvec, ada_w, ada_b)

    mv_l, mv_c = _mod_rows(mod[0], nb, d)
    ng = norm_g[0, 0][None, :]
    wgate = lru_w_in[0, :, :w].astype(BF16)
    wrec = lru_w_in[0, :, w:].astype(BF16)
    bgate = lru_b_in[0, :w][None, :]
    brec = lru_b_in[0, w:][None, :]
    cw = lru_conv_w[0]
    cb = lru_conv_b[0][None, :]
    wg = (0.5 * jnp.concatenate([lru_w_a[0, 0], lru_w_i[0, 0], lru_w_a[0, 1], lru_w_i[0, 1]], axis=-1)).astype(BF16)
    bg = 0.5 * jnp.concatenate([lru_b_a[0, 0], lru_b_i[0, 0], lru_b_a[0, 1], lru_b_i[0, 1]], axis=-1)[:, None, :]
    lam = lru_lambda[0]
    lru_w = (ng, wrec, brec, cw, cb, wg, bg, lam)
    agg_c = _lru_pass1_call(ctx, mv_c, *lru_w, tile=ctx_len, wrap=True)
    agg_l, pq, xt = _lru_mix_call(x, mv_l, *lru_w, wgate, bgate, tile=lru_tile)
    hin = _segscan_call(agg_l, agg_c)
    x2 = _mlp_call(xt, mv_l, norm_g[0, 1][None, :], mlp_w1[0].astype(BF16), mlp_w2[0].astype(BF16), groups=NSEG,
                   lru=(pq, hin, lru_w_out[0].astype(BF16), lru_b_out[0][None, :]))

    mv1, _ = _mod_rows(mod[1], nb, d)
    v, xa, xb = _hyproj_call(x2, mv1, norm_g[1, 0][None, :], hy_w_in[0].astype(BF16), hy_b_in[0][None, :],
                             hy_conv_w[0], hy_conv_b[0][None, :], groups=BF16_ROWS)
    pos, deltas = (jnp.asarray(a) for a in _filter_constants(seq, d))
    fw1 = jnp.pad(hy_fw1[0], ((0, pos.shape[2] - hy_fw1.shape[1]), (0, 0)))
    twin = lambda wmat: jnp.kron(jnp.eye(2, dtype=F32), wmat)
    twice = lambda vec: jnp.tile(vec[None, :], (1, 2))
    hraw, nrm = _filter_call(pos, twin(fw1), twice(hy_fb1[0]), twin(hy_fw2[0]), twice(hy_fb2[0]),
                             twin(hy_fw3[0]), twice(hy_fb3[0]), twin(hy_fw4[0]).astype(BF16),
                             twice(hy_freq[0]), deltas, groups=BF16_ROWS)
    kf = _filtfft_call(hraw, nrm, seq)
    v1 = _longconv_call(v, xa, kf, 0, hy_skip[0, 0][None, :], seq)
    v2 = _longconv_call(v1, xb, kf, 1, hy_skip[0, 1][None, :], seq)
    return _mlp_call(x2, mv1, norm_g[1, 1][None, :], mlp_w1[1].astype(BF16), mlp_w2[1].astype(BF16),
                     groups=BF16_ROWS, hyena=(v2, hy_w_out[0].astype(BF16), hy_b_out[0][None, :]),
                     final_g=final_g[None, :])
```

```python
import functools
import math

import numpy as np
import jax
import jax.numpy as jnp
from jax import lax
from jax.experimental import pallas as pl
from jax.experimental.pallas import tpu as pltpu

F32 = jnp.float32
BF16 = jnp.bfloat16
HIGHEST = lax.Precision.HIGHEST

NORM_EPS = 1e-6
GRID_W = 64
LRU_HEADS = 4
LRU_C = 8.0
LRU_CONV_LEFT = 2
HYENA_CONV_LEFT = 1
FILTER_BANDS = 16
FILTER_TARGET = 1e-2
FAST_DECAY_PCT = 0.3
SLOW_DECAY_PCT = 1.5

SUBLANES = 8
LANES = 128
NSEG = SUBLANES
BF16_ROWS = 16
V7X_VMEM_BYTES = 64 * 1024 * 1024
VMEM_LIMIT = V7X_VMEM_BYTES - 6 * 1024 * 1024

DFT_N2 = GRID_W
SPEC_PAD = 4
SPEC_PITCH = 2 * DFT_N2 + SPEC_PAD
SLAB_UNROLL = 64
FREQ_UNROLL = 43


def _cparams(sem):
    return pltpu.CompilerParams(dimension_semantics=sem, vmem_limit_bytes=VMEM_LIMIT)


def _const_spec(shape):
    nd = len(shape)
    return pl.BlockSpec(shape, lambda *_: (0,) * nd, pipeline_mode=pl.Buffered(1))


def _round_up(a, m):
    return (a + m - 1) // m * m


def _rms_norm(x, g):
    ms = jnp.mean(x * x, axis=-1, keepdims=True)
    return (x * lax.rsqrt(ms + NORM_EPS)) * g


def _modulate(x, g, shift, scale):
    return _rms_norm(x, g * (1.0 + scale)) + shift


def _gelu_tanh(x):
    c = math.sqrt(2.0 / math.pi)
    return x * (0.5 * (1.0 + jnp.tanh(c * (x + 0.044715 * (x * x * x)))))


def _softplus(x):
    return jnp.maximum(x, 0.0) + jnp.log1p(jnp.exp(-jnp.abs(x)))


def _bdot(a, b):
    return jnp.dot(a, b, preferred_element_type=F32)


def _order_pitch(length):
    return length + 4


def _order_scratch(groups, length, d):
    return pltpu.VMEM((d // LANES, groups * _order_pitch(length), LANES), F32)


def _to_tile_order(x, groups, scr):
    n, d = x.shape
    length = n // groups
    pitch = _order_pitch(length)
    for lt in range(d // LANES):
        for g in range(groups):
            scr[lt, g * pitch:g * pitch + length, :] = x[g * length:(g + 1) * length,
                                                         lt * LANES:(lt + 1) * LANES]
    rows = [jnp.concatenate([scr[lt, pl.ds(p, groups, stride=pitch), :] for lt in range(d // LANES)],
                            axis=1) for p in range(length)]
    return jnp.concatenate(rows, axis=0)


def _from_tile_order(x, groups, scr, o_ref):
    n, d = x.shape
    length = n // groups
    pitch = _order_pitch(length)
    for lt in range(d // LANES):
        ls = slice(lt * LANES, (lt + 1) * LANES)
        for p in range(length):
            scr[lt, pl.ds(p, groups, stride=pitch), :] = x[p * groups:(p + 1) * groups, ls]
        for g in range(groups):
            o_ref[0, g * length:(g + 1) * length, ls] = scr[lt, g * pitch:g * pitch + length, :]


def _wrapped_edge(edge, step):
    pieces = []
    for p in range(edge.shape[0] // SUBLANES):
        piece = edge[p * SUBLANES:(p + 1) * SUBLANES]
        sub = lax.broadcasted_iota(jnp.int32, piece.shape, 0)
        if step > 0:
            pieces.append(jnp.where(sub == 0, 0.0, pltpu.roll(piece, 1, 0)))
        else:
            pieces.append(jnp.where(sub == SUBLANES - 1, 0.0, pltpu.roll(piece, SUBLANES - 1, 0)))
    return jnp.concatenate(pieces, axis=0) if len(pieces) > 1 else pieces[0]


def _shift_tokens(z, o, groups, wrap):
    n = abs(o) * groups
    rows = z.shape[0]
    if o < 0:
        edge = _wrapped_edge(z[rows - n:], 1) if wrap else jnp.zeros((n, z.shape[1]), z.dtype)
        return jnp.concatenate([edge, z[:rows - n]], axis=0)
    edge = _wrapped_edge(z[:n], -1) if wrap else jnp.zeros((n, z.shape[1]), z.dtype)
    return jnp.concatenate([z[n:], edge], axis=0)


def _row_conv(z, w, b, left, groups, wrap=False):
    acc = b + w[left:left + 1] * z
    for k in range(w.shape[0]):
        if k != left:
            acc = acc + w[k:k + 1] * _shift_tokens(z, k - left, groups, wrap)
    return acc


def _ada_kernel(c_ref, w_ref, b_ref, o_ref):
    c = c_ref[...]
    cond = c * jax.nn.sigmoid(c)
    o_ref[0] = jnp.dot(cond, w_ref[0], preferred_element_type=F32, precision=HIGHEST) + b_ref[0]


def _ada_call(cvec, ada_w, ada_b):
    depth, d, n = ada_w.shape
    tn = 1536
    return pl.pallas_call(
        _ada_kernel,
        grid=(depth, n // tn),
        in_specs=[
            pl.BlockSpec((SUBLANES, d), lambda i, j: (0, 0)),
            pl.BlockSpec((1, d, tn), lambda i, j: (i, 0, j)),
            pl.BlockSpec((1, 1, tn), lambda i, j: (i, 0, j)),
        ],
        out_specs=pl.BlockSpec((1, SUBLANES, tn), lambda i, j: (i, 0, j)),
        out_shape=jax.ShapeDtypeStruct((depth, SUBLANES, n), F32),
        compiler_params=_cparams(("parallel", "parallel")),
    )(cvec, ada_w, ada_b.reshape(depth, 1, n))


def _lru_head_coeffs(xh, wg_h, bg_h, half_c_sp_h, ab_scr, cs):
    hb = xh.shape[1]
    gates = _bdot(xh.astype(BF16), wg_h) + bg_h
    xh_half = 0.5 * xh
    for e in range(2):
        tr = jnp.tanh(gates[:, (2 * e) * hb:(2 * e + 1) * hb])
        ti = jnp.tanh(gates[:, (2 * e + 1) * hb:(2 * e + 2) * hb])
        c = half_c_sp_h[e:e + 1]
        a = jnp.exp(c * tr + c)
        q = 1.0 - a * a
        wgt = jnp.where(q > 0.0, q * lax.rsqrt(q), 0.0) * xh_half
        ab_scr[e, 0, :, cs] = a
        ab_scr[e, 1, :, cs] = wgt * ti + wgt


def _lru_head_scan(ab_scr, agg_ref, cs, seg_len, keep):
    hb = cs.stop - cs.start
    pf = pb = jnp.ones((NSEG, hb), F32)
    hf = hbk = jnp.zeros((NSEG, hb), F32)
    for i in range(seg_len):
        rf = slice(i * NSEG, (i + 1) * NSEG)
        rb = slice((seg_len - 1 - i) * NSEG, (seg_len - i) * NSEG)
        af = ab_scr[0, 0, rf, cs]
        ab = ab_scr[1, 0, rb, cs]
        pf, hf = pf * af, af * hf + ab_scr[0, 1, rf, cs]
        pb, hbk = pb * ab, ab * hbk + ab_scr[1, 1, rb, cs]
        if keep:
            ab_scr[0, 0, rf, cs] = hf
            ab_scr[0, 1, rf, cs] = pf
            ab_scr[1, 0, rb, cs] = hbk
            ab_scr[1, 1, rb, cs] = pb
    agg_ref[0, 0, 0, :, cs] = pf
    agg_ref[0, 0, 1, :, cs] = hf
    agg_ref[0, 1, 0, :, cs] = pb
    agg_ref[0, 1, 1, :, cs] = hbk


def _lru_local_scan(u, wrec_ref, brec_ref, cw_ref, cb_ref, wg_ref, bg_ref, lam_ref, agg_ref, ab_scr,
                    *, wrap, keep, after_head=None):
    seg_len = u.shape[0] // NSEG
    w = ab_scr.shape[-1]
    hb = w // LRU_HEADS
    zr = _bdot(u, wrec_ref[...]) + brec_ref[...]
    xl = _row_conv(zr, cw_ref[...], cb_ref[...], LRU_CONV_LEFT, NSEG, wrap)
    half_c_sp = (-0.5 * LRU_C) * _softplus(-lam_ref[...])
    for h in range(LRU_HEADS):
        cs = slice(h * hb, (h + 1) * hb)
        _lru_head_coeffs(xl[:, cs], wg_ref[h], bg_ref[h], half_c_sp[:, cs], ab_scr, cs)
        _lru_head_scan(ab_scr, agg_ref, cs, seg_len, keep)
        if after_head is not None:
            after_head(cs)


def _lru_pass1_kernel(x_ref, mv_ref, ng_ref, wrec_ref, brec_ref, cw_ref, cb_ref, wg_ref, bg_ref,
                      lam_ref, agg_ref, ab_scr, order_scr, *, wrap):
    mv = mv_ref[0]
    xp = _to_tile_order(x_ref[0], NSEG, order_scr)
    u = _modulate(xp, ng_ref[...], mv[0:1], mv[1:2]).astype(BF16)
    _lru_local_scan(u, wrec_ref, brec_ref, cw_ref, cb_ref, wg_ref, bg_ref, lam_ref, agg_ref, ab_scr,
                    wrap=wrap, keep=False)


def _lru_mix_kernel(x_ref, mv_ref, ng_ref, wrec_ref, brec_ref, cw_ref, cb_ref, wg_ref, bg_ref,
                    lam_ref, wgate_ref, bgate_ref, agg_ref, pq_ref, xt_ref, ab_scr, order_scr):
    mv = mv_ref[0]
    xp = _to_tile_order(x_ref[0], NSEG, order_scr)
    xt_ref[0] = xp.reshape(xt_ref.shape[1:])
    u = _modulate(xp, ng_ref[...], mv[0:1], mv[1:2]).astype(BF16)

    def emit(cs):
        gate = _gelu_tanh(_bdot(u, wgate_ref[:, cs]) + bgate_ref[:, cs])
        pq_ref[0, 0, 0, :, cs] = ((ab_scr[0, 0, :, cs] + ab_scr[1, 0, :, cs]) * gate).astype(BF16)
        pq_ref[0, 0, 1, :, cs] = (ab_scr[0, 1, :, cs] * gate).astype(BF16)
        pq_ref[0, 0, 2, :, cs] = (ab_scr[1, 1, :, cs] * gate).astype(BF16)

    _lru_local_scan(u, wrec_ref, brec_ref, cw_ref, cb_ref, wg_ref, bg_ref, lam_ref, agg_ref, ab_scr,
                    wrap=False, keep=True, after_head=emit)


def _lru_weight_specs(d, w, cw, wg, bg, lam):
    return [
        _const_spec((1, d)),
        _const_spec((d, w)),
        _const_spec((1, w)),
        _const_spec(cw.shape),
        _const_spec((1, w)),
        _const_spec(wg.shape),
        _const_spec(bg.shape),
        _const_spec(lam.shape),
    ]


def _lru_pass1_call(x, mv, ng, wrec, brec, cw, cb, wg, bg, lam, *, tile, wrap):
    b, s, d = x.shape
    w = wrec.shape[1]
    nt = s // tile
    return pl.pallas_call(
        functools.partial(_lru_pass1_kernel, wrap=wrap),
        grid=(b, nt),
        in_specs=[
            pl.BlockSpec((1, tile, d), lambda i, j: (i, j, 0)),
            pl.BlockSpec((1, SUBLANES, d), lambda i, j: (i, 0, 0)),
        ] + _lru_weight_specs(d, w, cw, wg, bg, lam),
        out_specs=pl.BlockSpec((1, 2, 2, NSEG, w), lambda i, j: (i, 0, 0, j, 0)),
        out_shape=jax.ShapeDtypeStruct((b, 2, 2, nt * NSEG, w), F32),
        scratch_shapes=[pltpu.VMEM((2, 2, tile, w), F32), _order_scratch(NSEG, tile // NSEG, d)],
        compiler_params=_cparams(("parallel", "parallel")),
    )(x, mv, ng, wrec, brec, cw, cb, wg, bg, lam)


def _segscan_kernel(aggl_ref, aggc_ref, hin_ref):
    nsl = aggl_ref.shape[3]
    nsc = aggc_ref.shape[3]
    w = aggl_ref.shape[-1]
    for e in range(2):
        order_c = range(nsc) if e == 0 else range(nsc - 1, -1, -1)
        order_l = range(nsl) if e == 0 else range(nsl - 1, -1, -1)
        st = jnp.zeros((1, w), F32)
        for s in order_c:
            st = aggc_ref[0, e, 0, s:s + 1, :] * st + aggc_ref[0, e, 1, s:s + 1, :]
        for s in order_l:
            hin_ref[0, e, s:s + 1, :] = st
            st = aggl_ref[0, e, 0, s:s + 1, :] * st + aggl_ref[0, e, 1, s:s + 1, :]


def _segscan_call(agg_l, agg_c):
    b, _, _, nsl, w = agg_l.shape
    nsc = agg_c.shape[3]
    return pl.pallas_call(
        _segscan_kernel,
        grid=(b,),
        in_specs=[
            pl.BlockSpec((1, 2, 2, nsl, w), lambda i: (i, 0, 0, 0, 0)),
            pl.BlockSpec((1, 2, 2, nsc, w), lambda i: (i, 0, 0, 0, 0)),
        ],
        out_specs=pl.BlockSpec((1, 2, nsl, w), lambda i: (i, 0, 0, 0)),
        out_shape=jax.ShapeDtypeStruct((b, 2, nsl, w), F32),
        compiler_params=_cparams(("parallel",)),
    )(agg_l, agg_c)


def _lru_mix_call(x, mv, ng, wrec, brec, cw, cb, wg, bg, lam, wgate, bgate, *, tile):
    b, s, d = x.shape
    w = wrec.shape[1]
    nt = s // tile
    return pl.pallas_call(
        _lru_mix_kernel,
        grid=(b, nt),
        in_specs=[
            pl.BlockSpec((1, tile, d), lambda i, j: (i, j, 0)),
            pl.BlockSpec((1, SUBLANES, d), lambda i, j: (i, 0, 0)),
        ] + _lru_weight_specs(d, w, cw, wg, bg, lam) + [
            _const_spec((d, w)),
            _const_spec((1, w)),
        ],
        out_specs=[
            pl.BlockSpec((1, 2, 2, NSEG, w), lambda i, j: (i, 0, 0, j, 0)),
            pl.BlockSpec((1, 1, 3, tile, w), lambda i, j: (i, j, 0, 0, 0)),
            pl.BlockSpec((1, GRID_W, NSEG, d), lambda i, j: (i, 0, j, 0)),
        ],
        out_shape=[
            jax.ShapeDtypeStruct((b, 2, 2, nt * NSEG, w), F32),
            jax.ShapeDtypeStruct((b, nt, 3, tile, w), BF16),
            jax.ShapeDtypeStruct((b, GRID_W, s // GRID_W, d), F32),
        ],
        scratch_shapes=[pltpu.VMEM((2, 2, tile, w), F32), _order_scratch(NSEG, tile // NSEG, d)],
        compiler_params=_cparams(("parallel", "parallel")),
    )(x, mv, ng, wrec, brec, cw, cb, wg, bg, lam, wgate, bgate)


FF_CHUNK = 1024


def _mlp_kernel(*refs, pre, final):
    refs = list(refs)
    x_ref = refs.pop(0)
    mv_ref = refs.pop(0)
    ng_ref = refs.pop(0)
    w1_ref = refs.pop(0)
    w2_ref = refs.pop(0)
    if pre == "hyena":
        v_ref = refs.pop(0)
    if pre == "lru":
        pq_ref = refs.pop(0)
        hin_ref = refs.pop(0)
    if pre:
        wout_ref = refs.pop(0)
        bout_ref = refs.pop(0)
    if final:
        fg_ref = refs.pop(0)
    o_ref = refs.pop(0)
    if final:
        order_scr = refs.pop(0)
    mv = mv_ref[0]
    groups = x_ref.shape[2]
    x = x_ref[0].reshape(GRID_W * groups, x_ref.shape[3])
    if pre == "lru":
        rows, w = pq_ref.shape[-2:]

        def times_entering(plane, e):
            running = pq_ref[0, 0, plane].astype(F32).reshape(rows // NSEG, NSEG, w)
            return (running * hin_ref[0, e][None]).reshape(rows, w)

        mixed = pq_ref[0, 0, 0].astype(F32) + times_entering(1, 0) + times_entering(2, 1)
        x = x + mv[2:3] * (_bdot(mixed.astype(BF16), wout_ref[...]) + bout_ref[...])
    if pre == "hyena":
        x = x + mv[2:3] * (_bdot(_load_lane_tiles(v_ref, 0), wout_ref[...]) + bout_ref[...])
    u = _modulate(x, ng_ref[...], mv[3:4], mv[4:5]).astype(BF16)
    acc = jnp.zeros(x.shape, F32)
    for c in range(w1_ref.shape[1] // FF_CHUNK):
        cs = slice(c * FF_CHUNK, (c + 1) * FF_CHUNK)
        h = jnp.maximum(_bdot(u, w1_ref[:, cs]), 0.0)
        acc = acc + _bdot((h * h).astype(BF16), w2_ref[cs, :])
    out = x + mv[5:6] * acc
    if final:
        _from_tile_order(_rms_norm(out, fg_ref[...]), groups, order_scr, o_ref)
    else:
        o_ref[0] = out.reshape(o_ref.shape[1:])


def _mlp_call(x, mv, ng, w1, w2, *, groups, hyena=None, lru=None, final_g=None):
    f = w1.shape[1]
    b, _, rows, d = x.shape
    tile_spec = pl.BlockSpec((1, GRID_W, groups, d), lambda i, j: (i, 0, j, 0))
    args = [x, mv, ng, w1, w2]
    in_specs = [
        tile_spec,
        pl.BlockSpec((1, SUBLANES, d), lambda i, j: (i, 0, 0)),
        _const_spec((1, d)),
        _const_spec((d, f)),
        _const_spec((f, d)),
    ]
    pre = None
    if hyena is not None:
        pre = "hyena"
        v, wout, bout = hyena
        args += [v, wout, bout]
        in_specs += [_lane_tiled_spec(d, groups), _const_spec(wout.shape), _const_spec((1, d))]
    if lru is not None:
        pre = "lru"
        pq, hin, wout, bout = lru
        args += [pq, hin, wout, bout]
        in_specs += [
            pl.BlockSpec((1, 1) + pq.shape[2:], lambda i, j: (i, j, 0, 0, 0)),
            pl.BlockSpec((1, 2, NSEG, hin.shape[-1]), lambda i, j: (i, 0, j, 0)),
            _const_spec(wout.shape),
            _const_spec((1, d)),
        ]
    if final_g is not None:
        args.append(final_g)
        in_specs.append(_const_spec((1, d)))
        out_spec = pl.BlockSpec((1, GRID_W * groups, d), lambda i, j: (i, j, 0))
        out_shape = jax.ShapeDtypeStruct((b, GRID_W * rows, d), F32)
    else:
        out_spec = tile_spec
        out_shape = jax.ShapeDtypeStruct((b, GRID_W, rows, d), F32)
    return pl.pallas_call(
        functools.partial(_mlp_kernel, pre=pre, final=final_g is not None),
        grid=(b, rows // groups),
        in_specs=in_specs,
        out_specs=out_spec,
        out_shape=out_shape,
        scratch_shapes=[_order_scratch(groups, GRID_W, d)] if final_g is not None else [],
        compiler_params=_cparams(("parallel", "parallel")),
    )(*args)


def _hyproj_kernel(x_ref, mv_ref, ng_ref, win_ref, bin_ref, cw_ref, cb_ref, v_ref, xa_ref, xb_ref):
    mv = mv_ref[0]
    groups = x_ref.shape[2]
    d = x_ref.shape[3]
    x = x_ref[0].reshape(GRID_W * groups, d)
    u = _modulate(x, ng_ref[...], mv[0:1], mv[1:2]).astype(BF16)
    for k, o_ref in enumerate((v_ref, xa_ref, xb_ref)):
        cs = slice(k * d, (k + 1) * d)
        z = _bdot(u, win_ref[:, cs]) + bin_ref[:, cs]
        z = _row_conv(z, cw_ref[:, cs], cb_ref[:, cs], HYENA_CONV_LEFT, groups).astype(BF16)
        _store_lane_tiles(o_ref, 0, z, groups)


def _store_lane_tiles(o_ref, lead, val, groups):
    for lt in range(val.shape[1] // LANES):
        o_ref[lead, lt] = val[:, lt * LANES:(lt + 1) * LANES].reshape(-1, groups, LANES)


def _load_lane_tiles(ref, lead):
    nlt, t2, groups, _ = ref.shape[1:]
    return jnp.concatenate([ref[lead, lt].reshape(t2 * groups, LANES) for lt in range(nlt)], axis=1)


def _lane_tiled_spec(d, groups):
    return pl.BlockSpec((1, d // LANES, GRID_W, groups, LANES), lambda i, j: (i, 0, 0, j, 0))


def _hyproj_call(x, mv, ng, win, bin_, cw, cb, *, groups):
    b, _, rows, d = x.shape
    tile_spec = pl.BlockSpec((1, GRID_W, groups, d), lambda i, j: (i, 0, j, 0))
    out_spec = _lane_tiled_spec(d, groups)
    out_sds = jax.ShapeDtypeStruct((b, d // LANES, GRID_W, rows, LANES), BF16)
    return pl.pallas_call(
        _hyproj_kernel,
        grid=(b, rows // groups),
        in_specs=[
            tile_spec,
            pl.BlockSpec((1, SUBLANES, d), lambda i, j: (i, 0, 0)),
            _const_spec((1, d)),
            _const_spec(win.shape),
            _const_spec(bin_.shape),
            _const_spec(cw.shape),
            _const_spec(cb.shape),
        ],
        out_specs=[out_spec, out_spec, out_spec],
        out_shape=[out_sds, out_sds, out_sds],
        compiler_params=_cparams(("parallel", "parallel")),
    )(x, mv, ng, win, bin_, cw, cb)


def _filter_kernel(pos_ref, fw1_ref, fb1_ref, fw2_ref, fb2_ref, fw3_ref, fb3_ref, fw4_ref, freq_ref,
                   deltas_ref, h_ref, nrm_ref):
    groups = pos_ref.shape[1]
    pe = pos_ref.shape[2]
    pos = pos_ref[...].reshape(GRID_W * groups, pe)
    half_rows = pos.shape[0] // 2
    half_t2 = GRID_W // 2
    pos2 = jnp.concatenate([pos[:half_rows], pos[half_rows:]], axis=1)
    freq = freq_ref[...]

    def hdot(a, b):
        return jnp.dot(a, b, preferred_element_type=F32, precision=HIGHEST)

    h = jnp.sin(freq * (hdot(pos2, fw1_ref[...]) + fb1_ref[...]))
    h = jnp.sin(freq * (hdot(h, fw2_ref[...]) + fb2_ref[...]))
    h = jnp.sin(freq * (hdot(h, fw3_ref[...]) + fb3_ref[...])).astype(BF16)
    d = deltas_ref.shape[1]
    n4 = fw4_ref.shape[1] // 2
    nparts = n4 // d
    sums = [jnp.zeros((1, d), F32)] * nparts
    for s in range(2):
        decay = jnp.exp(-pos2[:, s * pe:s * pe + 1] * deltas_ref[...])
        t2s = slice(s * half_t2, (s + 1) * half_t2)
        for p in range(nparts):
            cs = slice(p * d, (p + 1) * d)
            hp = _bdot(h, fw4_ref[:, s * n4 + p * d:s * n4 + (p + 1) * d]) * decay
            sums[p] = sums[p] + jnp.sum(jnp.abs(hp), axis=0, keepdims=True)
            hp16 = hp.astype(BF16)
            for lt in range(d // LANES):
                h_ref[p * (d // LANES) + lt, t2s] = (
                    hp16[:, lt * LANES:(lt + 1) * LANES].reshape(half_t2, groups, LANES))
    half = nparts // 2
    tot = jnp.concatenate([sums[p] + sums[p + half] for p in range(half)], axis=1)

    @pl.when(pl.program_id(0) == 0)
    def _():
        nrm_ref[...] = jnp.zeros_like(nrm_ref)

    nrm_ref[...] += tot


def _filter_call(pos, fw1, fb1, fw2, fb2, fw3, fb3, fw4, freq, deltas, *, groups):
    _, rows, pe = pos.shape
    fh = fw2.shape[0]
    n4 = fw4.shape[1] // 2
    d = deltas.shape[1]
    return pl.pallas_call(
        _filter_kernel,
        grid=(rows // groups,),
        in_specs=[
            pl.BlockSpec((GRID_W, groups, pe), lambda j: (0, j, 0)),
            _const_spec(fw1.shape), _const_spec((1, fh)),
            _const_spec((fh, fh)), _const_spec((1, fh)),
            _const_spec((fh, fh)), _const_spec((1, fh)),
            _const_spec(fw4.shape), _const_spec((1, fh)),
            _const_spec((1, d)),
        ],
        out_specs=[
            pl.BlockSpec((n4 // LANES, GRID_W, groups, LANES), lambda j: (0, 0, j, 0)),
            pl.BlockSpec((1, n4 // 2), lambda j: (0, 0)),
        ],
        out_shape=[
            jax.ShapeDtypeStruct((n4 // LANES, GRID_W, rows, LANES), BF16),
            jax.ShapeDtypeStruct((1, n4 // 2), F32),
        ],
        compiler_params=_cparams(("arbitrary",)),
    )(pos, fw1, fb1, fw2, fb2, fw3, fb3, fw4, freq, deltas)


@functools.lru_cache(maxsize=None)
def _dft_constants(seq_len):
    n = 2 * seq_len
    n2 = DFT_N2
    n1 = n // n2
    nt1 = n1 // 2
    nf = n1 // 2 + 1
    slots = _round_up(nf, SUBLANES)
    t1 = np.arange(nt1)[None, :]
    f1 = np.arange(slots)[:, None]
    live = (f1 < nf).astype(np.float64)
    ang1 = 2.0 * np.pi * (t1 * f1 % n1) / n1
    cos1, sin1 = np.cos(ang1) * live, np.sin(ang1) * live
    cf = np.full((slots, 1), 2.0)
    cf[0] = 1.0
    cf[nf - 1] = 1.0
    f1h = np.concatenate([cos1, -sin1], axis=0)
    f1i = np.concatenate([cos1 * cf, -sin1 * cf], axis=0).T
    t2 = np.arange(n2)[None, None, :]
    f2 = np.arange(n2)[None, :, None]
    ff1 = np.arange(nf)[:, None, None]
    ang2 = 2.0 * np.pi * ((t2 * (ff1 + n1 * f2)) % n) / n
    gr, gim = np.cos(ang2), -np.sin(ang2)
    g = np.concatenate([np.concatenate([gr, -gim], axis=2),
                        np.concatenate([gim, gr], axis=2)], axis=1)
    as32 = lambda a: np.ascontiguousarray(a, dtype=np.float32)
    return as32(f1h), as32(g), as32(f1i), nt1, nf, slots


def _spec_rows(t2, slots):
    return pl.ds(t2, slots, stride=SPEC_PITCH)


def _dft_stage1(load_slab, f1h_ref, spec_scr, slots):
    def body(t2, carry):
        a = _bdot(f1h_ref[...], load_slab(t2))
        for lt in range(2):
            ls = slice(lt * LANES, (lt + 1) * LANES)
            spec_scr[lt, _spec_rows(t2, slots), :] = a[:slots, ls]
            spec_scr[lt, _spec_rows(t2 + DFT_N2, slots), :] = a[slots:, ls]
        return carry

    lax.fori_loop(0, DFT_N2, body, 0, unroll=SLAB_UNROLL)


def _spec_slot_load(spec_scr, slot):
    rows = pl.ds(slot * SPEC_PITCH, 2 * DFT_N2)
    return jnp.concatenate([spec_scr[0, rows, :], spec_scr[1, rows, :]], axis=1), rows


def _filtfft_kernel(hf_ref, hb_ref, nrm_ref, f1h_ref, g_ref, k_ref, spec_scr, *, nf, slots):
    inv = 1.0 / nrm_ref[...]

    def slab(t2):
        return jnp.concatenate([hf_ref[0, t2], hb_ref[0, t2]], axis=1)

    _dft_stage1(slab, f1h_ref, spec_scr, slots)

    def body(f1, carry):
        a, _ = _spec_slot_load(spec_scr, f1)
        xs = _bdot(g_ref[f1], a.astype(BF16))
        fwd, bwd = xs[:, :LANES], xs[:, LANES:]
        k_ref[0, 0, f1] = (jnp.concatenate(
            [fwd[:DFT_N2] + bwd[:DFT_N2], fwd[DFT_N2:] - bwd[DFT_N2:]], axis=0) * inv).astype(BF16)
        return carry

    lax.fori_loop(0, nf, body, 0, unroll=FREQ_UNROLL)


def _filtfft_call(hraw, nrm, seq_len):
    f1h, g, _, nt1, nf, slots = _dft_constants(seq_len)
    nlt, _, rows, _ = hraw.shape
    nct = nlt // 4
    return pl.pallas_call(
        functools.partial(_filtfft_kernel, nf=nf, slots=slots),
        grid=(2, nct),
        in_specs=[
            pl.BlockSpec((1, GRID_W, rows, LANES), lambda o, c: (o * nct + c, 0, 0, 0)),
            pl.BlockSpec((1, GRID_W, rows, LANES), lambda o, c: (2 * nct + o * nct + c, 0, 0, 0)),
            pl.BlockSpec((1, LANES), lambda o, c: (0, o * nct + c)),
            _const_spec(f1h.shape),
            _const_spec(g.shape),
        ],
        out_specs=pl.BlockSpec((1, 1, nf, 2 * DFT_N2, LANES), lambda o, c: (o, c, 0, 0, 0)),
        out_shape=jax.ShapeDtypeStruct((2, nct, nf, 2 * DFT_N2, LANES), BF16),
        scratch_shapes=[pltpu.VMEM((2, slots * SPEC_PITCH, LANES), F32)],
        compiler_params=_cparams(("parallel", "parallel")),
    )(hraw, hraw, nrm, jnp.asarray(f1h).astype(BF16), jnp.asarray(g).astype(BF16))


def _longconv_kernel(v_ref, m_ref, k_ref, skip_ref, f1h_ref, g_ref, f1i_ref, o_ref,
                     spec_scr, *, nf, slots):
    def slab(t2):
        return jnp.concatenate([v_ref[0, 0, t2], v_ref[1, 0, t2]], axis=1)

    _dft_stage1(slab, f1h_ref, spec_scr, slots)

    def mid(f1, carry):
        a, rows = _spec_slot_load(spec_scr, f1)
        xs = _bdot(g_ref[f1], a.astype(BF16))
        kf = k_ref[0, 0, f1].astype(F32)
        kr = jnp.concatenate([kf[:DFT_N2]] * 2, axis=1)
        ki = jnp.concatenate([kf[DFT_N2:]] * 2, axis=1)
        xr, xi = xs[:DFT_N2], xs[DFT_N2:]
        ys = jnp.concatenate([xr * kr - xi * ki, xr * ki + xi * kr], axis=0).astype(BF16)
        bs = lax.dot_general(g_ref[f1], ys, (((0,), (0,)), ((), ())), preferred_element_type=F32)
        spec_scr[0, rows, :] = bs[:, :LANES]
        spec_scr[1, rows, :] = bs[:, LANES:]
        return carry

    lax.fori_loop(0, nf, mid, 0, unroll=FREQ_UNROLL)
    skip = skip_ref[...]

    def last(t2, carry):
        halves = []
        for lt in range(2):
            re = spec_scr[lt, _spec_rows(t2, slots), :]
            im = spec_scr[lt, _spec_rows(t2 + DFT_N2, slots), :]
            halves.append(jnp.concatenate([re, im], axis=0))
        y = _bdot(f1i_ref[...], jnp.concatenate(halves, axis=1).astype(BF16))
        for b in range(2):
            conv = y[:, b * LANES:(b + 1) * LANES]
            vs = v_ref[b, 0, t2].astype(F32)
            o_ref[b, 0, t2] = (m_ref[b, 0, t2].astype(F32) * (conv + vs * skip)).astype(BF16)
        return carry

    lax.fori_loop(0, DFT_N2, last, 0, unroll=SLAB_UNROLL)


def _longconv_call(v, m, kf, order, skip, seq_len):
    f1h, g, f1i, nt1, nf, slots = _dft_constants(seq_len)
    b, nct, _, rows, _ = v.shape
    seq_spec = pl.BlockSpec((2, 1, GRID_W, rows, LANES), lambda c, i: (i, c, 0, 0, 0))
    scale = 1.0 / (2 * seq_len)
    return pl.pallas_call(
        functools.partial(_longconv_kernel, nf=nf, slots=slots),
        grid=(nct, b // 2),
        in_specs=[
            seq_spec,
            seq_spec,
            pl.BlockSpec((1, 1, nf, 2 * DFT_N2, LANES), lambda c, i: (order, c, 0, 0, 0)),
            pl.BlockSpec((1, LANES), lambda c, i: (0, c)),
            _const_spec(f1h.shape),
            _const_spec(g.shape),
            _const_spec(f1i.shape),
        ],
        out_specs=seq_spec,
        out_shape=jax.ShapeDtypeStruct(v.shape, BF16),
        scratch_shapes=[pltpu.VMEM((2, slots * SPEC_PITCH, LANES), F32)],
        compiler_params=_cparams(("parallel", "parallel")),
    )(v, m, kf, skip, jnp.asarray(f1h).astype(BF16), jnp.asarray(g).astype(BF16),
      jnp.asarray(f1i * scale).astype(BF16))


def _mod_rows(mod_layer, nb, d):
    m = mod_layer.reshape(SUBLANES, 6, d)
    m = jnp.concatenate([m, jnp.zeros((SUBLANES, SUBLANES - 6, d), F32)], axis=1)
    return m[:nb], jnp.broadcast_to(m[nb:nb + 1], (nb, SUBLANES, d))


@functools.lru_cache(maxsize=None)
def _filter_constants(seq_len, d):
    t = np.linspace(0.0, 1.0, seq_len)[:, None]
    w = (2.0 * np.pi / seq_len) * np.arange(seq_len)[:, None]
    bands = np.linspace(1e-4, FILTER_BANDS - 1, FILTER_BANDS)
    pos = np.concatenate([t, np.cos(bands * w), -np.sin(bands * w)], axis=-1)
    pe = _round_up(pos.shape[1], LANES)
    pos = np.pad(pos, ((0, 0), (0, pe - pos.shape[1])))
    pos = pos.reshape(seq_len // GRID_W, GRID_W, pe).transpose(1, 0, 2)
    deltas = np.abs(np.linspace(math.log(FILTER_TARGET) / SLOW_DECAY_PCT,
                                math.log(FILTER_TARGET) / FAST_DECAY_PCT, d))[None, :]
    return np.ascontiguousarray(pos, dtype=np.float32), np.ascontiguousarray(deltas, dtype=np.float32)


def kernel(x, c, ctx, c_ctx, ada_w, ada_b, norm_g, mlp_w1, mlp_w2, lru_w_in, lru_b_in, lru_conv_w, lru_conv_b, lru_w_a, lru_b_a, lru_w_i, lru_b_i, lru_lambda, lru_w_out, lru_b_out, hy_w_in, hy_b_in, hy_conv_w, hy_conv_b, hy_fw1, hy_fb1, hy_fw2, hy_fb2, hy_fw3, hy_fb3, hy_fw4, hy_freq, hy_skip, hy_w_out, hy_b_out, final_g):
    nb, seq, d = x.shape
    ctx_len = ctx.shape[1]
    w = lru_w_out.shape[1]
    lru_tile = NSEG * GRID_W
    assert nb + 1 <= SUBLANES and nb % 2 == 0
    assert seq % (BF16_ROWS * GRID_W) == 0 and ctx_len % (NSEG * SUBLANES) == 0

    cvec = jnp.concatenate([c, c_ctx[None, :], jnp.zeros((SUBLANES - nb - 1, d), F32)], axis=0)
    mod = _ada_call(cvec, ada_w, ada_b)

    mv_l, mv_c = _mod_rows(mod[0], nb, d)
    ng = norm_g[0, 0][None, :]
    wgate = lru_w_in[0, :, :w].astype(BF16)
    wrec = lru_w_in[0, :, w:].astype(BF16)
    bgate = lru_b_in[0, :w][None, :]
    brec = lru_b_in[0, w:][None, :]
    cw = lru_conv_w[0]
    cb = lru_conv_b[0][None, :]
    wg = (0.5 * jnp.concatenate([lru_w_a[0, 0], lru_w_i[0, 0], lru_w_a[0, 1], lru_w_i[0, 1]], axis=-1)).astype(BF16)
    bg = 0.5 * jnp.concatenate([lru_b_a[0, 0], lru_b_i[0, 0], lru_b_a[0, 1], lru_b_i[0, 1]], axis=-1)[:, None, :]
    lam = lru_lambda[0]
    lru_w = (ng, wrec, brec, cw, cb, wg, bg, lam)
    agg_c = _lru_pass1_call(ctx, mv_c, *lru_w, tile=ctx_len, wrap=True)
    agg_l, pq, xt = _lru_mix_call(x, mv_l, *lru_w, wgate, bgate, tile=lru_tile)
    hin = _segscan_call(agg_l, agg_c)
    x2 = _mlp_call(xt, mv_l, norm_g[0, 1][None, :], mlp_w1[0].astype(BF16), mlp_w2[0].astype(BF16), groups=NSEG,
                   lru=(pq, hin, lru_w_out[0].astype(BF16), lru_b_out[0][None, :]))

    mv1, _ = _mod_rows(mod[1], nb, d)
    v, xa, xb = _hyproj_call(x2, mv1, norm_g[1, 0][None, :], hy_w_in[0].astype(BF16), hy_b_in[0][None, :],
                             hy_conv_w[0], hy_conv_b[0][None, :], groups=BF16_ROWS)
    pos, deltas = (jnp.asarray(a) for a in _filter_constants(seq, d))
    fw1 = jnp.pad(hy_fw1[0], ((0, pos.shape[2] - hy_fw1.shape[1]), (0, 0)))
    twin = lambda wmat: jnp.kron(jnp.eye(2, dtype=F32), wmat)
    twice = lambda vec: jnp.tile(vec[None, :], (1, 2))
    hraw, nrm = _filter_call(pos, twin(fw1), twice(hy_fb1[0]), twin(hy_fw2[0]), twice(hy_fb2[0]),
                             twin(hy_fw3[0]), twice(hy_fb3[0]), twin(hy_fw4[0]).astype(BF16),
                             twice(hy_freq[0]), deltas, groups=BF16_ROWS)
    kf = _filtfft_call(hraw, nrm, seq)
    v1 = _longconv_call(v, xa, kf, 0, hy_skip[0, 0][None, :], seq)
    v2 = _longconv_call(v1, xb, kf, 1, hy_skip[0, 1][None, :], seq)
    return _mlp_call(x2, mv1, norm_g[1, 1][None, :], mlp_w1[1].astype(BF16), mlp_w2[1].astype(BF16),
                     groups=BF16_ROWS, hyena=(v2, hy_w_out[0].astype(BF16), hy_b_out[0][None, :]),
                     final_g=final_g[None, :])
```

```python
import functools
import math

import numpy as np
import jax
import jax.numpy as jnp
from jax import lax
from jax.experimental import pallas as pl
from jax.experimental.pallas import tpu as pltpu

F32 = jnp.float32
BF16 = jnp.bfloat16
HIGHEST = lax.Precision.HIGHEST

NORM_EPS = 1e-6
GRID_W = 64
LRU_HEADS = 4
LRU_C = 8.0
LRU_CONV_LEFT = 2
HYENA_CONV_LEFT = 1
FILTER_BANDS = 16
FILTER_TARGET = 1e-2
FAST_DECAY_PCT = 0.3
SLOW_DECAY_PCT = 1.5

SUBLANES = 8
LANES = 128
NSEG = SUBLANES
BF16_ROWS = 16
V7X_VMEM_BYTES = 64 * 1024 * 1024
VMEM_LIMIT = V7X_VMEM_BYTES - 6 * 1024 * 1024

DFT_N2 = GRID_W
SPEC_PAD = 4
SPEC_PITCH = 2 * DFT_N2 + SPEC_PAD
SLAB_UNROLL = 64
FREQ_UNROLL = 43


def _cparams(sem):
    return pltpu.CompilerParams(dimension_semantics=sem, vmem_limit_bytes=VMEM_LIMIT)


def _const_spec(shape):
    nd = len(shape)
    return pl.BlockSpec(shape, lambda *_: (0,) * nd, pipeline_mode=pl.Buffered(1))


def _round_up(a, m):
    return (a + m - 1) // m * m


def _rms_norm(x, g):
    ms = jnp.mean(x * x, axis=-1, keepdims=True)
    return (x * lax.rsqrt(ms + NORM_EPS)) * g


def _modulate(x, g, shift, scale):
    return _rms_norm(x, g * (1.0 + scale)) + shift


def _gelu_tanh(x):
    c = math.sqrt(2.0 / math.pi)
    return x * (0.5 * (1.0 + jnp.tanh(c * (x + 0.044715 * (x * x * x)))))


def _softplus(x):
    return jnp.maximum(x, 0.0) + jnp.log1p(jnp.exp(-jnp.abs(x)))


def _bdot(a, b):
    return jnp.dot(a, b, preferred_element_type=F32)


def _order_pitch(length):
    return length + 4


def _order_scratch(groups, length, d):
    return pltpu.VMEM((d // LANES, groups * _order_pitch(length), LANES), F32)


def _to_tile_order(x, groups, scr):
    n, d = x.shape
    length = n // groups
    pitch = _order_pitch(length)
    for lt in range(d // LANES):
        for g in range(groups):
            scr[lt, g * pitch:g * pitch + length, :] = x[g * length:(g + 1) * length,
                                                         lt * LANES:(lt + 1) * LANES]
    rows = [jnp.concatenate([scr[lt, pl.ds(p, groups, stride=pitch), :] for lt in range(d // LANES)],
                            axis=1) for p in range(length)]
    return jnp.concatenate(rows, axis=0)


def _from_tile_order(x, groups, scr, o_ref):
    n, d = x.shape
    length = n // groups
    pitch = _order_pitch(length)
    for lt in range(d // LANES):
        ls = slice(lt * LANES, (lt + 1) * LANES)
        for p in range(length):
            scr[lt, pl.ds(p, groups, stride=pitch), :] = x[p * groups:(p + 1) * groups, ls]
        for g in range(groups):
            o_ref[0, g * length:(g + 1) * length, ls] = scr[lt, g * pitch:g * pitch + length, :]


def _wrapped_edge(edge, step):
    pieces = []
    for p in range(edge.shape[0] // SUBLANES):
        piece = edge[p * SUBLANES:(p + 1) * SUBLANES]
        sub = lax.broadcasted_iota(jnp.int32, piece.shape, 0)
        if step > 0:
            pieces.append(jnp.where(sub == 0, 0.0, pltpu.roll(piece, 1, 0)))
        else:
            pieces.append(jnp.where(sub == SUBLANES - 1, 0.0, pltpu.roll(piece, SUBLANES - 1, 0)))
    return jnp.concatenate(pieces, axis=0) if len(pieces) > 1 else pieces[0]


def _shift_tokens(z, o, groups, wrap):
    n = abs(o) * groups
    rows = z.shape[0]
    if o < 0:
        edge = _wrapped_edge(z[rows - n:], 1) if wrap else jnp.zeros((n, z.shape[1]), z.dtype)
        return jnp.concatenate([edge, z[:rows - n]], axis=0)
    edge = _wrapped_edge(z[:n], -1) if wrap else jnp.zeros((n, z.shape[1]), z.dtype)
    return jnp.concatenate([z[n:], edge], axis=0)


def _row_conv(z, w, b, left, groups, wrap=False):
    acc = b + w[left:left + 1] * z
    for k in range(w.shape[0]):
        if k != left:
            acc = acc + w[k:k + 1] * _shift_tokens(z, k - left, groups, wrap)
    return acc


def _ada_kernel(c_ref, w_ref, b_ref, o_ref):
    c = c_ref[...]
    cond = c * jax.nn.sigmoid(c)
    o_ref[0] = _bdot(cond.astype(BF16), w_ref[0].astype(BF16)) + b_ref[0]


def _ada_call(cvec, ada_w, ada_b):
    depth, d, n = ada_w.shape
    tn = 1536
    return pl.pallas_call(
        _ada_kernel,
        grid=(depth, n // tn),
        in_specs=[
            pl.BlockSpec((SUBLANES, d), lambda i, j: (0, 0)),
            pl.BlockSpec((1, d, tn), lambda i, j: (i, 0, j)),
            pl.BlockSpec((1, 1, tn), lambda i, j: (i, 0, j)),
        ],
        out_specs=pl.BlockSpec((1, SUBLANES, tn), lambda i, j: (i, 0, j)),
        out_shape=jax.ShapeDtypeStruct((depth, SUBLANES, n), F32),
        compiler_params=_cparams(("parallel", "parallel")),
    )(cvec, ada_w, ada_b.reshape(depth, 1, n))


def _lru_head_coeffs(xh, wg_h, bg_h, half_c_sp_h, ab_scr, cs):
    hb = xh.shape[1]
    gates = _bdot(xh.astype(BF16), wg_h) + bg_h
    xh_half = 0.5 * xh
    for e in range(2):
        tr = jnp.tanh(gates[:, (2 * e) * hb:(2 * e + 1) * hb])
        ti = jnp.tanh(gates[:, (2 * e + 1) * hb:(2 * e + 2) * hb])
        c = half_c_sp_h[e:e + 1]
        a = jnp.exp(c * tr + c)
        q = 1.0 - a * a
        wgt = jnp.where(q > 0.0, q * lax.rsqrt(q), 0.0) * xh_half
        ab_scr[e, 0, :, cs] = a
        ab_scr[e, 1, :, cs] = wgt * ti + wgt


def _lru_head_scan(ab_scr, agg_ref, cs, seg_len, keep):
    hb = cs.stop - cs.start
    pf = pb = jnp.ones((NSEG, hb), F32)
    hf = hbk = jnp.zeros((NSEG, hb), F32)
    for i in range(seg_len):
        rf = slice(i * NSEG, (i + 1) * NSEG)
        rb = slice((seg_len - 1 - i) * NSEG, (seg_len - i) * NSEG)
        af = ab_scr[0, 0, rf, cs]
        ab = ab_scr[1, 0, rb, cs]
        pf, hf = pf * af, af * hf + ab_scr[0, 1, rf, cs]
        pb, hbk = pb * ab, ab * hbk + ab_scr[1, 1, rb, cs]
        if keep:
            ab_scr[0, 0, rf, cs] = hf
            ab_scr[0, 1, rf, cs] = pf
            ab_scr[1, 0, rb, cs] = hbk
            ab_scr[1, 1, rb, cs] = pb
    agg_ref[0, 0, 0, :, cs] = pf
    agg_ref[0, 0, 1, :, cs] = hf
    agg_ref[0, 1, 0, :, cs] = pb
    agg_ref[0, 1, 1, :, cs] = hbk


def _lru_local_scan(u, wrec_ref, brec_ref, cw_ref, cb_ref, wg_ref, bg_ref, lam_ref, agg_ref, ab_scr,
                    *, wrap, keep, after_head=None):
    seg_len = u.shape[0] // NSEG
    w = ab_scr.shape[-1]
    hb = w // LRU_HEADS
    zr = _bdot(u, wrec_ref[...]) + brec_ref[...]
    xl = _row_conv(zr, cw_ref[...], cb_ref[...], LRU_CONV_LEFT, NSEG, wrap)
    half_c_sp = (-0.5 * LRU_C) * _softplus(-lam_ref[...])
    for h in range(LRU_HEADS):
        cs = slice(h * hb, (h + 1) * hb)
        _lru_head_coeffs(xl[:, cs], wg_ref[h], bg_ref[h], half_c_sp[:, cs], ab_scr, cs)
        _lru_head_scan(ab_scr, agg_ref, cs, seg_len, keep)
        if after_head is not None:
            after_head(cs)


def _lru_pass1_kernel(x_ref, mv_ref, ng_ref, wrec_ref, brec_ref, cw_ref, cb_ref, wg_ref, bg_ref,
                      lam_ref, agg_ref, ab_scr, order_scr, *, wrap):
    mv = mv_ref[0]
    xp = _to_tile_order(x_ref[0], NSEG, order_scr)
    u = _modulate(xp, ng_ref[...], mv[0:1], mv[1:2]).astype(BF16)
    _lru_local_scan(u, wrec_ref, brec_ref, cw_ref, cb_ref, wg_ref, bg_ref, lam_ref, agg_ref, ab_scr,
                    wrap=wrap, keep=False)


def _lru_mix_kernel(x_ref, mv_ref, ng_ref, wrec_ref, brec_ref, cw_ref, cb_ref, wg_ref, bg_ref,
                    lam_ref, wgate_ref, bgate_ref, agg_ref, pq_ref, xt_ref, ab_scr, order_scr):
    mv = mv_ref[0]
    xp = _to_tile_order(x_ref[0], NSEG, order_scr)
    xt_ref[0] = xp.reshape(xt_ref.shape[1:])
    u = _modulate(xp, ng_ref[...], mv[0:1], mv[1:2]).astype(BF16)

    def emit(cs):
        gate = _gelu_tanh(_bdot(u, wgate_ref[:, cs]) + bgate_ref[:, cs])
        pq_ref[0, 0, 0, :, cs] = ((ab_scr[0, 0, :, cs] + ab_scr[1, 0, :, cs]) * gate).astype(BF16)
        pq_ref[0, 0, 1, :, cs] = (ab_scr[0, 1, :, cs] * gate).astype(BF16)
        pq_ref[0, 0, 2, :, cs] = (ab_scr[1, 1, :, cs] * gate).astype(BF16)

    _lru_local_scan(u, wrec_ref, brec_ref, cw_ref, cb_ref, wg_ref, bg_ref, lam_ref, agg_ref, ab_scr,
                    wrap=False, keep=True, after_head=emit)


def _lru_weight_specs(d, w, cw, wg, bg, lam):
    return [
        _const_spec((1, d)),
        _const_spec((d, w)),
        _const_spec((1, w)),
        _const_spec(cw.shape),
        _const_spec((1, w)),
        _const_spec(wg.shape),
        _const_spec(bg.shape),
        _const_spec(lam.shape),
    ]


def _lru_pass1_call(x, mv, ng, wrec, brec, cw, cb, wg, bg, lam, *, tile, wrap):
    b, s, d = x.shape
    w = wrec.shape[1]
    nt = s // tile
    return pl.pallas_call(
        functools.partial(_lru_pass1_kernel, wrap=wrap),
        grid=(b, nt),
        in_specs=[
            pl.BlockSpec((1, tile, d), lambda i, j: (i, j, 0)),
            pl.BlockSpec((1, SUBLANES, d), lambda i, j: (i, 0, 0)),
        ] + _lru_weight_specs(d, w, cw, wg, bg, lam),
        out_specs=pl.BlockSpec((1, 2, 2, NSEG, w), lambda i, j: (i, 0, 0, j, 0)),
        out_shape=jax.ShapeDtypeStruct((b, 2, 2, nt * NSEG, w), F32),
        scratch_shapes=[pltpu.VMEM((2, 2, tile, w), F32), _order_scratch(NSEG, tile // NSEG, d)],
        compiler_params=_cparams(("parallel", "parallel")),
    )(x, mv, ng, wrec, brec, cw, cb, wg, bg, lam)


def _segscan_kernel(aggl_ref, aggc_ref, hin_ref):
    nsl = aggl_ref.shape[3]
    nsc = aggc_ref.shape[3]
    w = aggl_ref.shape[-1]
    for e in range(2):
        order_c = range(nsc) if e == 0 else range(nsc - 1, -1, -1)
        order_l = range(nsl) if e == 0 else range(nsl - 1, -1, -1)
        st = jnp.zeros((1, w), F32)
        for s in order_c:
            st = aggc_ref[0, e, 0, s:s + 1, :] * st + aggc_ref[0, e, 1, s:s + 1, :]
        for s in order_l:
            hin_ref[0, e, s:s + 1, :] = st
            st = aggl_ref[0, e, 0, s:s + 1, :] * st + aggl_ref[0, e, 1, s:s + 1, :]


def _segscan_call(agg_l, agg_c):
    b, _, _, nsl, w = agg_l.shape
    nsc = agg_c.shape[3]
    return pl.pallas_call(
        _segscan_kernel,
        grid=(b,),
        in_specs=[
            pl.BlockSpec((1, 2, 2, nsl, w), lambda i: (i, 0, 0, 0, 0)),
            pl.BlockSpec((1, 2, 2, nsc, w), lambda i: (i, 0, 0, 0, 0)),
        ],
        out_specs=pl.BlockSpec((1, 2, nsl, w), lambda i: (i, 0, 0, 0)),
        out_shape=jax.ShapeDtypeStruct((b, 2, nsl, w), F32),
        compiler_params=_cparams(("parallel",)),
    )(agg_l, agg_c)


def _lru_mix_call(x, mv, ng, wrec, brec, cw, cb, wg, bg, lam, wgate, bgate, *, tile):
    b, s, d = x.shape
    w = wrec.shape[1]
    nt = s // tile
    return pl.pallas_call(
        _lru_mix_kernel,
        grid=(b, nt),
        in_specs=[
            pl.BlockSpec((1, tile, d), lambda i, j: (i, j, 0)),
            pl.BlockSpec((1, SUBLANES, d), lambda i, j: (i, 0, 0)),
        ] + _lru_weight_specs(d, w, cw, wg, bg, lam) + [
            _const_spec((d, w)),
            _const_spec((1, w)),
        ],
        out_specs=[
            pl.BlockSpec((1, 2, 2, NSEG, w), lambda i, j: (i, 0, 0, j, 0)),
            pl.BlockSpec((1, 1, 3, tile, w), lambda i, j: (i, j, 0, 0, 0)),
            pl.BlockSpec((1, GRID_W, NSEG, d), lambda i, j: (i, 0, j, 0)),
        ],
        out_shape=[
            jax.ShapeDtypeStruct((b, 2, 2, nt * NSEG, w), F32),
            jax.ShapeDtypeStruct((b, nt, 3, tile, w), BF16),
            jax.ShapeDtypeStruct((b, GRID_W, s // GRID_W, d), F32),
        ],
        scratch_shapes=[pltpu.VMEM((2, 2, tile, w), F32), _order_scratch(NSEG, tile // NSEG, d)],
        compiler_params=_cparams(("parallel", "parallel")),
    )(x, mv, ng, wrec, brec, cw, cb, wg, bg, lam, wgate, bgate)


FF_CHUNK = 1024


def _mlp_kernel(*refs, pre, final):
    refs = list(refs)
    x_ref = refs.pop(0)
    mv_ref = refs.pop(0)
    ng_ref = refs.pop(0)
    w1_ref = refs.pop(0)
    w2_ref = refs.pop(0)
    if pre == "hyena":
        v_ref = refs.pop(0)
    if pre == "lru":
        pq_ref = refs.pop(0)
        hin_ref = refs.pop(0)
    if pre:
        wout_ref = refs.pop(0)
        bout_ref = refs.pop(0)
    if final:
        fg_ref = refs.pop(0)
    o_ref = refs.pop(0)
    if final:
        order_scr = refs.pop(0)
    mv = mv_ref[0]
    groups = x_ref.shape[2]
    x = x_ref[0].reshape(GRID_W * groups, x_ref.shape[3])
    if pre == "lru":
        rows, w = pq_ref.shape[-2:]

        def times_entering(plane, e):
            running = pq_ref[0, 0, plane].astype(F32).reshape(rows // NSEG, NSEG, w)
            return (running * hin_ref[0, e][None]).reshape(rows, w)

        mixed = pq_ref[0, 0, 0].astype(F32) + times_entering(1, 0) + times_entering(2, 1)
        x = x + mv[2:3] * (_bdot(mixed.astype(BF16), wout_ref[...]) + bout_ref[...])
    if pre == "hyena":
        x = x + mv[2:3] * (_bdot(_load_lane_tiles(v_ref, 0), wout_ref[...]) + bout_ref[...])
    u = _modulate(x, ng_ref[...], mv[3:4], mv[4:5]).astype(BF16)
    acc = jnp.zeros(x.shape, F32)
    for c in range(w1_ref.shape[1] // FF_CHUNK):
        cs = slice(c * FF_CHUNK, (c + 1) * FF_CHUNK)
        h = jnp.maximum(_bdot(u, w1_ref[:, cs]), 0.0)
        acc = acc + _bdot((h * h).astype(BF16), w2_ref[cs, :])
    out = x + mv[5:6] * acc
    if final:
        _from_tile_order(_rms_norm(out, fg_ref[...]), groups, order_scr, o_ref)
    else:
        o_ref[0] = out.reshape(o_ref.shape[1:])


def _mlp_call(x, mv, ng, w1, w2, *, groups, hyena=None, lru=None, final_g=None):
    f = w1.shape[1]
    b, _, rows, d = x.shape
    tile_spec = pl.BlockSpec((1, GRID_W, groups, d), lambda i, j: (i, 0, j, 0))
    args = [x, mv, ng, w1, w2]
    in_specs = [
        tile_spec,
        pl.BlockSpec((1, SUBLANES, d), lambda i, j: (i, 0, 0)),
        _const_spec((1, d)),
        _const_spec((d, f)),
        _const_spec((f, d)),
    ]
    pre = None
    if hyena is not None:
        pre = "hyena"
        v, wout, bout = hyena
        args += [v, wout, bout]
        in_specs += [_lane_tiled_spec(d, groups), _const_spec(wout.shape), _const_spec((1, d))]
    if lru is not None:
        pre = "lru"
        pq, hin, wout, bout = lru
        args += [pq, hin, wout, bout]
        in_specs += [
            pl.BlockSpec((1, 1) + pq.shape[2:], lambda i, j: (i, j, 0, 0, 0)),
            pl.BlockSpec((1, 2, NSEG, hin.shape[-1]), lambda i, j: (i, 0, j, 0)),
            _const_spec(wout.shape),
            _const_spec((1, d)),
        ]
    if final_g is not None:
        args.append(final_g)
        in_specs.append(_const_spec((1, d)))
        out_spec = pl.BlockSpec((1, GRID_W * groups, d), lambda i, j: (i, j, 0))
        out_shape = jax.ShapeDtypeStruct((b, GRID_W * rows, d), F32)
    else:
        out_spec = tile_spec
        out_shape = jax.ShapeDtypeStruct((b, GRID_W, rows, d), F32)
    return pl.pallas_call(
        functools.partial(_mlp_kernel, pre=pre, final=final_g is not None),
        grid=(b, rows // groups),
        in_specs=in_specs,
        out_specs=out_spec,
        out_shape=out_shape,
        scratch_shapes=[_order_scratch(groups, GRID_W, d)] if final_g is not None else [],
        compiler_params=_cparams(("parallel", "parallel")),
    )(*args)


def _hyproj_kernel(x_ref, mv_ref, ng_ref, win_ref, bin_ref, cw_ref, cb_ref, v_ref, xa_ref, xb_ref):
    mv = mv_ref[0]
    groups = x_ref.shape[2]
    d = x_ref.shape[3]
    x = x_ref[0].reshape(GRID_W * groups, d)
    u = _modulate(x, ng_ref[...], mv[0:1], mv[1:2]).astype(BF16)
    for k, o_ref in enumerate((v_ref, xa_ref, xb_ref)):
        cs = slice(k * d, (k + 1) * d)
        z = _bdot(u, win_ref[:, cs]) + bin_ref[:, cs]
        z = _row_conv(z, cw_ref[:, cs], cb_ref[:, cs], HYENA_CONV_LEFT, groups).astype(BF16)
        _store_lane_tiles(o_ref, 0, z, groups)


def _store_lane_tiles(o_ref, lead, val, groups):
    for lt in range(val.shape[1] // LANES):
        o_ref[lead, lt] = val[:, lt * LANES:(lt + 1) * LANES].reshape(-1, groups, LANES)


def _load_lane_tiles(ref, lead):
    nlt, t2, groups, _ = ref.shape[1:]
    return jnp.concatenate([ref[lead, lt].reshape(t2 * groups, LANES) for lt in range(nlt)], axis=1)


def _lane_tiled_spec(d, groups):
    return pl.BlockSpec((1, d // LANES, GRID_W, groups, LANES), lambda i, j: (i, 0, 0, j, 0))


def _hyproj_call(x, mv, ng, win, bin_, cw, cb, *, groups):
    b, _, rows, d = x.shape
    tile_spec = pl.BlockSpec((1, GRID_W, groups, d), lambda i, j: (i, 0, j, 0))
    out_spec = _lane_tiled_spec(d, groups)
    out_sds = jax.ShapeDtypeStruct((b, d // LANES, GRID_W, rows, LANES), BF16)
    return pl.pallas_call(
        _hyproj_kernel,
        grid=(b, rows // groups),
        in_specs=[
            tile_spec,
            pl.BlockSpec((1, SUBLANES, d), lambda i, j: (i, 0, 0)),
            _const_spec((1, d)),
            _const_spec(win.shape),
            _const_spec(bin_.shape),
            _const_spec(cw.shape),
            _const_spec(cb.shape),
        ],
        out_specs=[out_spec, out_spec, out_spec],
        out_shape=[out_sds, out_sds, out_sds],
        compiler_params=_cparams(("parallel", "parallel")),
    )(x, mv, ng, win, bin_, cw, cb)


def _filter_kernel(pos_ref, fw1_ref, fb1_ref, fw2_ref, fb2_ref, fw3_ref, fb3_ref, fw4_ref, freq_ref,
                   deltas_ref, h_ref, nrm_ref):
    groups = pos_ref.shape[1]
    pe = pos_ref.shape[2]
    pos = pos_ref[...].reshape(GRID_W * groups, pe)
    half_rows = pos.shape[0] // 2
    half_t2 = GRID_W // 2
    pos2 = jnp.concatenate([pos[:half_rows], pos[half_rows:]], axis=1)
    freq = freq_ref[...]

    def hdot(a, b):
        return jnp.dot(a, b, preferred_element_type=F32, precision=HIGHEST)

    h = jnp.sin(freq * (hdot(pos2, fw1_ref[...]) + fb1_ref[...]))
    h = jnp.sin(freq * (hdot(h, fw2_ref[...]) + fb2_ref[...]))
    h = jnp.sin(freq * (hdot(h, fw3_ref[...]) + fb3_ref[...])).astype(BF16)
    d = deltas_ref.shape[1]
    n4 = fw4_ref.shape[1] // 2
    nparts = n4 // d
    sums = [jnp.zeros((1, d), F32)] * nparts
    for s in range(2):
        decay = jnp.exp(-pos2[:, s * pe:s * pe + 1] * deltas_ref[...])
        t2s = slice(s * half_t2, (s + 1) * half_t2)
        for p in range(nparts):
            cs = slice(p * d, (p + 1) * d)
            hp = _bdot(h, fw4_ref[:, s * n4 + p * d:s * n4 + (p + 1) * d]) * decay
            sums[p] = sums[p] + jnp.sum(jnp.abs(hp), axis=0, keepdims=True)
            hp16 = hp.astype(BF16)
            for lt in range(d // LANES):
                h_ref[p * (d // LANES) + lt, t2s] = (
                    hp16[:, lt * LANES:(lt + 1) * LANES].reshape(half_t2, groups, LANES))
    half = nparts // 2
    tot = jnp.concatenate([sums[p] + sums[p + half] for p in range(half)], axis=1)

    @pl.when(pl.program_id(0) == 0)
    def _():
        nrm_ref[...] = jnp.zeros_like(nrm_ref)

    nrm_ref[...] += tot


def _filter_call(pos, fw1, fb1, fw2, fb2, fw3, fb3, fw4, freq, deltas, *, groups):
    _, rows, pe = pos.shape
    fh = fw2.shape[0]
    n4 = fw4.shape[1] // 2
    d = deltas.shape[1]
    return pl.pallas_call(
        _filter_kernel,
        grid=(rows // groups,),
        in_specs=[
            pl.BlockSpec((GRID_W, groups, pe), lambda j: (0, j, 0)),
            _const_spec(fw1.shape), _const_spec((1, fh)),
            _const_spec((fh, fh)), _const_spec((1, fh)),
            _const_spec((fh, fh)), _const_spec((1, fh)),
            _const_spec(fw4.shape), _const_spec((1, fh)),
            _const_spec((1, d)),
        ],
        out_specs=[
            pl.BlockSpec((n4 // LANES, GRID_W, groups, LANES), lambda j: (0, 0, j, 0)),
            pl.BlockSpec((1, n4 // 2), lambda j: (0, 0)),
        ],
        out_shape=[
            jax.ShapeDtypeStruct((n4 // LANES, GRID_W, rows, LANES), BF16),
            jax.ShapeDtypeStruct((1, n4 // 2), F32),
        ],
        compiler_params=_cparams(("arbitrary",)),
    )(pos, fw1, fb1, fw2, fb2, fw3, fb3, fw4, freq, deltas)


@functools.lru_cache(maxsize=None)
def _dft_constants(seq_len):
    n = 2 * seq_len
    n2 = DFT_N2
    n1 = n // n2
    nt1 = n1 // 2
    nf = n1 // 2 + 1
    slots = _round_up(nf, SUBLANES)
    t1 = np.arange(nt1)[None, :]
    f1 = np.arange(slots)[:, None]
    live = (f1 < nf).astype(np.float64)
    ang1 = 2.0 * np.pi * (t1 * f1 % n1) / n1
    cos1, sin1 = np.cos(ang1) * live, np.sin(ang1) * live
    cf = np.full((slots, 1), 2.0)
    cf[0] = 1.0
    cf[nf - 1] = 1.0
    f1h = np.concatenate([cos1, -sin1], axis=0)
    f1i = np.concatenate([cos1 * cf, -sin1 * cf], axis=0).T
    t2 = np.arange(n2)[None, None, :]
    f2 = np.arange(n2)[None, :, None]
    ff1 = np.arange(nf)[:, None, None]
    ang2 = 2.0 * np.pi * ((t2 * (ff1 + n1 * f2)) % n) / n
    gr, gim = np.cos(ang2), -np.sin(ang2)
    g = np.concatenate([np.concatenate([gr, -gim], axis=2),
                        np.concatenate([gim, gr], axis=2)], axis=1)
    as32 = lambda a: np.ascontiguousarray(a, dtype=np.float32)
    return as32(f1h), as32(g), as32(f1i), nt1, nf, slots


def _spec_rows(t2, slots):
    return pl.ds(t2, slots, stride=SPEC_PITCH)


def _dft_stage1(load_slab, f1h_ref, spec_scr, slots):
    def body(t2, carry):
        a = _bdot(f1h_ref[...], load_slab(t2))
        for lt in range(2):
            ls = slice(lt * LANES, (lt + 1) * LANES)
            spec_scr[lt, _spec_rows(t2, slots), :] = a[:slots, ls]
            spec_scr[lt, _spec_rows(t2 + DFT_N2, slots), :] = a[slots:, ls]
        return carry

    lax.fori_loop(0, DFT_N2, body, 0, unroll=SLAB_UNROLL)


def _spec_slot_load(spec_scr, slot):
    rows = pl.ds(slot * SPEC_PITCH, 2 * DFT_N2)
    return jnp.concatenate([spec_scr[0, rows, :], spec_scr[1, rows, :]], axis=1), rows


def _filtfft_kernel(hf_ref, hb_ref, nrm_ref, f1h_ref, g_ref, k_ref, spec_scr, *, nf, slots):
    inv = 1.0 / nrm_ref[...]

    def slab(t2):
        return jnp.concatenate([hf_ref[0, t2], hb_ref[0, t2]], axis=1)

    _dft_stage1(slab, f1h_ref, spec_scr, slots)

    def body(f1, carry):
        a, _ = _spec_slot_load(spec_scr, f1)
        xs = _bdot(g_ref[f1], a.astype(BF16))
        fwd, bwd = xs[:, :LANES], xs[:, LANES:]
        k_ref[0, 0, f1] = (jnp.concatenate(
            [fwd[:DFT_N2] + bwd[:DFT_N2], fwd[DFT_N2:] - bwd[DFT_N2:]], axis=0) * inv).astype(BF16)
        return carry

    lax.fori_loop(0, nf, body, 0, unroll=FREQ_UNROLL)


def _filtfft_call(hraw, nrm, seq_len):
    f1h, g, _, nt1, nf, slots = _dft_constants(seq_len)
    nlt, _, rows, _ = hraw.shape
    nct = nlt // 4
    return pl.pallas_call(
        functools.partial(_filtfft_kernel, nf=nf, slots=slots),
        grid=(2, nct),
        in_specs=[
            pl.BlockSpec((1, GRID_W, rows, LANES), lambda o, c: (o * nct + c, 0, 0, 0)),
            pl.BlockSpec((1, GRID_W, rows, LANES), lambda o, c: (2 * nct + o * nct + c, 0, 0, 0)),
            pl.BlockSpec((1, LANES), lambda o, c: (0, o * nct + c)),
            _const_spec(f1h.shape),
            _const_spec(g.shape),
        ],
        out_specs=pl.BlockSpec((1, 1, nf, 2 * DFT_N2, LANES), lambda o, c: (o, c, 0, 0, 0)),
        out_shape=jax.ShapeDtypeStruct((2, nct, nf, 2 * DFT_N2, LANES), BF16),
        scratch_shapes=[pltpu.VMEM((2, slots * SPEC_PITCH, LANES), F32)],
        compiler_params=_cparams(("parallel", "parallel")),
    )(hraw, hraw, nrm, jnp.asarray(f1h).astype(BF16), jnp.asarray(g).astype(BF16))


def _longconv_kernel(v_ref, m_ref, k_ref, skip_ref, f1h_ref, g_ref, f1i_ref, o_ref,
                     spec_scr, *, nf, slots):
    def slab(t2):
        return jnp.concatenate([v_ref[0, 0, t2], v_ref[1, 0, t2]], axis=1)

    _dft_stage1(slab, f1h_ref, spec_scr, slots)

    def mid(f1, carry):
        a, rows = _spec_slot_load(spec_scr, f1)
        xs = _bdot(g_ref[f1], a.astype(BF16))
        kf = k_ref[0, 0, f1].astype(F32)
        kr = jnp.concatenate([kf[:DFT_N2]] * 2, axis=1)
        ki = jnp.concatenate([kf[DFT_N2:]] * 2, axis=1)
        xr, xi = xs[:DFT_N2], xs[DFT_N2:]
        ys = jnp.concatenate([xr * kr - xi * ki, xr * ki + xi * kr], axis=0).astype(BF16)
        bs = lax.dot_general(g_ref[f1], ys, (((0,), (0,)), ((), ())), preferred_element_type=F32)
        spec_scr[0, rows, :] = bs[:, :LANES]
        spec_scr[1, rows, :] = bs[:, LANES:]
        return carry

    lax.fori_loop(0, nf, mid, 0, unroll=FREQ_UNROLL)
    skip = skip_ref[...]

    def last(t2, carry):
        halves = []
        for lt in range(2):
            re = spec_scr[lt, _spec_rows(t2, slots), :]
            im = spec_scr[lt, _spec_rows(t2 + DFT_N2, slots), :]
            halves.append(jnp.concatenate([re, im], axis=0))
        y = _bdot(f1i_ref[...], jnp.concatenate(halves, axis=1).astype(BF16))
        for b in range(2):
            conv = y[:, b * LANES:(b + 1) * LANES]
            vs = v_ref[b, 0, t2].astype(F32)
            o_ref[b, 0, t2] = (m_ref[b, 0, t2].astype(F32) * (conv + vs * skip)).astype(BF16)
        return carry

    lax.fori_loop(0, DFT_N2, last, 0, unroll=SLAB_UNROLL)


def _longconv_call(v, m, kf, order, skip, seq_len):
    f1h, g, f1i, nt1, nf, slots = _dft_constants(seq_len)
    b, nct, _, rows, _ = v.shape
    seq_spec = pl.BlockSpec((2, 1, GRID_W, rows, LANES), lambda c, i: (i, c, 0, 0, 0))
    scale = 1.0 / (2 * seq_len)
    return pl.pallas_call(
        functools.partial(_longconv_kernel, nf=nf, slots=slots),
        grid=(nct, b // 2),
        in_specs=[
            seq_spec,
            seq_spec,
            pl.BlockSpec((1, 1, nf, 2 * DFT_N2, LANES), lambda c, i: (order, c, 0, 0, 0)),
            pl.BlockSpec((1, LANES), lambda c, i: (0, c)),
            _const_spec(f1h.shape),
            _const_spec(g.shape),
            _const_spec(f1i.shape),
        ],
        out_specs=seq_spec,
        out_shape=jax.ShapeDtypeStruct(v.shape, BF16),
        scratch_shapes=[pltpu.VMEM((2, slots * SPEC_PITCH, LANES), F32)],
        compiler_params=_cparams(("parallel", "parallel")),
    )(v, m, kf, skip, jnp.asarray(f1h).astype(BF16), jnp.asarray(g).astype(BF16),
      jnp.asarray(f1i * scale).astype(BF16))


def _mod_rows(mod_layer, nb, d):
    m = mod_layer.reshape(SUBLANES, 6, d)
    m = jnp.concatenate([m, jnp.zeros((SUBLANES, SUBLANES - 6, d), F32)], axis=1)
    return m[:nb], jnp.broadcast_to(m[nb:nb + 1], (nb, SUBLANES, d))


@functools.lru_cache(maxsize=None)
def _filter_constants(seq_len, d):
    t = np.linspace(0.0, 1.0, seq_len)[:, None]
    w = (2.0 * np.pi / seq_len) * np.arange(seq_len)[:, None]
    bands = np.linspace(1e-4, FILTER_BANDS - 1, FILTER_BANDS)
    pos = np.concatenate([t, np.cos(bands * w), -np.sin(bands * w)], axis=-1)
    pe = _round_up(pos.shape[1], LANES)
    pos = np.pad(pos, ((0, 0), (0, pe - pos.shape[1])))
    pos = pos.reshape(seq_len // GRID_W, GRID_W, pe).transpose(1, 0, 2)
    deltas = np.abs(np.linspace(math.log(FILTER_TARGET) / SLOW_DECAY_PCT,
                                math.log(FILTER_TARGET) / FAST_DECAY_PCT, d))[None, :]
    return np.ascontiguousarray(pos, dtype=np.float32), np.ascontiguousarray(deltas, dtype=np.float32)


def kernel(x, c, ctx, c_ctx, ada_w, ada_b, norm_g, mlp_w1, mlp_w2, lru_w_in, lru_b_in, lru_conv_w, lru_conv_b, lru_w_a, lru_b_a, lru_w_i, lru_b_i, lru_lambda, lru_w_out, lru_b_out, hy_w_in, hy_b_in, hy_conv_w, hy_conv_b, hy_fw1, hy_fb1, hy_fw2, hy_fb2, hy_fw3, hy_fb3, hy_fw4, hy_freq, hy_skip, hy_w_out, hy_b_out, final_g):
    nb, seq, d = x.shape
    ctx_len = ctx.shape[1]
    w = lru_w_out.shape[1]
    lru_tile = NSEG * GRID_W
    assert nb + 1 <= SUBLANES and nb % 2 == 0
    assert seq % (BF16_ROWS * GRID_W) == 0 and ctx_len % (NSEG * SUBLANES) == 0

    cvec = jnp.concatenate([c, c_ctx[None, :], jnp.zeros((SUBLANES - nb - 1, d), F32)], axis=0)
    mod = _ada_call(cvec, ada_w, ada_b)

    mv_l, mv_c = _mod_rows(mod[0], nb, d)
    ng = norm_g[0, 0][None, :]
    wgate = lru_w_in[0, :, :w].astype(BF16)
    wrec = lru_w_in[0, :, w:].astype(BF16)
    bgate = lru_b_in[0, :w][None, :]
    brec = lru_b_in[0, w:][None, :]
    cw = lru_conv_w[0]
    cb = lru_conv_b[0][None, :]
    wg = (0.5 * jnp.concatenate([lru_w_a[0, 0], lru_w_i[0, 0], lru_w_a[0, 1], lru_w_i[0, 1]], axis=-1)).astype(BF16)
    bg = 0.5 * jnp.concatenate([lru_b_a[0, 0], lru_b_i[0, 0], lru_b_a[0, 1], lru_b_i[0, 1]], axis=-1)[:, None, :]
    lam = lru_lambda[0]
    lru_w = (ng, wrec, brec, cw, cb, wg, bg, lam)
    agg_c = _lru_pass1_call(ctx, mv_c, *lru_w, tile=ctx_len, wrap=True)
    agg_l, pq, xt = _lru_mix_call(x, mv_l, *lru_w, wgate, bgate, tile=lru_tile)
    hin = _segscan_call(agg_l, agg_c)
    x2 = _mlp_call(xt, mv_l, norm_g[0, 1][None, :], mlp_w1[0].astype(BF16), mlp_w2[0].astype(BF16), groups=NSEG,
                   lru=(pq, hin, lru_w_out[0].astype(BF16), lru_b_out[0][None, :]))

    mv1, _ = _mod_rows(mod[1], nb, d)
    v, xa, xb = _hyproj_call(x2, mv1, norm_g[1, 0][None, :], hy_w_in[0].astype(BF16), hy_b_in[0][None, :],
                             hy_conv_w[0], hy_conv_b[0][None, :], groups=BF16_ROWS)
    pos, deltas = (jnp.asarray(a) for a in _filter_constants(seq, d))
    fw1 = jnp.pad(hy_fw1[0], ((0, pos.shape[2] - hy_fw1.shape[1]), (0, 0)))
    twin = lambda wmat: jnp.kron(jnp.eye(2, dtype=F32), wmat)
    twice = lambda vec: jnp.tile(vec[None, :], (1, 2))
    hraw, nrm = _filter_call(pos, twin(fw1), twice(hy_fb1[0]), twin(hy_fw2[0]), twice(hy_fb2[0]),
                             twin(hy_fw3[0]), twice(hy_fb3[0]), twin(hy_fw4[0]).astype(BF16),
                             twice(hy_freq[0]), deltas, groups=BF16_ROWS)
    kf = _filtfft_call(hraw, nrm, seq)
    v1 = _longconv_call(v, xa, kf, 0, hy_skip[0, 0][None, :], seq)
    v2 = _longconv_call(v1, xb, kf, 1, hy_skip[0, 1][None, :], seq)
    return _mlp_call(x2, mv1, norm_g[1, 1][None, :], mlp_w1[1].astype(BF16), mlp_w2[1].astype(BF16),
                     groups=BF16_ROWS, hyena=(v2, hy_w_out[0].astype(BF16), hy_b_out[0][None, :]),
                     final_g=final_g[None, :])
```
